```python
import math
import jax
import jax.numpy as jnp
from jax import lax
import numpy as np

D_MODEL = 1024
BATCH = 4
SEQ = 8192
DEPTH = 2

SSD_HEADS = 8
SSD_HEAD_DIM = 64
SSD_INNER = SSD_HEADS * SSD_HEAD_DIM
SSD_GROUPS = 2
SSD_STATE = 64
SSD_CONV = 5
SSD_CHUNK = 128
SSD_CONV_CH = SSD_INNER + 2 * SSD_GROUPS * SSD_STATE
S5_GROUP_CH = 16
S5_GROUPS = 24
S5_WIDTH = S5_GROUPS * S5_GROUP_CH
S5_STATE = 64
S5_MAX_RE = -1e-4
RET_HEADS = 8
RET_HEAD_DIM = 64
RET_WIDTH = RET_HEADS * RET_HEAD_DIM
RET_CHUNK = 128
ROPE_BASE = 10000.0
N_BRANCH = 3
D_FF = 2816
EPS = 1e-6
IN_PROJ_SIZES = (SSD_INNER, SSD_CONV_CH, 2 * SSD_HEADS, S5_WIDTH,
                 RET_WIDTH, RET_WIDTH, RET_WIDTH, RET_WIDTH, N_BRANCH * D_MODEL)
D_IN_PROJ = sum(IN_PROJ_SIZES)

kernel_name = 'hybrid_ssd_s5_retention_macaron_encoder'


def rmsnorm(x, g):
    xf = x.astype(jnp.float32)
    y = xf * lax.rsqrt(jnp.mean(xf * xf, axis=-1, keepdims=True) + EPS)
    return (y * g.astype(jnp.float32)).astype(x.dtype)


def swiglu(x, w_gate, w_up, w_down):
    return (jax.nn.silu(x @ w_gate) * (x @ w_up)) @ w_down


def dwconv_centered(x, w, b):
    k, c = w.shape
    pad = k // 2
    y = lax.conv_general_dilated(x, w[:, None, :], window_strides=(1,), padding=[(pad, pad)],
                                 dimension_numbers=('NWC', 'WIO', 'NWC'), feature_group_count=c)
    return y + b


def segsum_exp(a):
    t = a.shape[-1]
    xa = jnp.broadcast_to(a[..., :, None], a.shape + (t,))
    xa = jnp.where(jnp.tril(jnp.ones((t, t), bool), -1), xa, 0.0)
    ss = jnp.cumsum(xa, axis=-2)
    return jnp.exp(jnp.where(jnp.tril(jnp.ones((t, t), bool)), ss, -jnp.inf))


def ssd_causal(x, dt, a, bm, cm):
    b, s, h, p = x.shape
    n = bm.shape[-1]
    L = SSD_CHUNK
    c = s // L
    xd = (x * dt[..., None]).reshape(b, c, L, h, p)
    da = (dt * a).reshape(b, c, L, h).transpose(0, 3, 1, 2)
    bm = bm.reshape(b, c, L, h, n)
    cm = cm.reshape(b, c, L, h, n)
    da_cs = jnp.cumsum(da, axis=-1)
    scores = jnp.einsum('bclhn,bcshn->bhcls', cm, bm) * segsum_exp(da)
    y_diag = jnp.einsum('bhcls,bcshp->bclhp', scores, xd)
    decay_states = jnp.exp(da_cs[..., -1:] - da_cs).transpose(0, 2, 3, 1)
    states = jnp.einsum('bclhn,bclhp->bchpn', bm * decay_states[..., None], xd)
    chunk_decay = segsum_exp(jnp.pad(da_cs[..., -1], ((0, 0), (0, 0), (1, 0))))
    states = jnp.concatenate([jnp.zeros_like(states[:, :1]), states], axis=1)
    states = jnp.einsum('bhzc,bchpn->bzhpn', chunk_decay, states)[:, :-1]
    out_decay = jnp.exp(da_cs).transpose(0, 2, 3, 1)
    y_off = jnp.einsum('bclhn,bchpn->bclhp', cm, states) * out_decay[..., None]
    return (y_diag + y_off).reshape(b, s, h, p)


def ssd_branch(z, xbc, dt_raw, conv_w, conv_b, dt_bias, a_log, d_skip, norm_g):
    b, s, _ = z.shape
    f32 = jnp.float32
    xbc = jax.nn.silu(dwconv_centered(xbc, conv_w, conv_b))
    xs, bm, cm = jnp.split(xbc, [SSD_INNER, SSD_INNER + SSD_GROUPS * SSD_STATE], axis=-1)
    rep = SSD_HEADS // SSD_GROUPS
    xs = xs.astype(f32).reshape(b, s, SSD_HEADS, SSD_HEAD_DIM)
    bm = jnp.repeat(bm.astype(f32).reshape(b, s, SSD_GROUPS, SSD_STATE), rep, axis=2)
    cm = jnp.repeat(cm.astype(f32).reshape(b, s, SSD_GROUPS, SSD_STATE), rep, axis=2)
    dt = jax.nn.softplus(dt_raw.astype(f32).reshape(b, s, 2, SSD_HEADS) + dt_bias.astype(f32))
    a = -jnp.exp(a_log.astype(f32))
    flip = lambda t: jnp.flip(t, axis=1)
    y_fwd = ssd_causal(xs, dt[:, :, 0], a[0], bm, cm)
    y_bwd = flip(ssd_causal(flip(xs), flip(dt[:, :, 1]), a[1], flip(bm), flip(cm)))
    y = y_fwd + y_bwd + d_skip.astype(f32)[:, None] * xs
    y = y.reshape(b, s, SSD_INNER).astype(z.dtype)
    return rmsnorm(y * jax.nn.silu(z), norm_g)


def s5_direction(u, lam_re, lam_im, log_step, b_re, b_im, reverse):
    f32 = jnp.float32
    lam_re = jnp.minimum(lam_re.astype(f32), S5_MAX_RE)
    lam_im = lam_im.astype(f32)
    step = jnp.exp(log_step.astype(f32))[:, None]
    mag = jnp.exp(lam_re * step)
    ang = lam_im * step
    lb_re = mag * jnp.cos(ang)
    lb_im = mag * jnp.sin(ang)
    den = lam_re * lam_re + lam_im * lam_im
    nr = lb_re - 1.0
    coef_re = ((nr * lam_re + lb_im * lam_im) / den)[..., None]
    coef_im = ((lb_im * lam_re - nr * lam_im) / den)[..., None]
    b_re = b_re.astype(f32)
    b_im = b_im.astype(f32)
    bb_re = coef_re * b_re - coef_im * b_im
    bb_im = coef_re * b_im + coef_im * b_re
    bu_re = jnp.einsum('gph,bsgh->bsgp', bb_re, u)
    bu_im = jnp.einsum('gph,bsgh->bsgp', bb_im, u)
    s = u.shape[1]
    a_re = jnp.broadcast_to(lb_re, (1, s) + lb_re.shape)
    a_im = jnp.broadcast_to(lb_im, (1, s) + lb_im.shape)

    def combine(e1, e2):
        a1r, a1i, b1r, b1i = e1
        a2r, a2i, b2r, b2i = e2
        return (a2r * a1r - a2i * a1i,
                a2r * a1i + a2i * a1r,
                a2r * b1r - a2i * b1i + b2r,
                a2r * b1i + a2i * b1r + b2i)

    _, _, h_re, h_im = lax.associative_scan(combine, (a_re, a_im, bu_re, bu_im), reverse=reverse, axis=1)
    return h_re, h_im


def s5_branch(u, lam_re, lam_im, log_step, b_re, b_im, c_re, c_im, d_s5, glu_wv, glu_wg):
    b, s, _ = u.shape
    f32 = jnp.float32
    uf = u.astype(f32).reshape(b, s, S5_GROUPS, S5_GROUP_CH)
    y = d_s5.astype(f32) * uf
    for direction, rev in ((0, False), (1, True)):
        h_re, h_im = s5_direction(uf, lam_re[direction], lam_im[direction], log_step[direction],
                                  b_re, b_im, rev)
        y = y + jnp.einsum('ghp,bsgp->bsgh', c_re[direction].astype(f32), h_re) \
              - jnp.einsum('ghp,bsgp->bsgh', c_im[direction].astype(f32), h_im)
    y = jax.nn.gelu(y.reshape(b, s, S5_WIDTH)).astype(u.dtype)
    return (y @ glu_wv) * jax.nn.sigmoid(y @ glu_wg)


def rotary(t, cos, sin):
    t1, t2 = jnp.split(t, 2, axis=-1)
    return jnp.concatenate([t1 * cos - t2 * sin, t1 * sin + t2 * cos], axis=-1)


def retention_direction(q, k, v, log_gamma, inclusive):
    b, s, h, d = q.shape
    L = RET_CHUNK
    c = s // L
    q = q.reshape(b, c, L, h, d)
    k = k.reshape(b, c, L, h, d)
    v = v.reshape(b, c, L, h, -1)
    idx = jnp.arange(L, dtype=jnp.float32)
    rel = idx[:, None] - idx[None, :]
    mask = jnp.tril(jnp.ones((L, L), bool), 0 if inclusive else -1)
    intra_decay = jnp.where(mask, jnp.exp(log_gamma[:, None, None] * jnp.where(mask, rel, 0.0)), 0.0)
    scores = jnp.einsum('bclhd,bcshd->bhcls', q, k) * intra_decay[:, None]
    y_intra = jnp.einsum('bhcls,bcshe->bclhe', scores, v)
    k_decay = jnp.exp(log_gamma[None, :] * (L - 1.0 - idx)[:, None])
    r = jnp.einsum('bclhd,bclhe->bchde', k * k_decay[:, :, None], v)
    ci = jnp.arange(c, dtype=jnp.float32)
    crel = ci[:, None] - ci[None, :] - 1.0
    cmask = jnp.tril(jnp.ones((c, c), bool), -1)
    chunk_decay = jnp.where(cmask, jnp.exp(log_gamma[:, None, None] * L * jnp.where(cmask, crel, 0.0)), 0.0)
    st = jnp.einsum('hij,bjhde->bihde', chunk_decay, r)
    q_decay = jnp.exp(log_gamma[None, :] * (idx + 1.0)[:, None])
    y_inter = jnp.einsum('bclhd,bchde->bclhe', q * q_decay[:, :, None], st)
    return (y_intra + y_inter).reshape(b, s, h, -1)


def retention_branch(q, k, v, g, norm_g):
    b, s, _ = q.shape
    f32 = jnp.float32
    shape = (b, s, RET_HEADS, RET_HEAD_DIM)
    pos = jnp.arange(s, dtype=f32)
    inv_freq = ROPE_BASE ** (-jnp.arange(0, RET_HEAD_DIM, 2, dtype=f32) / RET_HEAD_DIM)
    ang = pos[:, None] * inv_freq[None, :]
    cos = jnp.cos(ang)[None, :, None, :]
    sin = jnp.sin(ang)[None, :, None, :]
    qf = rotary(q.astype(f32).reshape(shape), cos, sin)
    kf = rotary(k.astype(f32).reshape(shape), cos, sin) * (RET_HEAD_DIM ** -0.5)
    vf = v.astype(f32).reshape(shape)
    log_gamma = jnp.log1p(-jnp.exp2(-5.0 - jnp.arange(RET_HEADS, dtype=f32)))
    flip = lambda t: jnp.flip(t, axis=1)
    y = retention_direction(qf, kf, vf, log_gamma, True) \
        + flip(retention_direction(flip(qf), flip(kf), flip(vf), log_gamma, False))
    y = y * lax.rsqrt(jnp.mean(y * y, axis=-1, keepdims=True) + EPS)
    y = y.reshape(b, s, RET_WIDTH) * norm_g.astype(f32)
    return (jax.nn.silu(g.astype(f32)) * y).astype(g.dtype)


def setup_inputs(seed: int = 0) -> dict:
    key = jax.random.key(seed)
    ks = iter(jax.random.split(key, 48))
    f32 = jnp.float32
    L = DEPTH
    D = D_MODEL

    def nrm(shape, scale):
        return scale * jax.random.normal(next(ks), shape, f32)

    def gain(shape):
        return 1.0 + nrm(shape, 0.02)

    x = jax.random.normal(next(ks), (BATCH, SEQ, D), f32)
    ffn1_norm = gain((L, D))
    ffn1_w_gate = nrm((L, D, D_FF), D ** -0.5)
    ffn1_w_up = nrm((L, D, D_FF), D ** -0.5)
    ffn1_w_down = nrm((L, D_FF, D), D_FF ** -0.5)
    mix_norm = gain((L, D))
    w_in = nrm((L, D, D_IN_PROJ), D ** -0.5)
    b_gate = nrm((L, N_BRANCH * D), 0.02)
    ssd_conv_w = nrm((L, SSD_CONV, SSD_CONV_CH), SSD_CONV ** -0.5)
    ssd_conv_b = nrm((L, SSD_CONV_CH), 0.02)
    dt0 = jnp.exp(jax.random.uniform(next(ks), (L, 2, SSD_HEADS), f32, math.log(1e-3), math.log(1e-1)))
    ssd_dt_bias = dt0 + jnp.log(-jnp.expm1(-dt0))
    ssd_a_log = jnp.log(jax.random.uniform(next(ks), (L, 2, SSD_HEADS), f32, 1.0, 16.0))
    ssd_d = 1.0 + nrm((L, SSD_HEADS), 0.1)
    ssd_norm = gain((L, SSD_INNER))
    w_br_ssd = nrm((L, SSD_INNER, D), SSD_INNER ** -0.5)
    s5_lam_re = -0.5 + nrm((L, 2, S5_GROUPS, S5_STATE), 0.01)
    s5_lam_im = jnp.pi * jnp.arange(S5_STATE, dtype=f32) + nrm((L, 2, S5_GROUPS, S5_STATE), 0.01)
    s5_log_step = jax.random.uniform(next(ks), (L, 2, S5_GROUPS), f32, math.log(1e-3), math.log(1e-1))
    s5_b_re = nrm((L, S5_GROUPS, S5_STATE, S5_GROUP_CH), (2 * S5_GROUP_CH) ** -0.5)
    s5_b_im = nrm((L, S5_GROUPS, S5_STATE, S5_GROUP_CH), (2 * S5_GROUP_CH) ** -0.5)
    s5_c_re = nrm((L, 2, S5_GROUPS, S5_GROUP_CH, S5_STATE), (2 * S5_STATE) ** -0.5)
    s5_c_im = nrm((L, 2, S5_GROUPS, S5_GROUP_CH, S5_STATE), (2 * S5_STATE) ** -0.5)
    s5_d = nrm((L, S5_GROUPS, S5_GROUP_CH), 1.0)
    s5_glu_wv = nrm((L, S5_WIDTH, S5_WIDTH), S5_WIDTH ** -0.5)
    s5_glu_wg = nrm((L, S5_WIDTH, S5_WIDTH), S5_WIDTH ** -0.5)
    w_br_s5 = nrm((L, S5_WIDTH, D), S5_WIDTH ** -0.5)
    ret_norm = gain((L, RET_WIDTH))
    w_br_ret = nrm((L, RET_WIDTH, D), RET_WIDTH ** -0.5)
    w_out = nrm((L, D, D), D ** -0.5)
    ffn2_norm = gain((L, D))
    ffn2_w_gate = nrm((L, D, D_FF), D ** -0.5)
    ffn2_w_up = nrm((L, D, D_FF), D ** -0.5)
    ffn2_w_down = nrm((L, D_FF, D), D_FF ** -0.5)
    final_norm = gain((D,))
    return {'x': x,
            'ffn1_norm': ffn1_norm, 'ffn1_w_gate': ffn1_w_gate, 'ffn1_w_up': ffn1_w_up, 'ffn1_w_down': ffn1_w_down,
            'mix_norm': mix_norm, 'w_in': w_in, 'b_gate': b_gate,
            'ssd_conv_w': ssd_conv_w, 'ssd_conv_b': ssd_conv_b, 'ssd_dt_bias': ssd_dt_bias,
            'ssd_a_log': ssd_a_log, 'ssd_d': ssd_d, 'ssd_norm': ssd_norm, 'w_br_ssd': w_br_ssd,
            's5_lam_re': s5_lam_re, 's5_lam_im': s5_lam_im, 's5_log_step': s5_log_step,
            's5_b_re': s5_b_re, 's5_b_im': s5_b_im, 's5_c_re': s5_c_re, 's5_c_im': s5_c_im,
            's5_d': s5_d, 's5_glu_wv': s5_glu_wv, 's5_glu_wg': s5_glu_wg, 'w_br_s5': w_br_s5,
            'ret_norm': ret_norm, 'w_br_ret': w_br_ret,
            'w_out': w_out,
            'ffn2_norm': ffn2_norm, 'ffn2_w_gate': ffn2_w_gate, 'ffn2_w_up': ffn2_w_up, 'ffn2_w_down': ffn2_w_down,
            'final_norm': final_norm}


def reference(x,
              ffn1_norm, ffn1_w_gate, ffn1_w_up, ffn1_w_down,
              mix_norm, w_in, b_gate,
              ssd_conv_w, ssd_conv_b, ssd_dt_bias, ssd_a_log, ssd_d, ssd_norm, w_br_ssd,
              s5_lam_re, s5_lam_im, s5_log_step, s5_b_re, s5_b_im, s5_c_re, s5_c_im,
              s5_d, s5_glu_wv, s5_glu_wg, w_br_s5,
              ret_norm, w_br_ret,
              w_out,
              ffn2_norm, ffn2_w_gate, ffn2_w_up, ffn2_w_down,
              final_norm):
    b, s, d = x.shape
    split_pts = np.cumsum(IN_PROJ_SIZES)[:-1].tolist()
    for i in range(DEPTH):
        x = x + 0.5 * swiglu(rmsnorm(x, ffn1_norm[i]), ffn1_w_gate[i], ffn1_w_up[i], ffn1_w_down[i])
        h = rmsnorm(x, mix_norm[i])
        proj = h @ w_in[i]
        z, xbc, dt_raw, u, q, k, v, g, gate_logits = jnp.split(proj, split_pts, axis=-1)
        y_ssd = ssd_branch(z, xbc, dt_raw, ssd_conv_w[i], ssd_conv_b[i], ssd_dt_bias[i],
                           ssd_a_log[i], ssd_d[i], ssd_norm[i]) @ w_br_ssd[i]
        y_s5 = s5_branch(u, s5_lam_re[i], s5_lam_im[i], s5_log_step[i], s5_b_re[i], s5_b_im[i],
                         s5_c_re[i], s5_c_im[i], s5_d[i], s5_glu_wv[i], s5_glu_wg[i]) @ w_br_s5[i]
        y_ret = retention_branch(q, k, v, g, ret_norm[i]) @ w_br_ret[i]
        gates = jax.nn.sigmoid(gate_logits + b_gate[i]).reshape(b, s, N_BRANCH, d)
        mixed = gates[:, :, 0] * y_ssd + gates[:, :, 1] * y_s5 + gates[:, :, 2] * y_ret
        x = x + mixed @ w_out[i]
        x = x + 0.5 * swiglu(rmsnorm(x, ffn2_norm[i]), ffn2_w_gate[i], ffn2_w_up[i], ffn2_w_down[i])
    return rmsnorm(x, final_norm)
```

```python
import functools
import math

import jax
import jax.numpy as jnp
from jax import lax
from jax.experimental import pallas as pl
from jax.experimental.pallas import tpu as pltpu

F32 = jnp.float32
BF16 = jnp.bfloat16

EPS = 1e-6
SSD_HEADS = 8
SSD_HEAD_DIM = 64
SSD_INNER = 512
SSD_GROUPS = 2
SSD_STATE = 64
SSD_CONV = 5
SSD_CONV_CH = 768
S5_GROUP_CH = 16
S5_GROUPS = 24
S5_WIDTH = 384
S5_STATE = 64
S5_MAX_RE = -1e-4
RET_HEADS = 8
RET_HEAD_DIM = 64
RET_WIDTH = 512
ROPE_BASE = 10000.0
N_BRANCH = 3
IN_PROJ_SIZES = (512, 768, 16, 384, 512, 512, 512, 512, 3072)

V7X_VMEM_BYTES = 64 * 1024 * 1024
VMEM_LIMIT = V7X_VMEM_BYTES - 8 * 1024 * 1024
LANES = 128
BF16_SUBLANES = 16

TOKEN_TILE = 512
SSD_CHUNK = 128
RET_CHUNK = 128
S5_CHUNK = 32
CONV_TILE = 512
DT_TILE = 2048


def _cparams(*sem):
    return pltpu.CompilerParams(dimension_semantics=sem, vmem_limit_bytes=VMEM_LIMIT)


def _resident(shape):
    nd = len(shape)
    return pl.BlockSpec(shape, lambda *_: (0,) * nd, pipeline_mode=pl.Buffered(1))


def _rms(x, g):
    return (x * lax.rsqrt(jnp.mean(x * x, axis=-1, keepdims=True) + EPS)) * g


def _silu(x):
    return x * jax.nn.sigmoid(x)


def _dot(a, b):
    return jnp.dot(a, b, preferred_element_type=F32)


def _dot_nt(a, b):
    return lax.dot_general(a, b, (((1,), (1,)), ((), ())), preferred_element_type=F32)


def _dot_tn(a, b):
    return lax.dot_general(a, b, (((0,), (0,)), ((), ())), preferred_element_type=F32)


def _split3(v):
    hi = v.astype(BF16)
    r1 = v - hi.astype(F32)
    mid = r1.astype(BF16)
    lo = (r1 - mid.astype(F32)).astype(BF16)
    return hi, mid, lo


def _expand_heads(v, e):
    hi, mid, lo = _split3(v)
    return _dot(hi, e) + _dot(mid, e) + _dot(lo, e)


def _ffn_kernel(x_ref, g_ref, wg_ref, wu_ref, wd_ref, fg_ref, o_ref, *, final):
    x = x_ref[...]
    h = _rms(x, g_ref[...]).astype(BF16)
    a = _dot(h, wg_ref[...])
    b = _dot(h, wu_ref[...])
    t = (_silu(a) * b).astype(BF16)
    y = x + 0.5 * _dot(t, wd_ref[...])
    if final:
        y = _rms(y, fg_ref[...])
    o_ref[...] = y


def _ffn(x, g, wg, wu, wd, fg, final):
    n, d = x.shape
    f = wg.shape[1]
    tm = TOKEN_TILE
    row = pl.BlockSpec((tm, d), lambda i: (i, 0))
    return pl.pallas_call(
        functools.partial(_ffn_kernel, final=final),
        grid=(n // tm,),
        in_specs=[row, _resident((1, d)), _resident((d, f)), _resident((d, f)), _resident((f, d)),
                  _resident((1, d))],
        out_specs=row,
        out_shape=jax.ShapeDtypeStruct((n, d), F32),
        compiler_params=_cparams("parallel"),
        name="ffn",
    )(x, g.reshape(1, d), wg, wu, wd, fg.reshape(1, d))


def _inproj_kernel(x_ref, g_ref, *refs):
    nw = len(refs) // 2
    h = _rms(x_ref[...], g_ref[...]).astype(BF16)
    for w_ref, o_ref in zip(refs[:nw], refs[nw:]):
        o_ref[...] = _dot(h, w_ref[...]).astype(o_ref.dtype)


def _inproj(x, g, weights, out_dtypes):
    n, d = x.shape
    tm = TOKEN_TILE
    row = lambda c: pl.BlockSpec((tm, c), lambda i: (i, 0))
    return pl.pallas_call(
        _inproj_kernel,
        grid=(n // tm,),
        in_specs=[row(d), _resident((1, d))] + [_resident(w.shape) for w in weights],
        out_specs=[row(w.shape[1]) for w in weights],
        out_shape=[jax.ShapeDtypeStruct((n, w.shape[1]), dt) for w, dt in zip(weights, out_dtypes)],
        compiler_params=_cparams("parallel"),
        name="inproj",
    )(x, g.reshape(1, d), *weights)


def _ssd_conv_kernel(prev_ref, x_ref, next_ref, w_ref, b_ref, xs_ref, bm_ref, cm_ref, *, tiles_per_seq):
    i = pl.program_id(0)
    t = x_ref.shape[0]
    first = (i % tiles_per_seq) == 0
    last = (i % tiles_per_seq) == tiles_per_seq - 1
    x = x_ref[...].astype(F32)
    p = jnp.where(first, 0.0, prev_ref[...].astype(F32))
    nx = jnp.where(last, 0.0, next_ref[...].astype(F32))
    ext = jnp.concatenate([p[8:16], x, nx[0:8]], axis=0)
    rows = t + 16
    acc = jnp.broadcast_to(b_ref[...], x.shape)
    half = SSD_CONV // 2
    for k in range(SSD_CONV):
        shifted = pltpu.roll(ext, (half - k) % rows, 0)
        acc = acc + w_ref[k:k + 1, :] * shifted[8:8 + t]
    y = _silu(acc)
    xs_ref[...] = y[:, :SSD_INNER].astype(xs_ref.dtype)
    bm_ref[...] = y[:, SSD_INNER:SSD_INNER + LANES].astype(bm_ref.dtype)
    cm_ref[...] = y[:, SSD_INNER + LANES:].astype(cm_ref.dtype)


def _ssd_conv(xbc, w, b, seq):
    n, c = xbc.shape
    t = CONV_TILE
    hb = BF16_SUBLANES
    nblk = n // hb
    return pl.pallas_call(
        functools.partial(_ssd_conv_kernel, tiles_per_seq=seq // t),
        grid=(n // t,),
        in_specs=[pl.BlockSpec((hb, c), lambda i: (jnp.maximum(i * (t // hb) - 1, 0), 0)),
                  pl.BlockSpec((t, c), lambda i: (i, 0)),
                  pl.BlockSpec((hb, c), lambda i: (jnp.minimum((i + 1) * (t // hb), nblk - 1), 0)),
                  _resident(w.shape), _resident((1, c))],
        out_specs=[pl.BlockSpec((t, SSD_INNER), lambda i: (i, 0)),
                   pl.BlockSpec((t, LANES), lambda i: (i, 0)),
                   pl.BlockSpec((t, LANES), lambda i: (i, 0))],
        out_shape=[jax.ShapeDtypeStruct((n, SSD_INNER), BF16),
                   jax.ShapeDtypeStruct((n, LANES), BF16),
                   jax.ShapeDtypeStruct((n, LANES), BF16)],
        compiler_params=_cparams("parallel"),
        name="ssd_conv",
    )(xbc, xbc, xbc, w, b.reshape(1, c))


def _ssd_dt_kernel(raw_ref, bias_ref, a_ref, dt_ref, cs_ref, *, chunk):
    x = raw_ref[...] + bias_ref[...]
    dt = jnp.maximum(x, 0.0) + jnp.log1p(jnp.exp(-jnp.abs(x)))
    cs = dt * a_ref[...]
    t = x.shape[1]
    pos = lax.broadcasted_iota(jnp.int32, x.shape, 1) % chunk
    fwd = lax.broadcasted_iota(jnp.int32, x.shape, 0) < SSD_HEADS
    k = 1
    while k < chunk:
        before = jnp.where(pos >= k, pltpu.roll(cs, k, 1), 0.0)
        after = jnp.where(pos < chunk - k, pltpu.roll(cs, t - k, 1), 0.0)
        cs = cs + jnp.where(fwd, before, after)
        k *= 2
    dt_ref[...] = dt
    cs_ref[...] = cs


def _ssd_dt(raw_t, bias, a):
    r, n = raw_t.shape
    t = DT_TILE
    blk = pl.BlockSpec((r, t), lambda i: (0, i))
    return pl.pallas_call(
        functools.partial(_ssd_dt_kernel, chunk=SSD_CHUNK),
        grid=(n // t,),
        in_specs=[blk, _resident((r, 1)), _resident((r, 1))],
        out_specs=[blk, blk],
        out_shape=[jax.ShapeDtypeStruct((r, n), F32)] * 2,
        compiler_params=_cparams("parallel"),
        name="ssd_dt",
    )(raw_t, bias.reshape(r, 1), a.reshape(r, 1))


def _ssd_chunk_state(xs, bm, wexp):
    xw = (xs.astype(F32) * wexp).astype(BF16)
    gw = SSD_INNER // SSD_GROUPS
    parts = [_dot_tn(bm[:, g * SSD_STATE:(g + 1) * SSD_STATE], xw[:, g * gw:(g + 1) * gw])
             for g in range(SSD_GROUPS)]
    return jnp.concatenate(parts, axis=1)


def _ssd_state_kernel(xs_f, bm_f, dtc_f, csc_f, xs_b, bm_b, dtc_b, csc_b, e_ref,
                      sf_out, sb_out, sf, sb):
    c = pl.program_id(1)
    h = SSD_HEADS
    l = xs_f.shape[0]

    @pl.when(c == 0)
    def _():
        sf[...] = jnp.zeros_like(sf)
        sb[...] = jnp.zeros_like(sb)

    sf_out[0, 0] = sf[...]
    sb_out[0, 0] = sb[...]

    cs_f = csc_f[...]
    tot_f = cs_f[l - 1:l, :]
    w_f = jnp.exp(tot_f - cs_f) * dtc_f[...]
    cs_b = csc_b[...]
    tot_b = cs_b[0:1, :]
    w_b = jnp.exp(tot_b - cs_b) * dtc_b[...]
    lane = lax.broadcasted_iota(jnp.int32, (1, 2 * h), 1)
    e_f = e_ref[:, :SSD_INNER]
    e_b = e_ref[:, SSD_INNER:]
    wexp_f = _expand_heads(jnp.where(lane < h, w_f, 0.0), e_f)
    wexp_b = _expand_heads(jnp.where(lane >= h, w_b, 0.0), e_b)
    dec_f = _expand_heads(jnp.broadcast_to(jnp.exp(tot_f), (8, 2 * h)), e_f)[0:1]
    dec_b = _expand_heads(jnp.broadcast_to(jnp.exp(tot_b), (8, 2 * h)), e_b)[0:1]
    sf[...] = sf[...] * dec_f + _ssd_chunk_state(xs_f[...], bm_f[...], wexp_f)
    sb[...] = sb[...] * dec_b + _ssd_chunk_state(xs_b[...], bm_b[...], wexp_b)


def _ssd_states(xs, bm, dt_c, cs_c, e, batch, seq):
    l = SSD_CHUNK
    nc = seq // l
    fwd = lambda w: pl.BlockSpec((l, w), lambda b, c: (b * nc + c, 0))
    bwd = lambda w: pl.BlockSpec((l, w), lambda b, c: (b * nc + nc - 1 - c, 0))
    st = (batch, nc, SSD_STATE, SSD_INNER)
    return pl.pallas_call(
        _ssd_state_kernel,
        grid=(batch, nc),
        in_specs=[fwd(SSD_INNER), fwd(LANES), fwd(16), fwd(16),
                  bwd(SSD_INNER), bwd(LANES), bwd(16), bwd(16), _resident(e.shape)],
        out_specs=[pl.BlockSpec((1, 1, SSD_STATE, SSD_INNER), lambda b, c: (b, c, 0, 0)),
                   pl.BlockSpec((1, 1, SSD_STATE, SSD_INNER), lambda b, c: (b, nc - 1 - c, 0, 0))],
        out_shape=[jax.ShapeDtypeStruct(st, F32)] * 2,
        scratch_shapes=[pltpu.VMEM((SSD_STATE, SSD_INNER), F32)] * 2,
        compiler_params=_cparams("parallel", "arbitrary"),
        name="ssd_states",
    )(xs, bm, dt_c, cs_c, xs, bm, dt_c, cs_c, e)


def _ssd_out_kernel(xs_ref, bm_ref, cm_ref, z_ref, dtc_ref, csc_ref, dtr_ref, csr_ref,
                    sf_ref, sb_ref, e_ref, dskip_ref, ng_ref, o_ref):
    h = SSD_HEADS
    l = xs_ref.shape[0]
    hd = SSD_HEAD_DIM
    xs = xs_ref[...]
    bm = bm_ref[...]
    cm = cm_ref[...]
    csc = csc_ref[...]
    csr = csr_ref[...]
    dtr = dtr_ref[...]
    ri = lax.broadcasted_iota(jnp.int32, (l, l), 0)
    ci = lax.broadcasted_iota(jnp.int32, (l, l), 1)
    causal = ri >= ci
    anti = ci >= ri
    neg = -1e30
    ys = []
    for g in range(SSD_GROUPS):
        sl = slice(g * SSD_STATE, (g + 1) * SSD_STATE)
        scores = _dot_nt(cm[:, sl], bm[:, sl])
        for hh in range(g * (h // SSD_GROUPS), (g + 1) * (h // SSD_GROUPS)):
            ef = jnp.exp(jnp.where(causal, csc[:, hh:hh + 1] - csr[hh:hh + 1, :], neg))
            eb = jnp.exp(jnp.where(anti, csc[:, h + hh:h + hh + 1] - csr[h + hh:h + hh + 1, :], neg))
            w = ef * dtr[hh:hh + 1, :] + eb * dtr[h + hh:h + hh + 1, :]
            m = (scores * w).astype(BF16)
            ys.append(_dot(m, xs[:, hh * hd:(hh + 1) * hd]))
    y = jnp.concatenate(ys, axis=1)
    eo = _expand_heads(jnp.exp(csc), e_ref[...])
    gw = SSD_INNER // SSD_GROUPS
    off_f = []
    off_b = []
    for g in range(SSD_GROUPS):
        cg = cm[:, g * SSD_STATE:(g + 1) * SSD_STATE]
        off_f.append(_dot(cg, sf_ref[0, 0, :, g * gw:(g + 1) * gw].astype(BF16)))
        off_b.append(_dot(cg, sb_ref[0, 0, :, g * gw:(g + 1) * gw].astype(BF16)))
    y = y + jnp.concatenate(off_f, axis=1) * eo[:, :SSD_INNER] + jnp.concatenate(off_b, axis=1) * eo[:, SSD_INNER:]
    y = y + dskip_ref[...] * xs.astype(F32)
    y = y * _silu(z_ref[...].astype(F32))
    o_ref[...] = _rms(y, ng_ref[...]).astype(o_ref.dtype)


def _ssd_out(xs, bm, cm, z, dt_c, cs_c, dt_r, cs_r, sf, sb, e, dskip, ng, batch, seq):
    n = xs.shape[0]
    l = SSD_CHUNK
    nc = seq // l
    row = lambda w: pl.BlockSpec((l, w), lambda b, c: (b * nc + c, 0))
    col = pl.BlockSpec((16, l), lambda b, c: (0, b * nc + c))
    st = pl.BlockSpec((1, 1, SSD_STATE, SSD_INNER), lambda b, c: (b, c, 0, 0))
    return pl.pallas_call(
        _ssd_out_kernel,
        grid=(batch, nc),
        in_specs=[row(SSD_INNER), row(LANES), row(LANES), row(SSD_INNER), row(16), row(16), col, col,
                  st, st, _resident(e.shape), _resident((1, SSD_INNER)), _resident((1, SSD_INNER))],
        out_specs=row(SSD_INNER),
        out_shape=jax.ShapeDtypeStruct((n, SSD_INNER), BF16),
        compiler_params=_cparams("parallel", "parallel"),
        name="ssd_out",
    )(xs, bm, cm, z, dt_c, cs_c, dt_r, cs_r, sf, sb, e, dskip, ng)


def _ssd_branch(z, xbc, dt_raw, conv_w, conv_b, dt_bias, a_log, d_skip, norm_g, e, batch, seq):
    xs, bm, cm = _ssd_conv(xbc, conv_w, conv_b, seq)
    a = -jnp.exp(a_log.astype(F32)).reshape(2 * SSD_HEADS)
    dt_r, cs_r = _ssd_dt(dt_raw[:, :2 * SSD_HEADS].T, dt_bias.reshape(2 * SSD_HEADS), a)
    dt_c = dt_r.T
    cs_c = cs_r.T
    sf, sb = _ssd_states(xs, bm, dt_c, cs_c, e, batch, seq)
    dskip = jnp.repeat(d_skip.astype(F32), SSD_HEAD_DIM).reshape(1, SSD_INNER)
    return _ssd_out(xs, bm, cm, z, dt_c, cs_c, dt_r, cs_r, sf, sb, e, dskip,
                    norm_g.reshape(1, SSD_INNER), batch, seq)


def _s5_matrices(lam_re, lam_im, log_step, b_re, b_im, c_re, c_im, d_s5):
    q = S5_CHUNK
    hi = lax.Precision.HIGHEST
    lr = jnp.minimum(lam_re.astype(F32), S5_MAX_RE)
    li = lam_im.astype(F32)
    step = jnp.exp(log_step.astype(F32))[..., None]
    tau = jnp.arange(q + 1, dtype=F32)[:, None, None, None]
    mag = jnp.exp(lr * step * tau)
    ang = li * step * tau
    pr = mag * jnp.cos(ang)
    pi = mag * jnp.sin(ang)
    den = lr * lr + li * li
    nr = pr[1] - 1.0
    coef_re = ((nr * lr + pi[1] * li) / den)[..., None]
    coef_im = ((pi[1] * lr - nr * li) / den)[..., None]
    br = b_re.astype(F32)
    bi = b_im.astype(F32)
    bb_re = coef_re * br - coef_im * bi
    bb_im = coef_re * bi + coef_im * br
    cr = c_re.astype(F32)
    ci = c_im.astype(F32)
    cp_re = cr[None] * pr[:, :, :, None, :] - ci[None] * pi[:, :, :, None, :]
    cp_im = cr[None] * pi[:, :, :, None, :] + ci[None] * pr[:, :, :, None, :]
    kern = (jnp.einsum('tdgip,dgpj->tdgij', cp_re, bb_re, precision=hi)
            - jnp.einsum('tdgip,dgpj->tdgij', cp_im, bb_im, precision=hi))
    ar = jnp.arange(q)
    diff = ar[None, :] - ar[:, None]
    kf = kern[:q, 0][jnp.abs(diff)]
    kb = kern[:q, 1][jnp.abs(diff)]
    eye = jnp.eye(S5_GROUP_CH, dtype=F32) * d_s5.astype(F32)[:, :, None]
    dm = diff[:, :, None, None, None]
    toe = jnp.where(dm > 0, kf, jnp.where(dm < 0, kb, kf + kb + eye[None, None]))
    toe = toe.transpose(2, 0, 4, 1, 3).reshape(S5_GROUPS, q * S5_GROUP_CH, q * S5_GROUP_CH)
    pf_r, pf_i = pr[:q, 0][::-1], pi[:q, 0][::-1]
    pb_r, pb_i = pr[:q, 1], pi[:q, 1]

    def to_state(p_r, p_i, d):
        re = p_r[..., None] * bb_re[d][None] - p_i[..., None] * bb_im[d][None]
        im = p_r[..., None] * bb_im[d][None] + p_i[..., None] * bb_re[d][None]
        return jnp.concatenate([re, im], axis=2)

    bmat = jnp.concatenate([to_state(pf_r, pf_i, 0), to_state(pb_r, pb_i, 1)], axis=2)
    bmat = bmat.transpose(1, 0, 3, 2).reshape(S5_GROUPS, q * S5_GROUP_CH, 4 * S5_STATE)
    cf_r, cf_i = cp_re[1:q + 1, 0], cp_im[1:q + 1, 0]
    cb_r, cb_i = cp_re[1:q + 1, 1][::-1], cp_im[1:q + 1, 1][::-1]
    cmat = jnp.concatenate([cf_r, -cf_i, cb_r, -cb_i], axis=3)
    cmat = cmat.transpose(1, 3, 0, 2).reshape(S5_GROUPS, 4 * S5_STATE, q * S5_GROUP_CH)
    aq_r, aq_i = pr[q], pi[q]
    a_mul = jnp.concatenate([aq_r[0], aq_r[0], aq_r[1], aq_r[1]], axis=-1)[:, None, :]
    a_swp = jnp.concatenate([-aq_i[0], aq_i[0], -aq_i[1], aq_i[1]], axis=-1)[:, None, :]
    return toe.astype(BF16), bmat.astype(BF16), cmat.astype(BF16), a_mul, a_swp


def _s5_state_kernel(u_ref, b_ref, am_ref, as_ref, h_ref, loc, *, batch, nc):
    p2 = 2 * S5_STATE
    loc[...] = _dot(u_ref[0], b_ref[0])
    am = am_ref[0]
    asw = as_ref[0]
    am_f, am_b = am[:, :p2], am[:, p2:]
    as_f, as_b = asw[:, :p2], asw[:, p2:]

    rows = 8

    def step(j, carry):
        out = []
        for b in range(batch):
            hf, hb = carry[2 * b], carry[2 * b + 1]
            rf = pl.multiple_of(b * nc + j * rows, rows)
            rb = pl.multiple_of(b * nc + nc - rows - j * rows, rows)
            lf = loc[pl.ds(rf, rows), :p2]
            lb = loc[pl.ds(rb, rows), p2:]
            ent_f = []
            ent_b = []
            for i in range(rows):
                ent_f.append(hf)
                hf = am_f * hf + as_f * pltpu.roll(hf, S5_STATE, 1) + lf[i:i + 1]
                ent_b.append(hb)
                hb = am_b * hb + as_b * pltpu.roll(hb, S5_STATE, 1) + lb[rows - 1 - i:rows - i]
            h_ref[0, pl.ds(rf, rows), :p2] = jnp.concatenate(ent_f, axis=0)
            h_ref[0, pl.ds(rb, rows), p2:] = jnp.concatenate(ent_b[::-1], axis=0)
            out += [hf, hb]
        return tuple(out)

    zero = jnp.zeros((1, p2), F32)
    lax.fori_loop(0, nc // rows, step, (zero,) * (2 * batch))


def _s5_states(u_g, bmat, a_mul, a_swp, batch):
    g, r, k = u_g.shape
    w = bmat.shape[2]
    blk = lambda s: pl.BlockSpec((1,) + s, lambda i: (i, 0, 0))
    return pl.pallas_call(
        functools.partial(_s5_state_kernel, batch=batch, nc=r // batch),
        grid=(g,),
        in_specs=[blk((r, k)), blk((k, w)), blk((1, w)), blk((1, w))],
        out_specs=blk((r, w)),
        out_shape=jax.ShapeDtypeStruct((g, r, w), F32),
        scratch_shapes=[pltpu.VMEM((r, w), F32)],
        compiler_params=_cparams("parallel"),
        name="s5_states",
    )(u_g, bmat, a_mul, a_swp)


def _s5_out_kernel(u_ref, h_ref, t_ref, c_ref, o_ref):
    y = _dot(u_ref[0], t_ref[0]) + _dot(h_ref[0].astype(BF16), c_ref[0])
    o_ref[0] = jax.nn.gelu(y, approximate=True).astype(o_ref.dtype)


def _s5_out(u_g, h_in, toe, cmat):
    g, r, k = u_g.shape
    w = h_in.shape[2]
    blk = lambda s: pl.BlockSpec((1,) + s, lambda i: (i, 0, 0))
    return pl.pallas_call(
        _s5_out_kernel,
        grid=(g,),
        in_specs=[blk((r, k)), blk((r, w)), blk((k, k)), blk((w, k))],
        out_specs=blk((r, k)),
        out_shape=jax.ShapeDtypeStruct((g, r, k), BF16),
        compiler_params=_cparams("parallel"),
        name="s5_out",
    )(u_g, h_in, toe, cmat)


def _s5_branch(u, mats, batch, seq):
    toe, bmat, cmat, a_mul, a_swp = mats
    n = u.shape[0]
    q = S5_CHUNK
    r = n // q
    u_g = u.reshape(r, q, S5_GROUPS, S5_GROUP_CH).transpose(2, 0, 1, 3).reshape(S5_GROUPS, r, q * S5_GROUP_CH)
    h_in = _s5_states(u_g, bmat, a_mul, a_swp, batch)
    y_g = _s5_out(u_g, h_in, toe, cmat)
    return y_g.reshape(S5_GROUPS, r, q, S5_GROUP_CH).transpose(1, 2, 0, 3).reshape(n, S5_WIDTH)


def _rotary(t, cos, sin_signed):
    w = t.shape[1]
    half = RET_HEAD_DIM // 2
    lane = lax.broadcasted_iota(jnp.int32, t.shape, 1) % RET_HEAD_DIM
    partner = jnp.where(lane < half, pltpu.roll(t, w - half, 1), pltpu.roll(t, half, 1))
    reps = w // cos.shape[1]
    return t * jnp.tile(cos, (1, reps)) + partner * jnp.tile(sin_signed, (1, reps))


def _ret_chunk_state(kw, v):
    hd = RET_HEAD_DIM
    parts = [_dot_tn(kw[:, h * hd:(h + 1) * hd], v[:, h * hd:(h + 1) * hd]) for h in range(RET_HEADS)]
    return jnp.concatenate(parts, axis=1)


def _ret_state_kernel(k_f, v_f, cos_f, sin_f, k_b, v_b, cos_b, sin_b, kdf_ref, kdb_ref, gl_ref,
                      sf_out, sb_out, sf, sb):
    c = pl.program_id(1)

    @pl.when(c == 0)
    def _():
        sf[...] = jnp.zeros_like(sf)
        sb[...] = jnp.zeros_like(sb)

    sf_out[0, 0] = sf[...]
    sb_out[0, 0] = sb[...]
    scale = RET_HEAD_DIM ** -0.5
    kf = _rotary(k_f[...].astype(F32), cos_f[...], sin_f[...]) * scale
    kb = _rotary(k_b[...].astype(F32), cos_b[...], sin_b[...]) * scale
    sf[...] = sf[...] * gl_ref[...] + _ret_chunk_state((kf * kdf_ref[...]).astype(BF16), v_f[...])
    sb[...] = sb[...] * gl_ref[...] + _ret_chunk_state((kb * kdb_ref[...]).astype(BF16), v_b[...])


def _ret_states(k, v, cos, sin_s, kdf, kdb, gl, batch, seq):
    l = RET_CHUNK
    nc = seq // l
    fwd = lambda w: pl.BlockSpec((l, w), lambda b, c: (b * nc + c, 0))
    bwd = lambda w: pl.BlockSpec((l, w), lambda b, c: (b * nc + nc - 1 - c, 0))
    pf = pl.BlockSpec((l, LANES), lambda b, c: (c, 0))
    pb = pl.BlockSpec((l, LANES), lambda b, c: (nc - 1 - c, 0))
    st = (batch, nc, RET_HEAD_DIM, RET_WIDTH)
    return pl.pallas_call(
        _ret_state_kernel,
        grid=(batch, nc),
        in_specs=[fwd(RET_WIDTH), fwd(RET_WIDTH), pf, pf, bwd(RET_WIDTH), bwd(RET_WIDTH), pb, pb,
                  _resident(kdf.shape), _resident(kdb.shape), _resident(gl.shape)],
        out_specs=[pl.BlockSpec((1, 1, RET_HEAD_DIM, RET_WIDTH), lambda b, c: (b, c, 0, 0)),
                   pl.BlockSpec((1, 1, RET_HEAD_DIM, RET_WIDTH), lambda b, c: (b, nc - 1 - c, 0, 0))],
        out_shape=[jax.ShapeDtypeStruct(st, F32)] * 2,
        scratch_shapes=[pltpu.VMEM((RET_HEAD_DIM, RET_WIDTH), F32)] * 2,
        compiler_params=_cparams("parallel", "arbitrary"),
        name="ret_states",
    )(k, v, cos, sin_s, k, v, cos, sin_s, kdf, kdb, gl)


def _ret_out_kernel(q_ref, k_ref, v_ref, g_ref, cos_ref, sin_ref, sf_ref, sb_ref,
                    dm_ref, qdf_ref, qdb_ref, ng_ref, o_ref):
    hd = RET_HEAD_DIM
    cos = cos_ref[...]
    sin = sin_ref[...]
    qf = _rotary(q_ref[...].astype(F32), cos, sin)
    kf = (_rotary(k_ref[...].astype(F32), cos, sin) * (hd ** -0.5)).astype(BF16)
    q_in = qf.astype(BF16)
    q_lf = (qf * qdf_ref[...]).astype(BF16)
    q_lb = (qf * qdb_ref[...]).astype(BF16)
    v = v_ref[...]
    ys = []
    for h in range(RET_HEADS):
        sl = slice(h * hd, (h + 1) * hd)
        p = (_dot_nt(q_in[:, sl], kf[:, sl]) * dm_ref[h]).astype(BF16)
        y = _dot(p, v[:, sl])
        y = y + _dot(q_lf[:, sl], sf_ref[0, 0, :, sl].astype(BF16))
        y = y + _dot(q_lb[:, sl], sb_ref[0, 0, :, sl].astype(BF16))
        ys.append(y * lax.rsqrt(jnp.mean(y * y, axis=-1, keepdims=True) + EPS))
    y = jnp.concatenate(ys, axis=1) * ng_ref[...]
    o_ref[...] = (_silu(g_ref[...].astype(F32)) * y).astype(o_ref.dtype)


def _ret_out(q, k, v, g, cos, sin_s, sf, sb, dm, qdf, qdb, ng, batch, seq):
    n = q.shape[0]
    l = RET_CHUNK
    nc = seq // l
    row = pl.BlockSpec((l, RET_WIDTH), lambda b, c: (b * nc + c, 0))
    pos = pl.BlockSpec((l, LANES), lambda b, c: (c, 0))
    st = pl.BlockSpec((1, 1, RET_HEAD_DIM, RET_WIDTH), lambda b, c: (b, c, 0, 0))
    return pl.pallas_call(
        _ret_out_kernel,
        grid=(batch, nc),
        in_specs=[row, row, row, row, pos, pos, st, st,
                  _resident(dm.shape), _resident(qdf.shape), _resident(qdb.shape), _resident((1, RET_WIDTH))],
        out_specs=row,
        out_shape=jax.ShapeDtypeStruct((n, RET_WIDTH), BF16),
        compiler_params=_cparams("parallel", "parallel"),
        name="ret_out",
    )(q, k, v, g, cos, sin_s, sf, sb, dm, qdf, qdb, ng)


def _ret_tables(seq):
    l = RET_CHUNK
    hd = RET_HEAD_DIM
    pos = jnp.arange(seq, dtype=F32)
    inv_freq = ROPE_BASE ** (-jnp.arange(0, hd, 2, dtype=F32) / hd)
    ang = pos[:, None] * inv_freq[None, :]
    cos = jnp.tile(jnp.cos(ang), (1, LANES // (hd // 2)))
    sin = jnp.sin(ang)
    sin_s = jnp.tile(jnp.concatenate([-sin, sin], axis=1), (1, LANES // hd))
    log_gamma = jnp.log1p(-jnp.exp2(-5.0 - jnp.arange(RET_HEADS, dtype=F32)))
    idx = jnp.arange(l, dtype=F32)
    dm = jnp.exp(log_gamma[:, None, None] * jnp.abs(idx[:, None] - idx[None, :]))
    per_head = lambda t: jnp.repeat(t, hd, axis=1)
    kdf = per_head(jnp.exp(log_gamma[None, :] * (l - 1.0 - idx)[:, None]))
    kdb = per_head(jnp.exp(log_gamma[None, :] * idx[:, None]))
    qdf = per_head(jnp.exp(log_gamma[None, :] * (idx + 1.0)[:, None]))
    qdb = per_head(jnp.exp(log_gamma[None, :] * (l - idx)[:, None]))
    gl = per_head(jnp.exp(log_gamma * l)[None, :])
    return cos, sin_s, dm, kdf, kdb, qdf, qdb, gl


def _ret_branch(q, k, v, g, norm_g, tables, batch, seq):
    cos, sin_s, dm, kdf, kdb, qdf, qdb, gl = tables
    sf, sb = _ret_states(k, v, cos, sin_s, kdf, kdb, gl, batch, seq)
    return _ret_out(q, k, v, g, cos, sin_s, sf, sb, dm, qdf, qdb, norm_g.reshape(1, RET_WIDTH), batch, seq)


def _merge_kernel(x_ref, ya_ref, yb_ref, yc_ref, g_ref, wgate_ref, bgate_ref, wa_ref, wv_ref, wgg_ref,
                  wb_ref, wc_ref, wo_ref, o_ref):
    x = x_ref[...]
    d = x.shape[1]
    h = _rms(x, g_ref[...]).astype(BF16)
    yb = yb_ref[...]
    glu = (_dot(yb, wv_ref[...]) * jax.nn.sigmoid(_dot(yb, wgg_ref[...]))).astype(BF16)
    branches = (_dot(ya_ref[...], wa_ref[...]), _dot(glu, wb_ref[...]), _dot(yc_ref[...], wc_ref[...]))
    mixed = jnp.zeros(x.shape, F32)
    for i, y in enumerate(branches):
        gate = jax.nn.sigmoid(_dot(h, wgate_ref[:, i * d:(i + 1) * d]) + bgate_ref[:, i * d:(i + 1) * d])
        mixed = mixed + gate * y
    o_ref[...] = x + _dot(mixed.astype(BF16), wo_ref[...])


def _merge(x, ya, yb, yc, g, wgate, bgate, wa, wv, wgg, wb, wc, wo):
    n, d = x.shape
    tm = TOKEN_TILE
    row = lambda c: pl.BlockSpec((tm, c), lambda i: (i, 0))
    consts = (g.reshape(1, d), wgate, bgate.reshape(1, -1), wa, wv, wgg, wb, wc, wo)
    return pl.pallas_call(
        _merge_kernel,
        grid=(n // tm,),
        in_specs=[row(d), row(ya.shape[1]), row(yb.shape[1]), row(yc.shape[1])]
                 + [_resident(c.shape) for c in consts],
        out_specs=row(d),
        out_shape=jax.ShapeDtypeStruct((n, d), F32),
        compiler_params=_cparams("parallel"),
        name="merge",
    )(x, ya, yb, yc, *consts)


def _head_expander():
    return jnp.repeat(jnp.eye(2 * SSD_HEADS, dtype=BF16), SSD_HEAD_DIM, axis=1)


def kernel(x, ffn1_norm, ffn1_w_gate, ffn1_w_up, ffn1_w_down, mix_norm, w_in, b_gate, ssd_conv_w, ssd_conv_b, ssd_dt_bias, ssd_a_log, ssd_d, ssd_norm, w_br_ssd, s5_lam_re, s5_lam_im, s5_log_step, s5_b_re, s5_b_im, s5_c_re, s5_c_im, s5_d, s5_glu_wv, s5_glu_wg, w_br_s5, ret_norm, w_br_ret, w_out, ffn2_norm, ffn2_w_gate, ffn2_w_up, ffn2_w_down, final_norm):
    batch, seq, d = x.shape
    depth = w_in.shape[0]
    n = batch * seq
    assert seq % CONV_TILE == 0 and seq % SSD_CHUNK == 0 and seq % RET_CHUNK == 0 and seq % S5_CHUNK == 0
    assert n % TOKEN_TILE == 0 and n % DT_TILE == 0
    bf = lambda w: w.astype(BF16)
    e = _head_expander()
    tables = _ret_tables(seq)
    offs = [0]
    for s in IN_PROJ_SIZES:
        offs.append(offs[-1] + s)
    xf = x.reshape(n, d).astype(F32)
    for i in range(depth):
        xf = _ffn(xf, ffn1_norm[i], bf(ffn1_w_gate[i]), bf(ffn1_w_up[i]), bf(ffn1_w_down[i]), final_norm, False)
        w = w_in[i]
        seg = [w[:, offs[j]:offs[j + 1]] for j in range(8)]
        seg[2] = jnp.pad(seg[2], ((0, 0), (0, LANES - seg[2].shape[1])))
        z, xbc, dt_raw, u, q, k, v, g = _inproj(
            xf, mix_norm[i], [bf(s) for s in seg], [BF16, BF16, F32, BF16, BF16, BF16, BF16, BF16])
        ya = _ssd_branch(z, xbc, dt_raw, ssd_conv_w[i].astype(F32), ssd_conv_b[i].astype(F32), ssd_dt_bias[i],
                         ssd_a_log[i], ssd_d[i], ssd_norm[i].astype(F32), e, batch, seq)
        mats = _s5_matrices(s5_lam_re[i], s5_lam_im[i], s5_log_step[i], s5_b_re[i], s5_b_im[i],
                            s5_c_re[i], s5_c_im[i], s5_d[i])
        yb = _s5_branch(u, mats, batch, seq)
        yc = _ret_branch(q, k, v, g, ret_norm[i].astype(F32), tables, batch, seq)
        xf = _merge(xf, ya, yb, yc, mix_norm[i], bf(w[:, offs[8]:]), b_gate[i].astype(F32), bf(w_br_ssd[i]),
                    bf(s5_glu_wv[i]), bf(s5_glu_wg[i]), bf(w_br_s5[i]), bf(w_br_ret[i]), bf(w_out[i]))
        xf = _ffn(xf, ffn2_norm[i], bf(ffn2_w_gate[i]), bf(ffn2_w_up[i]), bf(ffn2_w_down[i]), final_norm,
                  i == depth - 1)
    return xf.reshape(batch, seq, d).astype(x.dtype)
```

```python
import functools
import math

import jax
import jax.numpy as jnp
from jax import lax
from jax.experimental import pallas as pl
from jax.experimental.pallas import tpu as pltpu

F32 = jnp.float32
BF16 = jnp.bfloat16

EPS = 1e-6
SSD_HEADS = 8
SSD_HEAD_DIM = 64
SSD_INNER = 512
SSD_GROUPS = 2
SSD_STATE = 64
SSD_CONV = 5
SSD_CONV_CH = 768
S5_GROUP_CH = 16
S5_GROUPS = 24
S5_WIDTH = 384
S5_STATE = 64
S5_MAX_RE = -1e-4
RET_HEADS = 8
RET_HEAD_DIM = 64
RET_WIDTH = 512
ROPE_BASE = 10000.0
N_BRANCH = 3
IN_PROJ_SIZES = (512, 768, 16, 384, 512, 512, 512, 512, 3072)

V7X_VMEM_BYTES = 64 * 1024 * 1024
VMEM_LIMIT = V7X_VMEM_BYTES - 8 * 1024 * 1024
LANES = 128
BF16_SUBLANES = 16

TOKEN_TILE = 512
SSD_CHUNK = 128
RET_CHUNK = 128
S5_CHUNK = 32
S5_ROWS = 64
CONV_TILE = 512
DT_TILE = 2048


def _cparams(*sem):
    return pltpu.CompilerParams(dimension_semantics=sem, vmem_limit_bytes=VMEM_LIMIT)


def _resident(shape):
    nd = len(shape)
    return pl.BlockSpec(shape, lambda *_: (0,) * nd, pipeline_mode=pl.Buffered(1))


def _rms(x, g):
    return (x * lax.rsqrt(jnp.mean(x * x, axis=-1, keepdims=True) + EPS)) * g


def _silu(x):
    return x * jax.nn.sigmoid(x)


def _dot(a, b):
    return jnp.dot(a, b, preferred_element_type=F32)


def _dot_nt(a, b):
    return lax.dot_general(a, b, (((1,), (1,)), ((), ())), preferred_element_type=F32)


def _dot_tn(a, b):
    return lax.dot_general(a, b, (((0,), (0,)), ((), ())), preferred_element_type=F32)


def _split3(v):
    hi = v.astype(BF16)
    r1 = v - hi.astype(F32)
    mid = r1.astype(BF16)
    lo = (r1 - mid.astype(F32)).astype(BF16)
    return hi, mid, lo


def _expand_heads(v, e):
    hi, mid, lo = _split3(v)
    return _dot(hi, e) + _dot(mid, e) + _dot(lo, e)


def _ffn_kernel(x_ref, g_ref, wg_ref, wu_ref, wd_ref, fg_ref, o_ref, *, final):
    x = x_ref[...]
    h = _rms(x, g_ref[...]).astype(BF16)
    a = _dot(h, wg_ref[...])
    b = _dot(h, wu_ref[...])
    t = (_silu(a) * b).astype(BF16)
    y = x + 0.5 * _dot(t, wd_ref[...])
    if final:
        y = _rms(y, fg_ref[...])
    o_ref[...] = y


def _ffn(x, g, wg, wu, wd, fg, final):
    n, d = x.shape
    f = wg.shape[1]
    tm = TOKEN_TILE
    row = pl.BlockSpec((tm, d), lambda i: (i, 0))
    return pl.pallas_call(
        functools.partial(_ffn_kernel, final=final),
        grid=(n // tm,),
        in_specs=[row, _resident((1, d)), _resident((d, f)), _resident((d, f)), _resident((f, d)),
                  _resident((1, d))],
        out_specs=row,
        out_shape=jax.ShapeDtypeStruct((n, d), F32),
        compiler_params=_cparams("parallel"),
        name="ffn",
    )(x, g.reshape(1, d), wg, wu, wd, fg.reshape(1, d))


def _inproj_kernel(x_ref, g_ref, *refs):
    nw = len(refs) // 2
    h = _rms(x_ref[...], g_ref[...]).astype(BF16)
    for w_ref, o_ref in zip(refs[:nw], refs[nw:]):
        o_ref[...] = _dot(h, w_ref[...]).astype(o_ref.dtype)


def _inproj(x, g, weights, out_dtypes):
    n, d = x.shape
    tm = TOKEN_TILE
    row = lambda c: pl.BlockSpec((tm, c), lambda i: (i, 0))
    return pl.pallas_call(
        _inproj_kernel,
        grid=(n // tm,),
        in_specs=[row(d), _resident((1, d))] + [_resident(w.shape) for w in weights],
        out_specs=[row(w.shape[1]) for w in weights],
        out_shape=[jax.ShapeDtypeStruct((n, w.shape[1]), dt) for w, dt in zip(weights, out_dtypes)],
        compiler_params=_cparams("parallel"),
        name="inproj",
    )(x, g.reshape(1, d), *weights)


def _ssd_conv_kernel(prev_ref, x_ref, next_ref, w_ref, b_ref, xs_ref, bm_ref, cm_ref, *, tiles_per_seq):
    i = pl.program_id(0)
    t = x_ref.shape[0]
    first = (i % tiles_per_seq) == 0
    last = (i % tiles_per_seq) == tiles_per_seq - 1
    x = x_ref[...].astype(F32)
    p = jnp.where(first, 0.0, prev_ref[...].astype(F32))
    nx = jnp.where(last, 0.0, next_ref[...].astype(F32))
    ext = jnp.concatenate([p[8:16], x, nx[0:8]], axis=0)
    rows = t + 16
    acc = jnp.broadcast_to(b_ref[...], x.shape)
    half = SSD_CONV // 2
    for k in range(SSD_CONV):
        shifted = pltpu.roll(ext, (half - k) % rows, 0)
        acc = acc + w_ref[k:k + 1, :] * shifted[8:8 + t]
    y = _silu(acc)
    xs_ref[...] = y[:, :SSD_INNER].astype(xs_ref.dtype)
    bm_ref[...] = y[:, SSD_INNER:SSD_INNER + LANES].astype(bm_ref.dtype)
    cm_ref[...] = y[:, SSD_INNER + LANES:].astype(cm_ref.dtype)


def _ssd_conv(xbc, w, b, seq):
    n, c = xbc.shape
    t = CONV_TILE
    hb = BF16_SUBLANES
    nblk = n // hb
    return pl.pallas_call(
        functools.partial(_ssd_conv_kernel, tiles_per_seq=seq // t),
        grid=(n // t,),
        in_specs=[pl.BlockSpec((hb, c), lambda i: (jnp.maximum(i * (t // hb) - 1, 0), 0)),
                  pl.BlockSpec((t, c), lambda i: (i, 0)),
                  pl.BlockSpec((hb, c), lambda i: (jnp.minimum((i + 1) * (t // hb), nblk - 1), 0)),
                  _resident(w.shape), _resident((1, c))],
        out_specs=[pl.BlockSpec((t, SSD_INNER), lambda i: (i, 0)),
                   pl.BlockSpec((t, LANES), lambda i: (i, 0)),
                   pl.BlockSpec((t, LANES), lambda i: (i, 0))],
        out_shape=[jax.ShapeDtypeStruct((n, SSD_INNER), BF16),
                   jax.ShapeDtypeStruct((n, LANES), BF16),
                   jax.ShapeDtypeStruct((n, LANES), BF16)],
        compiler_params=_cparams("parallel"),
        name="ssd_conv",
    )(xbc, xbc, xbc, w, b.reshape(1, c))


def _ssd_dt_kernel(raw_ref, bias_ref, a_ref, dt_ref, cs_ref, *, chunk):
    x = raw_ref[...] + bias_ref[...]
    dt = jnp.maximum(x, 0.0) + jnp.log1p(jnp.exp(-jnp.abs(x)))
    cs = dt * a_ref[...]
    t = x.shape[1]
    pos = lax.broadcasted_iota(jnp.int32, x.shape, 1) % chunk
    fwd = lax.broadcasted_iota(jnp.int32, x.shape, 0) < SSD_HEADS
    k = 1
    while k < chunk:
        before = jnp.where(pos >= k, pltpu.roll(cs, k, 1), 0.0)
        after = jnp.where(pos < chunk - k, pltpu.roll(cs, t - k, 1), 0.0)
        cs = cs + jnp.where(fwd, before, after)
        k *= 2
    dt_ref[...] = dt
    cs_ref[...] = cs


def _ssd_dt(raw_t, bias, a):
    r, n = raw_t.shape
    t = DT_TILE
    blk = pl.BlockSpec((r, t), lambda i: (0, i))
    return pl.pallas_call(
        functools.partial(_ssd_dt_kernel, chunk=SSD_CHUNK),
        grid=(n // t,),
        in_specs=[blk, _resident((r, 1)), _resident((r, 1))],
        out_specs=[blk, blk],
        out_shape=[jax.ShapeDtypeStruct((r, n), F32)] * 2,
        compiler_params=_cparams("parallel"),
        name="ssd_dt",
    )(raw_t, bias.reshape(r, 1), a.reshape(r, 1))


def _ssd_chunk_state(xs, bm, wexp):
    xw = (xs.astype(F32) * wexp).astype(BF16)
    gw = SSD_INNER // SSD_GROUPS
    parts = [_dot_tn(bm[:, g * SSD_STATE:(g + 1) * SSD_STATE], xw[:, g * gw:(g + 1) * gw])
             for g in range(SSD_GROUPS)]
    return jnp.concatenate(parts, axis=1)


def _ssd_state_kernel(xs_f, bm_f, dtc_f, csc_f, xs_b, bm_b, dtc_b, csc_b, e_ref,
                      sf_out, sb_out, sf, sb):
    c = pl.program_id(1)
    h = SSD_HEADS
    l = xs_f.shape[0]

    @pl.when(c == 0)
    def _():
        sf[...] = jnp.zeros_like(sf)
        sb[...] = jnp.zeros_like(sb)

    sf_out[0, 0] = sf[...]
    sb_out[0, 0] = sb[...]

    cs_f = csc_f[...]
    tot_f = cs_f[l - 1:l, :]
    w_f = jnp.exp(tot_f - cs_f) * dtc_f[...]
    cs_b = csc_b[...]
    tot_b = cs_b[0:1, :]
    w_b = jnp.exp(tot_b - cs_b) * dtc_b[...]
    lane = lax.broadcasted_iota(jnp.int32, (1, 2 * h), 1)
    e_f = e_ref[:, :SSD_INNER]
    e_b = e_ref[:, SSD_INNER:]
    wexp_f = _expand_heads(jnp.where(lane < h, w_f, 0.0), e_f)
    wexp_b = _expand_heads(jnp.where(lane >= h, w_b, 0.0), e_b)
    dec_f = _expand_heads(jnp.broadcast_to(jnp.exp(tot_f), (8, 2 * h)), e_f)[0:1]
    dec_b = _expand_heads(jnp.broadcast_to(jnp.exp(tot_b), (8, 2 * h)), e_b)[0:1]
    sf[...] = sf[...] * dec_f + _ssd_chunk_state(xs_f[...], bm_f[...], wexp_f)
    sb[...] = sb[...] * dec_b + _ssd_chunk_state(xs_b[...], bm_b[...], wexp_b)


def _ssd_states(xs, bm, dt_c, cs_c, e, batch, seq):
    l = SSD_CHUNK
    nc = seq // l
    fwd = lambda w: pl.BlockSpec((l, w), lambda b, c: (b * nc + c, 0))
    bwd = lambda w: pl.BlockSpec((l, w), lambda b, c: (b * nc + nc - 1 - c, 0))
    st = (batch, nc, SSD_STATE, SSD_INNER)
    return pl.pallas_call(
        _ssd_state_kernel,
        grid=(batch, nc),
        in_specs=[fwd(SSD_INNER), fwd(LANES), fwd(16), fwd(16),
                  bwd(SSD_INNER), bwd(LANES), bwd(16), bwd(16), _resident(e.shape)],
        out_specs=[pl.BlockSpec((1, 1, SSD_STATE, SSD_INNER), lambda b, c: (b, c, 0, 0)),
                   pl.BlockSpec((1, 1, SSD_STATE, SSD_INNER), lambda b, c: (b, nc - 1 - c, 0, 0))],
        out_shape=[jax.ShapeDtypeStruct(st, F32)] * 2,
        scratch_shapes=[pltpu.VMEM((SSD_STATE, SSD_INNER), F32)] * 2,
        compiler_params=_cparams("parallel", "arbitrary"),
        name="ssd_states",
    )(xs, bm, dt_c, cs_c, xs, bm, dt_c, cs_c, e)


def _ssd_out_kernel(xs_ref, bm_ref, cm_ref, z_ref, dtc_ref, csc_ref, dtr_ref, csr_ref,
                    sf_ref, sb_ref, e_ref, dskip_ref, ng_ref, o_ref):
    h = SSD_HEADS
    l = xs_ref.shape[0]
    hd = SSD_HEAD_DIM
    xs = xs_ref[...]
    bm = bm_ref[...]
    cm = cm_ref[...]
    csc = csc_ref[...]
    csr = csr_ref[...]
    dtr = dtr_ref[...]
    ri = lax.broadcasted_iota(jnp.int32, (l, l), 0)
    ci = lax.broadcasted_iota(jnp.int32, (l, l), 1)
    causal = ri >= ci
    anti = ci >= ri
    neg = -1e30
    ys = []
    for g in range(SSD_GROUPS):
        sl = slice(g * SSD_STATE, (g + 1) * SSD_STATE)
        scores = _dot_nt(cm[:, sl], bm[:, sl])
        for hh in range(g * (h // SSD_GROUPS), (g + 1) * (h // SSD_GROUPS)):
            ef = jnp.exp(jnp.where(causal, csc[:, hh:hh + 1] - csr[hh:hh + 1, :], neg))
            eb = jnp.exp(jnp.where(anti, csc[:, h + hh:h + hh + 1] - csr[h + hh:h + hh + 1, :], neg))
            w = ef * dtr[hh:hh + 1, :] + eb * dtr[h + hh:h + hh + 1, :]
            m = (scores * w).astype(BF16)
            ys.append(_dot(m, xs[:, hh * hd:(hh + 1) * hd]))
    y = jnp.concatenate(ys, axis=1)
    eo = _expand_heads(jnp.exp(csc), e_ref[...])
    gw = SSD_INNER // SSD_GROUPS
    off_f = []
    off_b = []
    for g in range(SSD_GROUPS):
        cg = cm[:, g * SSD_STATE:(g + 1) * SSD_STATE]
        off_f.append(_dot(cg, sf_ref[0, 0, :, g * gw:(g + 1) * gw].astype(BF16)))
        off_b.append(_dot(cg, sb_ref[0, 0, :, g * gw:(g + 1) * gw].astype(BF16)))
    y = y + jnp.concatenate(off_f, axis=1) * eo[:, :SSD_INNER] + jnp.concatenate(off_b, axis=1) * eo[:, SSD_INNER:]
    y = y + dskip_ref[...] * xs.astype(F32)
    y = y * _silu(z_ref[...].astype(F32))
    o_ref[...] = _rms(y, ng_ref[...]).astype(o_ref.dtype)


def _ssd_out(xs, bm, cm, z, dt_c, cs_c, dt_r, cs_r, sf, sb, e, dskip, ng, batch, seq):
    n = xs.shape[0]
    l = SSD_CHUNK
    nc = seq // l
    row = lambda w: pl.BlockSpec((l, w), lambda b, c: (b * nc + c, 0))
    col = pl.BlockSpec((16, l), lambda b, c: (0, b * nc + c))
    st = pl.BlockSpec((1, 1, SSD_STATE, SSD_INNER), lambda b, c: (b, c, 0, 0))
    return pl.pallas_call(
        _ssd_out_kernel,
        grid=(batch, nc),
        in_specs=[row(SSD_INNER), row(LANES), row(LANES), row(SSD_INNER), row(16), row(16), col, col,
                  st, st, _resident(e.shape), _resident((1, SSD_INNER)), _resident((1, SSD_INNER))],
        out_specs=row(SSD_INNER),
        out_shape=jax.ShapeDtypeStruct((n, SSD_INNER), BF16),
        compiler_params=_cparams("parallel", "parallel"),
        name="ssd_out",
    )(xs, bm, cm, z, dt_c, cs_c, dt_r, cs_r, sf, sb, e, dskip, ng)


def _ssd_branch(z, xbc, dt_raw, conv_w, conv_b, dt_bias, a_log, d_skip, norm_g, e, batch, seq):
    xs, bm, cm = _ssd_conv(xbc, conv_w, conv_b, seq)
    a = -jnp.exp(a_log.astype(F32)).reshape(2 * SSD_HEADS)
    dt_r, cs_r = _ssd_dt(dt_raw[:, :2 * SSD_HEADS].T, dt_bias.reshape(2 * SSD_HEADS), a)
    dt_c = dt_r.T
    cs_c = cs_r.T
    sf, sb = _ssd_states(xs, bm, dt_c, cs_c, e, batch, seq)
    dskip = jnp.repeat(d_skip.astype(F32), SSD_HEAD_DIM).reshape(1, SSD_INNER)
    return _ssd_out(xs, bm, cm, z, dt_c, cs_c, dt_r, cs_r, sf, sb, e, dskip,
                    norm_g.reshape(1, SSD_INNER), batch, seq)


def _s5_matrices(lam_re, lam_im, log_step, b_re, b_im, c_re, c_im, d_s5, nsteps):
    q = S5_CHUNK
    hi = lax.Precision.HIGHEST
    lr = jnp.minimum(lam_re.astype(F32), S5_MAX_RE)
    li = lam_im.astype(F32)
    step = jnp.exp(log_step.astype(F32))[..., None]
    tau = jnp.arange(q + 1, dtype=F32)[:, None, None, None]
    mag = jnp.exp(lr * step * tau)
    ang = li * step * tau
    pr = mag * jnp.cos(ang)
    pi = mag * jnp.sin(ang)
    den = lr * lr + li * li
    nr = pr[1] - 1.0
    coef_re = ((nr * lr + pi[1] * li) / den)[..., None]
    coef_im = ((pi[1] * lr - nr * li) / den)[..., None]
    br = b_re.astype(F32)
    bi = b_im.astype(F32)
    bb_re = coef_re * br - coef_im * bi
    bb_im = coef_re * bi + coef_im * br
    cr = c_re.astype(F32)
    ci = c_im.astype(F32)
    cp_re = cr[None] * pr[:, :, :, None, :] - ci[None] * pi[:, :, :, None, :]
    cp_im = cr[None] * pi[:, :, :, None, :] + ci[None] * pr[:, :, :, None, :]
    kern = (jnp.einsum('tdgip,dgpj->dgjti', cp_re[:q], bb_re, precision=hi)
            - jnp.einsum('tdgip,dgpj->dgjti', cp_im[:q], bb_im, precision=hi))
    kf, kb = kern[0], kern[1]
    centre = kf[:, :, 0] + kb[:, :, 0] + jnp.eye(S5_GROUP_CH, dtype=F32)[None] * d_s5.astype(F32)[:, None, :]
    kcat = jnp.concatenate([kb[:, :, :0:-1], centre[:, :, None], kf[:, :, 1:]], axis=2)
    kcat = kcat.reshape(S5_GROUPS, S5_GROUP_CH, (2 * q - 1) * S5_GROUP_CH)
    kcat = jnp.pad(kcat, ((0, 0), (0, 0), (0, S5_GROUP_CH)))

    def to_state(p_r, p_i, d):
        p_r = p_r.transpose(1, 0, 2)[:, :, None, :]
        p_i = p_i.transpose(1, 0, 2)[:, :, None, :]
        b_r = bb_re[d].transpose(0, 2, 1)[:, None]
        b_i = bb_im[d].transpose(0, 2, 1)[:, None]
        return p_r * b_r - p_i * b_i, p_r * b_i + p_i * b_r

    sf_r, sf_i = to_state(pr[:q, 0][::-1], pi[:q, 0][::-1], 0)
    sb_r, sb_i = to_state(pr[:q, 1], pi[:q, 1], 1)
    bmat = jnp.concatenate([sf_r, sb_r, sf_i, sb_i], axis=-1)
    bmat = bmat.reshape(S5_GROUPS, q * S5_GROUP_CH, 4 * S5_STATE)

    def from_state(p_r, p_i, d):
        p_r = p_r.transpose(1, 2, 0)[:, :, :, None]
        p_i = p_i.transpose(1, 2, 0)[:, :, :, None]
        c_r = cr[d].transpose(0, 2, 1)[:, :, None, :]
        c_i = ci[d].transpose(0, 2, 1)[:, :, None, :]
        return c_r * p_r - c_i * p_i, c_r * p_i + c_i * p_r

    of_r, of_i = from_state(pr[1:q + 1, 0], pi[1:q + 1, 0], 0)
    ob_r, ob_i = from_state(pr[1:q + 1, 1][::-1], pi[1:q + 1, 1][::-1], 1)
    cmat = jnp.concatenate([of_r, ob_r, -of_i, -ob_i], axis=1)
    cmat = cmat.reshape(S5_GROUPS, 4 * S5_STATE, q * S5_GROUP_CH)
    m = (q * 2.0 ** jnp.arange(nsteps, dtype=F32))[:, None, None, None]
    amag = jnp.exp(lr * step * m)
    aang = li * step * m
    lanes = lambda t: jnp.concatenate([t[:, 0], t[:, 1]], axis=-1).transpose(1, 0, 2)
    return kcat, bmat.astype(BF16), cmat.astype(BF16), lanes(amag * jnp.cos(aang)), lanes(amag * jnp.sin(aang))


def _s5_gather_kernel(*refs):
    u_refs, o_ref = refs[:-1], refs[-1]
    q = S5_CHUNK
    rb = o_ref.shape[1]
    per = LANES // S5_GROUP_CH
    slot = lax.broadcasted_iota(jnp.int32, (rb, LANES), 1) // S5_GROUP_CH
    for g in range(S5_GROUPS):
        a, gl = divmod(g, per)
        for a2 in range(q // per):
            acc = None
            for sl in range(per):
                src = u_refs[a][pl.ds(a2 * per + sl, rb, stride=q), :]
                piece = src if sl == gl else pltpu.roll(src, ((sl - gl) * S5_GROUP_CH) % LANES, 1)
                acc = piece if acc is None else jnp.where(slot == sl, piece, acc)
            o_ref[g, :, a2 * LANES:(a2 + 1) * LANES] = acc.astype(o_ref.dtype)


def _s5_gather(u_tiles):
    n = u_tiles[0].shape[0]
    q = S5_CHUNK
    rb = S5_ROWS
    w = q * S5_GROUP_CH
    return pl.pallas_call(
        _s5_gather_kernel,
        grid=(n // (rb * q),),
        in_specs=[pl.BlockSpec((rb * q, LANES), lambda i: (i, 0)) for _ in u_tiles],
        out_specs=pl.BlockSpec((S5_GROUPS, rb, w), lambda i: (0, i, 0)),
        out_shape=jax.ShapeDtypeStruct((S5_GROUPS, n // q, w), BF16),
        compiler_params=_cparams("parallel"),
        name="s5_gather",
    )(*u_tiles)


def _s5_core_kernel(u_ref, k_ref, b_ref, c_ref, apr_ref, api_ref, o_ref, toe, *, nc):
    q = S5_CHUNK
    w = q * S5_GROUP_CH
    r = u_ref.shape[1]
    p2 = 2 * S5_STATE
    kc = k_ref[0]
    for s in range(q):
        off = (q - 1 - s) * S5_GROUP_CH
        win = kc if off == 0 else pltpu.roll(kc, 2 * w - off, 1)
        toe[s * S5_GROUP_CH:(s + 1) * S5_GROUP_CH, :] = win[:, :w].astype(BF16)
    u = u_ref[0]
    loc = _dot(u, b_ref[0])
    xr, xi = loc[:, :p2], loc[:, p2:]
    row = lax.broadcasted_iota(jnp.int32, (r, p2), 0) % nc
    fwd = lax.broadcasted_iota(jnp.int32, (r, p2), 1) < S5_STATE

    def shift(v, k):
        dn = jnp.where(row >= k, pltpu.roll(v, k, 0), 0.0)
        up = jnp.where(row < nc - k, pltpu.roll(v, r - k, 0), 0.0)
        return jnp.where(fwd, dn, up)

    k, i = 1, 0
    while k < nc:
        sr, si = shift(xr, k), shift(xi, k)
        ar, ai = apr_ref[0, i:i + 1, :], api_ref[0, i:i + 1, :]
        xr, xi = xr + (ar * sr - ai * si), xi + (ar * si + ai * sr)
        k, i = 2 * k, i + 1
    h = jnp.concatenate([shift(xr, 1), shift(xi, 1)], axis=1).astype(BF16)
    y = _dot(u, toe[...]) + _dot(h, c_ref[0])
    o_ref[0] = jax.nn.gelu(y, approximate=True).astype(o_ref.dtype)


def _s5_core(u_g, kcat, bmat, cmat, apr, api, batch):
    g, r, w = u_g.shape
    blk = lambda a: pl.BlockSpec((1,) + a.shape[1:], lambda i: (i, 0, 0))
    return pl.pallas_call(
        functools.partial(_s5_core_kernel, nc=r // batch),
        grid=(g,),
        in_specs=[blk(u_g), blk(kcat), blk(bmat), blk(cmat), blk(apr), blk(api)],
        out_specs=blk(u_g),
        out_shape=jax.ShapeDtypeStruct((g, r, w), BF16),
        scratch_shapes=[pltpu.VMEM((w, w), BF16)],
        compiler_params=_cparams("parallel"),
        name="s5_core",
    )(u_g, kcat, bmat, cmat, apr, api)


def _s5_scatter_kernel(y_ref, wv_ref, wg_ref, o_ref, *nat):
    q = S5_CHUNK
    rb = y_ref.shape[1]
    per = LANES // S5_GROUP_CH
    slot = lax.broadcasted_iota(jnp.int32, (rb, LANES), 1) // S5_GROUP_CH
    for s in range(q):
        a2, sl = divmod(s, per)
        for a in range(S5_GROUPS // per):
            acc = None
            for gl in range(per):
                src = y_ref[a * per + gl, :, a2 * LANES:(a2 + 1) * LANES].astype(F32)
                piece = src if gl == sl else pltpu.roll(src, ((gl - sl) * S5_GROUP_CH) % LANES, 1)
                acc = piece if acc is None else jnp.where(slot == gl, piece, acc)
            nat[a][pl.ds(s, rb, stride=q), :] = acc
    y = jnp.concatenate([t[...] for t in nat], axis=1).astype(BF16)
    o_ref[...] = (_dot(y, wv_ref[...]) * jax.nn.sigmoid(_dot(y, wg_ref[...]))).astype(o_ref.dtype)


def _s5_scatter(y_g, wv, wg):
    g, r, w = y_g.shape
    q = S5_CHUNK
    rb = S5_ROWS
    return pl.pallas_call(
        _s5_scatter_kernel,
        grid=(r // rb,),
        in_specs=[pl.BlockSpec((g, rb, w), lambda i: (0, i, 0)), _resident(wv.shape), _resident(wg.shape)],
        out_specs=pl.BlockSpec((rb * q, S5_WIDTH), lambda i: (i, 0)),
        out_shape=jax.ShapeDtypeStruct((r * q, S5_WIDTH), BF16),
        scratch_shapes=[pltpu.VMEM((rb * q, LANES), F32)] * (S5_WIDTH // LANES),
        compiler_params=_cparams("parallel"),
        name="s5_scatter",
    )(y_g, wv, wg)


def _s5_branch(u_tiles, mats, wv, wg, batch):
    kcat, bmat, cmat, apr, api = mats
    y_g = _s5_core(_s5_gather(u_tiles), kcat, bmat, cmat, apr, api, batch)
    return _s5_scatter(y_g, wv, wg)


def _rotary(t, cos, sin_signed):
    w = t.shape[1]
    half = RET_HEAD_DIM // 2
    lane = lax.broadcasted_iota(jnp.int32, t.shape, 1) % RET_HEAD_DIM
    partner = jnp.where(lane < half, pltpu.roll(t, w - half, 1), pltpu.roll(t, half, 1))
    reps = w // cos.shape[1]
    return t * jnp.tile(cos, (1, reps)) + partner * jnp.tile(sin_signed, (1, reps))


def _ret_chunk_state(kw, v):
    hd = RET_HEAD_DIM
    parts = [_dot_tn(kw[:, h * hd:(h + 1) * hd], v[:, h * hd:(h + 1) * hd]) for h in range(RET_HEADS)]
    return jnp.concatenate(parts, axis=1)


def _ret_state_kernel(k_f, v_f, cos_f, sin_f, k_b, v_b, cos_b, sin_b, kdf_ref, kdb_ref, gl_ref,
                      sf_out, sb_out, sf, sb):
    c = pl.program_id(1)

    @pl.when(c == 0)
    def _():
        sf[...] = jnp.zeros_like(sf)
        sb[...] = jnp.zeros_like(sb)

    sf_out[0, 0] = sf[...]
    sb_out[0, 0] = sb[...]
    scale = RET_HEAD_DIM ** -0.5
    kf = _rotary(k_f[...].astype(F32), cos_f[...], sin_f[...]) * scale
    kb = _rotary(k_b[...].astype(F32), cos_b[...], sin_b[...]) * scale
    sf[...] = sf[...] * gl_ref[...] + _ret_chunk_state((kf * kdf_ref[...]).astype(BF16), v_f[...])
    sb[...] = sb[...] * gl_ref[...] + _ret_chunk_state((kb * kdb_ref[...]).astype(BF16), v_b[...])


def _ret_states(k, v, cos, sin_s, kdf, kdb, gl, batch, seq):
    l = RET_CHUNK
    nc = seq // l
    fwd = lambda w: pl.BlockSpec((l, w), lambda b, c: (b * nc + c, 0))
    bwd = lambda w: pl.BlockSpec((l, w), lambda b, c: (b * nc + nc - 1 - c, 0))
    pf = pl.BlockSpec((l, LANES), lambda b, c: (c, 0))
    pb = pl.BlockSpec((l, LANES), lambda b, c: (nc - 1 - c, 0))
    st = (batch, nc, RET_HEAD_DIM, RET_WIDTH)
    return pl.pallas_call(
        _ret_state_kernel,
        grid=(batch, nc),
        in_specs=[fwd(RET_WIDTH), fwd(RET_WIDTH), pf, pf, bwd(RET_WIDTH), bwd(RET_WIDTH), pb, pb,
                  _resident(kdf.shape), _resident(kdb.shape), _resident(gl.shape)],
        out_specs=[pl.BlockSpec((1, 1, RET_HEAD_DIM, RET_WIDTH), lambda b, c: (b, c, 0, 0)),
                   pl.BlockSpec((1, 1, RET_HEAD_DIM, RET_WIDTH), lambda b, c: (b, nc - 1 - c, 0, 0))],
        out_shape=[jax.ShapeDtypeStruct(st, F32)] * 2,
        scratch_shapes=[pltpu.VMEM((RET_HEAD_DIM, RET_WIDTH), F32)] * 2,
        compiler_params=_cparams("parallel", "arbitrary"),
        name="ret_states",
    )(k, v, cos, sin_s, k, v, cos, sin_s, kdf, kdb, gl)


def _ret_out_kernel(q_ref, k_ref, v_ref, g_ref, cos_ref, sin_ref, sf_ref, sb_ref,
                    dm_ref, qdf_ref, qdb_ref, ng_ref, o_ref):
    hd = RET_HEAD_DIM
    cos = cos_ref[...]
    sin = sin_ref[...]
    qf = _rotary(q_ref[...].astype(F32), cos, sin)
    kf = (_rotary(k_ref[...].astype(F32), cos, sin) * (hd ** -0.5)).astype(BF16)
    q_in = qf.astype(BF16)
    q_lf = (qf * qdf_ref[...]).astype(BF16)
    q_lb = (qf * qdb_ref[...]).astype(BF16)
    v = v_ref[...]
    ys = []
    for h in range(RET_HEADS):
        sl = slice(h * hd, (h + 1) * hd)
        p = (_dot_nt(q_in[:, sl], kf[:, sl]) * dm_ref[h]).astype(BF16)
        y = _dot(p, v[:, sl])
        y = y + _dot(q_lf[:, sl], sf_ref[0, 0, :, sl].astype(BF16))
        y = y + _dot(q_lb[:, sl], sb_ref[0, 0, :, sl].astype(BF16))
        ys.append(y * lax.rsqrt(jnp.mean(y * y, axis=-1, keepdims=True) + EPS))
    y = jnp.concatenate(ys, axis=1) * ng_ref[...]
    o_ref[...] = (_silu(g_ref[...].astype(F32)) * y).astype(o_ref.dtype)


def _ret_out(q, k, v, g, cos, sin_s, sf, sb, dm, qdf, qdb, ng, batch, seq):
    n = q.shape[0]
    l = RET_CHUNK
    nc = seq // l
    row = pl.BlockSpec((l, RET_WIDTH), lambda b, c: (b * nc + c, 0))
    pos = pl.BlockSpec((l, LANES), lambda b, c: (c, 0))
    st = pl.BlockSpec((1, 1, RET_HEAD_DIM, RET_WIDTH), lambda b, c: (b, c, 0, 0))
    return pl.pallas_call(
        _ret_out_kernel,
        grid=(batch, nc),
        in_specs=[row, row, row, row, pos, pos, st, st,
                  _resident(dm.shape), _resident(qdf.shape), _resident(qdb.shape), _resident((1, RET_WIDTH))],
        out_specs=row,
        out_shape=jax.ShapeDtypeStruct((n, RET_WIDTH), BF16),
        compiler_params=_cparams("parallel", "parallel"),
        name="ret_out",
    )(q, k, v, g, cos, sin_s, sf, sb, dm, qdf, qdb, ng)


def _ret_tables(seq):
    l = RET_CHUNK
    hd = RET_HEAD_DIM
    pos = jnp.arange(seq, dtype=F32)
    inv_freq = ROPE_BASE ** (-jnp.arange(0, hd, 2, dtype=F32) / hd)
    ang = pos[:, None] * inv_freq[None, :]
    cos = jnp.tile(jnp.cos(ang), (1, LANES // (hd // 2)))
    sin = jnp.sin(ang)
    sin_s = jnp.tile(jnp.concatenate([-sin, sin], axis=1), (1, LANES // hd))
    log_gamma = jnp.log1p(-jnp.exp2(-5.0 - jnp.arange(RET_HEADS, dtype=F32)))
    idx = jnp.arange(l, dtype=F32)
    dm = jnp.exp(log_gamma[:, None, None] * jnp.abs(idx[:, None] - idx[None, :]))
    per_head = lambda t: jnp.repeat(t, hd, axis=1)
    kdf = per_head(jnp.exp(log_gamma[None, :] * (l - 1.0 - idx)[:, None]))
    kdb = per_head(jnp.exp(log_gamma[None, :] * idx[:, None]))
    qdf = per_head(jnp.exp(log_gamma[None, :] * (idx + 1.0)[:, None]))
    qdb = per_head(jnp.exp(log_gamma[None, :] * (l - idx)[:, None]))
    gl = per_head(jnp.exp(log_gamma * l)[None, :])
    return cos, sin_s, dm, kdf, kdb, qdf, qdb, gl


def _ret_branch(q, k, v, g, norm_g, tables, batch, seq):
    cos, sin_s, dm, kdf, kdb, qdf, qdb, gl = tables
    sf, sb = _ret_states(k, v, cos, sin_s, kdf, kdb, gl, batch, seq)
    return _ret_out(q, k, v, g, cos, sin_s, sf, sb, dm, qdf, qdb, norm_g.reshape(1, RET_WIDTH), batch, seq)


def _merge_kernel(x_ref, ya_ref, yb_ref, yc_ref, g_ref, wgate_ref, bgate_ref, wa_ref, wb_ref, wc_ref, wo_ref,
                  o_ref):
    x = x_ref[...]
    d = x.shape[1]
    h = _rms(x, g_ref[...]).astype(BF16)
    branches = (_dot(ya_ref[...], wa_ref[...]), _dot(yb_ref[...], wb_ref[...]), _dot(yc_ref[...], wc_ref[...]))
    mixed = jnp.zeros(x.shape, F32)
    for i, y in enumerate(branches):
        gate = jax.nn.sigmoid(_dot(h, wgate_ref[:, i * d:(i + 1) * d]) + bgate_ref[:, i * d:(i + 1) * d])
        mixed = mixed + gate * y
    o_ref[...] = x + _dot(mixed.astype(BF16), wo_ref[...])


def _merge(x, ya, yb, yc, g, wgate, bgate, wa, wb, wc, wo):
    n, d = x.shape
    tm = TOKEN_TILE
    row = lambda c: pl.BlockSpec((tm, c), lambda i: (i, 0))
    consts = (g.reshape(1, d), wgate, bgate.reshape(1, -1), wa, wb, wc, wo)
    return pl.pallas_call(
        _merge_kernel,
        grid=(n // tm,),
        in_specs=[row(d), row(ya.shape[1]), row(yb.shape[1]), row(yc.shape[1])]
                 + [_resident(c.shape) for c in consts],
        out_specs=row(d),
        out_shape=jax.ShapeDtypeStruct((n, d), F32),
        compiler_params=_cparams("parallel"),
        name="merge",
    )(x, ya, yb, yc, *consts)


def _head_expander():
    return jnp.repeat(jnp.eye(2 * SSD_HEADS, dtype=BF16), SSD_HEAD_DIM, axis=1)


def kernel(x, ffn1_norm, ffn1_w_gate, ffn1_w_up, ffn1_w_down, mix_norm, w_in, b_gate, ssd_conv_w, ssd_conv_b, ssd_dt_bias, ssd_a_log, ssd_d, ssd_norm, w_br_ssd, s5_lam_re, s5_lam_im, s5_log_step, s5_b_re, s5_b_im, s5_c_re, s5_c_im, s5_d, s5_glu_wv, s5_glu_wg, w_br_s5, ret_norm, w_br_ret, w_out, ffn2_norm, ffn2_w_gate, ffn2_w_up, ffn2_w_down, final_norm):
    batch, seq, d = x.shape
    depth = w_in.shape[0]
    n = batch * seq
    assert seq % CONV_TILE == 0 and seq % SSD_CHUNK == 0 and seq % RET_CHUNK == 0 and seq % S5_CHUNK == 0
    assert n % TOKEN_TILE == 0 and n % DT_TILE == 0 and n % (S5_ROWS * S5_CHUNK) == 0
    s5_steps = (seq // S5_CHUNK - 1).bit_length()
    assert 2 ** s5_steps == seq // S5_CHUNK
    bf = lambda w: w.astype(BF16)
    e = _head_expander()
    tables = _ret_tables(seq)
    offs = [0]
    for s in IN_PROJ_SIZES:
        offs.append(offs[-1] + s)
    xf = x.reshape(n, d).astype(F32)
    for i in range(depth):
        xf = _ffn(xf, ffn1_norm[i], bf(ffn1_w_gate[i]), bf(ffn1_w_up[i]), bf(ffn1_w_down[i]), final_norm, False)
        w = w_in[i]
        seg = [w[:, offs[j]:offs[j + 1]] for j in range(8)]
        seg[2] = jnp.pad(seg[2], ((0, 0), (0, LANES - seg[2].shape[1])))
        seg[3:4] = [seg[3][:, c:c + LANES] for c in range(0, S5_WIDTH, LANES)]
        z, xbc, dt_raw, u0, u1, u2, q, k, v, g = _inproj(
            xf, mix_norm[i], [bf(s) for s in seg], [BF16, BF16, F32, F32, F32, F32, BF16, BF16, BF16, BF16])
        ya = _ssd_branch(z, xbc, dt_raw, ssd_conv_w[i].astype(F32), ssd_conv_b[i].astype(F32), ssd_dt_bias[i],
                         ssd_a_log[i], ssd_d[i], ssd_norm[i].astype(F32), e, batch, seq)
        mats = _s5_matrices(s5_lam_re[i], s5_lam_im[i], s5_log_step[i], s5_b_re[i], s5_b_im[i],
                            s5_c_re[i], s5_c_im[i], s5_d[i], s5_steps)
        yb = _s5_branch((u0, u1, u2), mats, bf(s5_glu_wv[i]), bf(s5_glu_wg[i]), batch)
        yc = _ret_branch(q, k, v, g, ret_norm[i].astype(F32), tables, batch, seq)
        xf = _merge(xf, ya, yb, yc, mix_norm[i], bf(w[:, offs[8]:]), b_gate[i].astype(F32), bf(w_br_ssd[i]),
                    bf(w_br_s5[i]), bf(w_br_ret[i]), bf(w_out[i]))
        xf = _ffn(xf, ffn2_norm[i], bf(ffn2_w_gate[i]), bf(ffn2_w_up[i]), bf(ffn2_w_down[i]), final_norm,
                  i == depth - 1)
    return xf.reshape(batch, seq, d).astype(x.dtype)
```

```python
import functools
import math

import jax
import jax.numpy as jnp
from jax import lax
from jax.experimental import pallas as pl
from jax.experimental.pallas import tpu as pltpu

F32 = jnp.float32
BF16 = jnp.bfloat16

EPS = 1e-6
SSD_HEADS = 8
SSD_HEAD_DIM = 64
SSD_INNER = 512
SSD_GROUPS = 2
SSD_STATE = 64
SSD_CONV = 5
SSD_CONV_CH = 768
S5_GROUP_CH = 16
S5_GROUPS = 24
S5_WIDTH = 384
S5_STATE = 64
S5_MAX_RE = -1e-4
RET_HEADS = 8
RET_HEAD_DIM = 64
RET_WIDTH = 512
ROPE_BASE = 10000.0
N_BRANCH = 3
IN_PROJ_SIZES = (512, 768, 16, 384, 512, 512, 512, 512, 3072)

V7X_VMEM_BYTES = 64 * 1024 * 1024
VMEM_LIMIT = V7X_VMEM_BYTES - 8 * 1024 * 1024
LANES = 128
BF16_SUBLANES = 16

TOKEN_TILE = 512
SSD_CHUNK = 128
RET_CHUNK = 128
STEP_CHUNKS = 4
S5_CHUNK = 32
S5_ROWS = 64
CONV_TILE = 512
DT_TILE = 2048


def _cparams(*sem):
    return pltpu.CompilerParams(dimension_semantics=sem, vmem_limit_bytes=VMEM_LIMIT)


def _resident(shape):
    nd = len(shape)
    return pl.BlockSpec(shape, lambda *_: (0,) * nd, pipeline_mode=pl.Buffered(1))


def _rms(x, g):
    return (x * lax.rsqrt(jnp.mean(x * x, axis=-1, keepdims=True) + EPS)) * g


def _silu(x):
    return x * jax.nn.sigmoid(x)


def _dot(a, b):
    return jnp.dot(a, b, preferred_element_type=F32)


def _dot_nt(a, b):
    return lax.dot_general(a, b, (((1,), (1,)), ((), ())), preferred_element_type=F32)


def _dot_tn(a, b):
    return lax.dot_general(a, b, (((0,), (0,)), ((), ())), preferred_element_type=F32)


def _split3(v):
    hi = v.astype(BF16)
    r1 = v - hi.astype(F32)
    mid = r1.astype(BF16)
    lo = (r1 - mid.astype(F32)).astype(BF16)
    return hi, mid, lo


def _expand_heads(v, e3):
    return _dot(jnp.concatenate(_split3(v), axis=1), e3)


def _ffn_kernel(x_ref, g_ref, wg_ref, wu_ref, wd_ref, fg_ref, o_ref, *, final):
    x = x_ref[...]
    h = _rms(x, g_ref[...]).astype(BF16)
    a = _dot(h, wg_ref[...])
    b = _dot(h, wu_ref[...])
    t = (_silu(a) * b).astype(BF16)
    y = x + 0.5 * _dot(t, wd_ref[...])
    if final:
        y = _rms(y, fg_ref[...])
    o_ref[...] = y


def _ffn(x, g, wg, wu, wd, fg, final):
    n, d = x.shape
    f = wg.shape[1]
    tm = TOKEN_TILE
    row = pl.BlockSpec((tm, d), lambda i: (i, 0))
    return pl.pallas_call(
        functools.partial(_ffn_kernel, final=final),
        grid=(n // tm,),
        in_specs=[row, _resident((1, d)), _resident((d, f)), _resident((d, f)), _resident((f, d)),
                  _resident((1, d))],
        out_specs=row,
        out_shape=jax.ShapeDtypeStruct((n, d), F32),
        compiler_params=_cparams("parallel"),
        name="ffn",
    )(x, g.reshape(1, d), wg, wu, wd, fg.reshape(1, d))


def _inproj_kernel(x_ref, g_ref, cos_ref, sin_ref, *refs, rope_scale):
    nw = len(refs) // 2
    h = _rms(x_ref[...], g_ref[...]).astype(BF16)
    for w_ref, o_ref, scale in zip(refs[:nw], refs[nw:], rope_scale):
        y = _dot(h, w_ref[...])
        if scale is not None:
            y = _rotary(y, cos_ref[...], sin_ref[...])
            if scale != 1.0:
                y = y * scale
        o_ref[...] = y.astype(o_ref.dtype)


def _inproj(x, g, cos, sin_s, weights, out_dtypes, rope_scale, seq):
    n, d = x.shape
    tm = TOKEN_TILE
    row = lambda c: pl.BlockSpec((tm, c), lambda i: (i, 0))
    pos = pl.BlockSpec((tm, LANES), lambda i: (i % (seq // tm), 0))
    return pl.pallas_call(
        functools.partial(_inproj_kernel, rope_scale=rope_scale),
        grid=(n // tm,),
        in_specs=[row(d), _resident((1, d)), pos, pos] + [_resident(w.shape) for w in weights],
        out_specs=[row(w.shape[1]) for w in weights],
        out_shape=[jax.ShapeDtypeStruct((n, w.shape[1]), dt) for w, dt in zip(weights, out_dtypes)],
        compiler_params=_cparams("parallel"),
        name="inproj",
    )(x, g.reshape(1, d), cos, sin_s, *weights)


def _ssd_conv_kernel(prev_ref, x_ref, next_ref, w_ref, b_ref, xs_ref, bm_ref, cm_ref, *, tiles_per_seq):
    i = pl.program_id(0)
    t = x_ref.shape[0]
    first = (i % tiles_per_seq) == 0
    last = (i % tiles_per_seq) == tiles_per_seq - 1
    x = x_ref[...].astype(F32)
    p = jnp.where(first, 0.0, prev_ref[...].astype(F32))
    nx = jnp.where(last, 0.0, next_ref[...].astype(F32))
    ext = jnp.concatenate([p[8:16], x, nx[0:8]], axis=0)
    rows = t + 16
    acc = jnp.broadcast_to(b_ref[...], x.shape)
    half = SSD_CONV // 2
    for k in range(SSD_CONV):
        shifted = pltpu.roll(ext, (half - k) % rows, 0)
        acc = acc + w_ref[k:k + 1, :] * shifted[8:8 + t]
    y = _silu(acc)
    xs_ref[...] = y[:, :SSD_INNER].astype(xs_ref.dtype)
    bm_ref[...] = y[:, SSD_INNER:SSD_INNER + LANES].astype(bm_ref.dtype)
    cm_ref[...] = y[:, SSD_INNER + LANES:].astype(cm_ref.dtype)


def _ssd_conv(xbc, w, b, seq):
    n, c = xbc.shape
    t = CONV_TILE
    hb = BF16_SUBLANES
    nblk = n // hb
    return pl.pallas_call(
        functools.partial(_ssd_conv_kernel, tiles_per_seq=seq // t),
        grid=(n // t,),
        in_specs=[pl.BlockSpec((hb, c), lambda i: (jnp.maximum(i * (t // hb) - 1, 0), 0)),
                  pl.BlockSpec((t, c), lambda i: (i, 0)),
                  pl.BlockSpec((hb, c), lambda i: (jnp.minimum((i + 1) * (t // hb), nblk - 1), 0)),
                  _resident(w.shape), _resident((1, c))],
        out_specs=[pl.BlockSpec((t, SSD_INNER), lambda i: (i, 0)),
                   pl.BlockSpec((t, LANES), lambda i: (i, 0)),
                   pl.BlockSpec((t, LANES), lambda i: (i, 0))],
        out_shape=[jax.ShapeDtypeStruct((n, SSD_INNER), BF16),
                   jax.ShapeDtypeStruct((n, LANES), BF16),
                   jax.ShapeDtypeStruct((n, LANES), BF16)],
        compiler_params=_cparams("parallel"),
        name="ssd_conv",
    )(xbc, xbc, xbc, w, b.reshape(1, c))


def _ssd_dt_kernel(raw_ref, bias_ref, a_ref, dt_ref, cs_ref, *, chunk):
    x = raw_ref[...] + bias_ref[...]
    dt = jnp.maximum(x, 0.0) + jnp.log1p(jnp.exp(-jnp.abs(x)))
    cs = dt * a_ref[...]
    t = x.shape[1]
    pos = lax.broadcasted_iota(jnp.int32, x.shape, 1) % chunk
    fwd = lax.broadcasted_iota(jnp.int32, x.shape, 0) < SSD_HEADS
    k = 1
    while k < chunk:
        before = jnp.where(pos >= k, pltpu.roll(cs, k, 1), 0.0)
        after = jnp.where(pos < chunk - k, pltpu.roll(cs, t - k, 1), 0.0)
        cs = cs + jnp.where(fwd, before, after)
        k *= 2
    dt_ref[...] = dt
    cs_ref[...] = cs


def _ssd_dt(raw_t, bias, a):
    r, n = raw_t.shape
    t = DT_TILE
    blk = pl.BlockSpec((r, t), lambda i: (0, i))
    return pl.pallas_call(
        functools.partial(_ssd_dt_kernel, chunk=SSD_CHUNK),
        grid=(n // t,),
        in_specs=[blk, _resident((r, 1)), _resident((r, 1))],
        out_specs=[blk, blk],
        out_shape=[jax.ShapeDtypeStruct((r, n), F32)] * 2,
        compiler_params=_cparams("parallel"),
        name="ssd_dt",
    )(raw_t, bias.reshape(r, 1), a.reshape(r, 1))


def _ssd_chunk_state(xs, bm, wexp):
    xw = (xs.astype(F32) * wexp).astype(BF16)
    gw = SSD_INNER // SSD_GROUPS
    parts = [_dot_tn(bm[:, g * SSD_STATE:(g + 1) * SSD_STATE], xw[:, g * gw:(g + 1) * gw])
             for g in range(SSD_GROUPS)]
    return jnp.concatenate(parts, axis=1)


def _ssd_state_kernel(xs_f, bm_f, dtc_f, csc_f, xs_b, bm_b, dtc_b, csc_b, e_ref,
                      sf_out, sb_out, sf, sb):
    c = pl.program_id(1)
    h = SSD_HEADS
    l = SSD_CHUNK
    nt = xs_f.shape[0] // l

    @pl.when(c == 0)
    def _():
        sf[...] = jnp.zeros_like(sf)
        sb[...] = jnp.zeros_like(sb)

    lane = lax.broadcasted_iota(jnp.int32, (1, 2 * h), 1)
    e_f = e_ref[:, :SSD_INNER]
    e_b = e_ref[:, SSD_INNER:]
    for t in range(nt):
        rf = slice(t * l, (t + 1) * l)
        rb = slice((nt - 1 - t) * l, (nt - t) * l)
        sf_out[0, t] = sf[...].astype(sf_out.dtype)
        sb_out[0, nt - 1 - t] = sb[...].astype(sb_out.dtype)
        cs_f = csc_f[rf, :]
        tot_f = cs_f[l - 1:l, :]
        w_f = jnp.exp(tot_f - cs_f) * dtc_f[rf, :]
        cs_b = csc_b[rb, :]
        tot_b = cs_b[0:1, :]
        w_b = jnp.exp(tot_b - cs_b) * dtc_b[rb, :]
        wexp_f = _expand_heads(jnp.where(lane < h, w_f, 0.0), e_f)
        wexp_b = _expand_heads(jnp.where(lane >= h, w_b, 0.0), e_b)
        dec_f = _expand_heads(jnp.broadcast_to(jnp.exp(tot_f), (8, 2 * h)), e_f)[0:1]
        dec_b = _expand_heads(jnp.broadcast_to(jnp.exp(tot_b), (8, 2 * h)), e_b)[0:1]
        sf[...] = sf[...] * dec_f + _ssd_chunk_state(xs_f[rf, :], bm_f[rf, :], wexp_f)
        sb[...] = sb[...] * dec_b + _ssd_chunk_state(xs_b[rb, :], bm_b[rb, :], wexp_b)


def _ssd_states(xs, bm, dt_c, cs_c, e, batch, seq):
    l = SSD_CHUNK
    nt = STEP_CHUNKS
    nb = seq // (l * nt)
    fwd = lambda w: pl.BlockSpec((nt * l, w), lambda b, c: (b * nb + c, 0))
    bwd = lambda w: pl.BlockSpec((nt * l, w), lambda b, c: (b * nb + nb - 1 - c, 0))
    st = (batch, nb * nt, SSD_STATE, SSD_INNER)
    return pl.pallas_call(
        _ssd_state_kernel,
        grid=(batch, nb),
        in_specs=[fwd(SSD_INNER), fwd(LANES), fwd(16), fwd(16),
                  bwd(SSD_INNER), bwd(LANES), bwd(16), bwd(16), _resident(e.shape)],
        out_specs=[pl.BlockSpec((1, nt, SSD_STATE, SSD_INNER), lambda b, c: (b, c, 0, 0)),
                   pl.BlockSpec((1, nt, SSD_STATE, SSD_INNER), lambda b, c: (b, nb - 1 - c, 0, 0))],
        out_shape=[jax.ShapeDtypeStruct(st, BF16)] * 2,
        scratch_shapes=[pltpu.VMEM((SSD_STATE, SSD_INNER), F32)] * 2,
        compiler_params=_cparams("parallel", "arbitrary"),
        name="ssd_states",
    )(xs, bm, dt_c, cs_c, xs, bm, dt_c, cs_c, e)


def _ssd_out_kernel(xs_ref, bm_ref, cm_ref, z_ref, dtc_ref, csc_ref, dtr_ref, csr_ref,
                    sf_ref, sb_ref, e_ref, dskip_ref, ng_ref, o_ref):
    h = SSD_HEADS
    l = SSD_CHUNK
    gw = SSD_INNER // SSD_GROUPS
    ri = lax.broadcasted_iota(jnp.int32, (l, l), 0)
    ci = lax.broadcasted_iota(jnp.int32, (l, l), 1)
    causal = ri >= ci
    anti = ci >= ri
    low = lax.broadcasted_iota(jnp.int32, (l, LANES), 1) < LANES // 2
    neg = -1e30

    def block_diag(t):
        zero = jnp.zeros_like(t)
        return jnp.concatenate([jnp.where(low, t, zero), jnp.where(low, zero, t)], axis=0)

    for t in range(xs_ref.shape[0] // l):
        rows = slice(t * l, (t + 1) * l)
        xs = xs_ref[rows, :]
        cm = cm_ref[rows, :]
        csc = csc_ref[rows, :]
        csr = csr_ref[:, rows]
        dtr = dtr_ref[:, rows]
        scores = _dot_nt(cm, block_diag(bm_ref[rows, :]))
        ys = []
        for pair in range(h // 2):
            g = pair // (h // 2 // SSD_GROUPS)
            sc = scores[:, g * l:(g + 1) * l]
            ms = []
            for hh in (2 * pair, 2 * pair + 1):
                ef = jnp.exp(jnp.where(causal, csc[:, hh:hh + 1] - csr[hh:hh + 1, :], neg))
                eb = jnp.exp(jnp.where(anti, csc[:, h + hh:h + hh + 1] - csr[h + hh:h + hh + 1, :], neg))
                w = ef * dtr[hh:hh + 1, :] + eb * dtr[h + hh:h + hh + 1, :]
                ms.append((sc * w).astype(BF16))
            ys.append(_dot(jnp.concatenate(ms, axis=1), block_diag(xs[:, pair * LANES:(pair + 1) * LANES])))
        y = jnp.concatenate(ys, axis=1)
        eo = _expand_heads(jnp.exp(csc), e_ref[...])
        off_f = []
        off_b = []
        for g in range(SSD_GROUPS):
            cg = cm[:, g * SSD_STATE:(g + 1) * SSD_STATE]
            off_f.append(_dot(cg, sf_ref[0, t, :, g * gw:(g + 1) * gw]))
            off_b.append(_dot(cg, sb_ref[0, t, :, g * gw:(g + 1) * gw]))
        y = (y + jnp.concatenate(off_f, axis=1) * eo[:, :SSD_INNER]
             + jnp.concatenate(off_b, axis=1) * eo[:, SSD_INNER:])
        y = y + dskip_ref[...] * xs.astype(F32)
        y = y * _silu(z_ref[rows, :].astype(F32))
        o_ref[rows, :] = _rms(y, ng_ref[...]).astype(o_ref.dtype)


def _ssd_out(xs, bm, cm, z, dt_c, cs_c, dt_r, cs_r, sf, sb, e, dskip, ng, batch, seq):
    n = xs.shape[0]
    nt = STEP_CHUNKS
    l = SSD_CHUNK * nt
    nc = seq // l
    row = lambda w: pl.BlockSpec((l, w), lambda b, c: (b * nc + c, 0))
    col = pl.BlockSpec((16, l), lambda b, c: (0, b * nc + c))
    st = pl.BlockSpec((1, nt, SSD_STATE, SSD_INNER), lambda b, c: (b, c, 0, 0))
    return pl.pallas_call(
        _ssd_out_kernel,
        grid=(batch, nc),
        in_specs=[row(SSD_INNER), row(LANES), row(LANES), row(SSD_INNER), row(16), row(16), col, col,
                  st, st, _resident(e.shape), _resident((1, SSD_INNER)), _resident((1, SSD_INNER))],
        out_specs=row(SSD_INNER),
        out_shape=jax.ShapeDtypeStruct((n, SSD_INNER), BF16),
        compiler_params=_cparams("parallel", "parallel"),
        name="ssd_out",
    )(xs, bm, cm, z, dt_c, cs_c, dt_r, cs_r, sf, sb, e, dskip, ng)


def _ssd_branch(z, xbc, dt_raw, conv_w, conv_b, dt_bias, a_log, d_skip, norm_g, e, batch, seq):
    xs, bm, cm = _ssd_conv(xbc, conv_w, conv_b, seq)
    a = -jnp.exp(a_log.astype(F32)).reshape(2 * SSD_HEADS)
    dt_r, cs_r = _ssd_dt(dt_raw[:, :2 * SSD_HEADS].T, dt_bias.reshape(2 * SSD_HEADS), a)
    dt_c = dt_r.T
    cs_c = cs_r.T
    sf, sb = _ssd_states(xs, bm, dt_c, cs_c, e, batch, seq)
    dskip = jnp.repeat(d_skip.astype(F32), SSD_HEAD_DIM).reshape(1, SSD_INNER)
    return _ssd_out(xs, bm, cm, z, dt_c, cs_c, dt_r, cs_r, sf, sb, e, dskip,
                    norm_g.reshape(1, SSD_INNER), batch, seq)


def _s5_matrices(lam_re, lam_im, log_step, b_re, b_im, c_re, c_im, d_s5, nsteps):
    q = S5_CHUNK
    hi = lax.Precision.HIGHEST
    lr = jnp.minimum(lam_re.astype(F32), S5_MAX_RE)
    li = lam_im.astype(F32)
    step = jnp.exp(log_step.astype(F32))[..., None]
    tau = jnp.arange(q + 1, dtype=F32)[:, None, None, None]
    mag = jnp.exp(lr * step * tau)
    ang = li * step * tau
    pr = mag * jnp.cos(ang)
    pi = mag * jnp.sin(ang)
    den = lr * lr + li * li
    nr = pr[1] - 1.0
    coef_re = ((nr * lr + pi[1] * li) / den)[..., None]
    coef_im = ((pi[1] * lr - nr * li) / den)[..., None]
    br = b_re.astype(F32)
    bi = b_im.astype(F32)
    bb_re = coef_re * br - coef_im * bi
    bb_im = coef_re * bi + coef_im * br
    cr = c_re.astype(F32)
    ci = c_im.astype(F32)
    cp_re = cr[None] * pr[:, :, :, None, :] - ci[None] * pi[:, :, :, None, :]
    cp_im = cr[None] * pi[:, :, :, None, :] + ci[None] * pr[:, :, :, None, :]
    kern = (jnp.einsum('tdgip,dgpj->dgjti', cp_re[:q], bb_re, precision=hi)
            - jnp.einsum('tdgip,dgpj->dgjti', cp_im[:q], bb_im, precision=hi))
    kf, kb = kern[0], kern[1]
    centre = kf[:, :, 0] + kb[:, :, 0] + jnp.eye(S5_GROUP_CH, dtype=F32)[None] * d_s5.astype(F32)[:, None, :]
    kcat = jnp.concatenate([kb[:, :, :0:-1], centre[:, :, None], kf[:, :, 1:]], axis=2)
    kcat = kcat.reshape(S5_GROUPS, S5_GROUP_CH, (2 * q - 1) * S5_GROUP_CH)
    kcat = jnp.pad(kcat, ((0, 0), (0, 0), (0, S5_GROUP_CH)))

    def to_state(p_r, p_i, d):
        p_r = p_r.transpose(1, 0, 2)[:, :, None, :]
        p_i = p_i.transpose(1, 0, 2)[:, :, None, :]
        b_r = bb_re[d].transpose(0, 2, 1)[:, None]
        b_i = bb_im[d].transpose(0, 2, 1)[:, None]
        return p_r * b_r - p_i * b_i, p_r * b_i + p_i * b_r

    sf_r, sf_i = to_state(pr[:q, 0][::-1], pi[:q, 0][::-1], 0)
    sb_r, sb_i = to_state(pr[:q, 1], pi[:q, 1], 1)
    bmat = jnp.concatenate([sf_r, sb_r, sf_i, sb_i], axis=-1)
    bmat = bmat.reshape(S5_GROUPS, q * S5_GROUP_CH, 4 * S5_STATE)

    def from_state(p_r, p_i, d):
        p_r = p_r.transpose(1, 2, 0)[:, :, :, None]
        p_i = p_i.transpose(1, 2, 0)[:, :, :, None]
        c_r = cr[d].transpose(0, 2, 1)[:, :, None, :]
        c_i = ci[d].transpose(0, 2, 1)[:, :, None, :]
        return c_r * p_r - c_i * p_i, c_r * p_i + c_i * p_r

    of_r, of_i = from_state(pr[1:q + 1, 0], pi[1:q + 1, 0], 0)
    ob_r, ob_i = from_state(pr[1:q + 1, 1][::-1], pi[1:q + 1, 1][::-1], 1)
    cmat = jnp.concatenate([of_r, ob_r, -of_i, -ob_i], axis=1)
    cmat = cmat.reshape(S5_GROUPS, 4 * S5_STATE, q * S5_GROUP_CH)
    m = (q * 2.0 ** jnp.arange(nsteps, dtype=F32))[:, None, None, None]
    amag = jnp.exp(lr * step * m)
    aang = li * step * m
    lanes = lambda t: jnp.concatenate([t[:, 0], t[:, 1]], axis=-1).transpose(1, 0, 2)
    return kcat, bmat.astype(BF16), cmat.astype(BF16), lanes(amag * jnp.cos(aang)), lanes(amag * jnp.sin(aang))


def _s5_gather_kernel(*refs):
    u_refs, o_ref = refs[:-1], refs[-1]
    q = S5_CHUNK
    rb = o_ref.shape[1]
    per = LANES // S5_GROUP_CH
    slot = lax.broadcasted_iota(jnp.int32, (rb, LANES), 1) // S5_GROUP_CH
    for a in range(S5_GROUPS // per):
        for a2 in range(q // per):
            acc = [None] * per
            for sl in range(per):
                src = u_refs[a][pl.ds(a2 * per + sl, rb, stride=q), :]
                for gl in range(per):
                    piece = src if sl == gl else pltpu.roll(src, ((sl - gl) * S5_GROUP_CH) % LANES, 1)
                    acc[gl] = piece if sl == 0 else jnp.where(slot == sl, piece, acc[gl])
            for gl in range(per):
                o_ref[a * per + gl, :, a2 * LANES:(a2 + 1) * LANES] = acc[gl].astype(o_ref.dtype)


def _s5_gather(u_tiles):
    n = u_tiles[0].shape[0]
    q = S5_CHUNK
    rb = S5_ROWS
    w = q * S5_GROUP_CH
    return pl.pallas_call(
        _s5_gather_kernel,
        grid=(n // (rb * q),),
        in_specs=[pl.BlockSpec((rb * q, LANES), lambda i: (i, 0)) for _ in u_tiles],
        out_specs=pl.BlockSpec((S5_GROUPS, rb, w), lambda i: (0, i, 0)),
        out_shape=jax.ShapeDtypeStruct((S5_GROUPS, n // q, w), BF16),
        compiler_params=_cparams("parallel"),
        name="s5_gather",
    )(*u_tiles)


def _s5_core_kernel(u_ref, k_ref, b_ref, c_ref, apr_ref, api_ref, o_ref, toe, *, nc):
    q = S5_CHUNK
    w = q * S5_GROUP_CH
    r = u_ref.shape[1]
    p2 = 2 * S5_STATE
    kc = k_ref[0]
    for s in range(q):
        off = (q - 1 - s) * S5_GROUP_CH
        win = kc if off == 0 else pltpu.roll(kc, 2 * w - off, 1)
        toe[s * S5_GROUP_CH:(s + 1) * S5_GROUP_CH, :] = win[:, :w].astype(BF16)
    u = u_ref[0]
    loc = _dot(u, b_ref[0])
    xr, xi = loc[:, :p2], loc[:, p2:]
    row = lax.broadcasted_iota(jnp.int32, (r, p2), 0) % nc
    fwd = lax.broadcasted_iota(jnp.int32, (r, p2), 1) < S5_STATE

    def shift(v, k):
        dn = jnp.where(row >= k, pltpu.roll(v, k, 0), 0.0)
        up = jnp.where(row < nc - k, pltpu.roll(v, r - k, 0), 0.0)
        return jnp.where(fwd, dn, up)

    k, i = 1, 0
    while k < nc:
        sr, si = shift(xr, k), shift(xi, k)
        ar, ai = apr_ref[0, i:i + 1, :], api_ref[0, i:i + 1, :]
        xr, xi = xr + (ar * sr - ai * si), xi + (ar * si + ai * sr)
        k, i = 2 * k, i + 1
    h = jnp.concatenate([shift(xr, 1), shift(xi, 1)], axis=1).astype(BF16)
    y = _dot(u, toe[...]) + _dot(h, c_ref[0])
    o_ref[0] = jax.nn.gelu(y, approximate=True).astype(o_ref.dtype)


def _s5_core(u_g, kcat, bmat, cmat, apr, api, batch):
    g, r, w = u_g.shape
    blk = lambda a: pl.BlockSpec((1,) + a.shape[1:], lambda i: (i, 0, 0))
    return pl.pallas_call(
        functools.partial(_s5_core_kernel, nc=r // batch),
        grid=(g,),
        in_specs=[blk(u_g), blk(kcat), blk(bmat), blk(cmat), blk(apr), blk(api)],
        out_specs=blk(u_g),
        out_shape=jax.ShapeDtypeStruct((g, r, w), BF16),
        scratch_shapes=[pltpu.VMEM((w, w), BF16)],
        compiler_params=_cparams("parallel"),
        name="s5_core",
    )(u_g, kcat, bmat, cmat, apr, api)


def _s5_scatter_kernel(y_ref, wv_ref, wg_ref, o_ref, *nat):
    q = S5_CHUNK
    rb = y_ref.shape[1]
    per = LANES // S5_GROUP_CH
    slot = lax.broadcasted_iota(jnp.int32, (rb, LANES), 1) // S5_GROUP_CH
    for a in range(S5_GROUPS // per):
        for a2 in range(q // per):
            acc = [None] * per
            for gl in range(per):
                src = y_ref[a * per + gl, :, a2 * LANES:(a2 + 1) * LANES].astype(F32)
                for sl in range(per):
                    piece = src if gl == sl else pltpu.roll(src, ((gl - sl) * S5_GROUP_CH) % LANES, 1)
                    acc[sl] = piece if gl == 0 else jnp.where(slot == gl, piece, acc[sl])
            for sl in range(per):
                nat[a][pl.ds(a2 * per + sl, rb, stride=q), :] = acc[sl]
    y = jnp.concatenate([t[...] for t in nat], axis=1).astype(BF16)
    o_ref[...] = (_dot(y, wv_ref[...]) * jax.nn.sigmoid(_dot(y, wg_ref[...]))).astype(o_ref.dtype)


def _s5_scatter(y_g, wv, wg):
    g, r, w = y_g.shape
    q = S5_CHUNK
    rb = S5_ROWS
    return pl.pallas_call(
        _s5_scatter_kernel,
        grid=(r // rb,),
        in_specs=[pl.BlockSpec((g, rb, w), lambda i: (0, i, 0)), _resident(wv.shape), _resident(wg.shape)],
        out_specs=pl.BlockSpec((rb * q, S5_WIDTH), lambda i: (i, 0)),
        out_shape=jax.ShapeDtypeStruct((r * q, S5_WIDTH), BF16),
        scratch_shapes=[pltpu.VMEM((rb * q, LANES), F32)] * (S5_WIDTH // LANES),
        compiler_params=_cparams("parallel"),
        name="s5_scatter",
    )(y_g, wv, wg)


def _s5_branch(u_tiles, mats, wv, wg, batch):
    kcat, bmat, cmat, apr, api = mats
    y_g = _s5_core(_s5_gather(u_tiles), kcat, bmat, cmat, apr, api, batch)
    return _s5_scatter(y_g, wv, wg)


def _rotary(t, cos, sin_signed):
    w = t.shape[1]
    half = RET_HEAD_DIM // 2
    lane = lax.broadcasted_iota(jnp.int32, t.shape, 1) % RET_HEAD_DIM
    partner = jnp.where(lane < half, pltpu.roll(t, w - half, 1), pltpu.roll(t, half, 1))
    reps = w // cos.shape[1]
    return t * jnp.tile(cos, (1, reps)) + partner * jnp.tile(sin_signed, (1, reps))


def _ret_state_kernel(k_f, v_f, k_b, v_b, kdf_ref, kdb_ref, gl_ref, sf_out, sb_out, sf, sb):
    c = pl.program_id(1)
    l = RET_CHUNK
    nt = k_f.shape[0] // l
    pairs = RET_WIDTH // LANES

    @pl.when(c == 0)
    def _():
        sf[...] = jnp.zeros_like(sf)
        sb[...] = jnp.zeros_like(sb)

    same_head = (lax.broadcasted_iota(jnp.int32, (LANES, LANES), 0) // RET_HEAD_DIM
                 == lax.broadcasted_iota(jnp.int32, (LANES, LANES), 1) // RET_HEAD_DIM)
    for t in range(nt):
        rf = slice(t * l, (t + 1) * l)
        rb = slice((nt - 1 - t) * l, (nt - t) * l)
        sf_out[0, t] = sf[...].astype(sf_out.dtype)
        sb_out[0, nt - 1 - t] = sb[...].astype(sb_out.dtype)
        kf = (k_f[rf, :].astype(F32) * kdf_ref[...]).astype(BF16)
        kb = (k_b[rb, :].astype(F32) * kdb_ref[...]).astype(BF16)
        for p in range(pairs):
            sl = slice(p * LANES, (p + 1) * LANES)
            new_f = jnp.where(same_head, _dot_tn(kf[:, sl], v_f[rf, sl]), 0.0)
            new_b = jnp.where(same_head, _dot_tn(kb[:, sl], v_b[rb, sl]), 0.0)
            sf[p] = sf[p] * gl_ref[:, sl] + new_f
            sb[p] = sb[p] * gl_ref[:, sl] + new_b


def _ret_states(k, v, kdf, kdb, gl, batch, seq):
    nt = STEP_CHUNKS
    l = RET_CHUNK * nt
    nb = seq // l
    pairs = RET_WIDTH // LANES
    fwd = pl.BlockSpec((l, RET_WIDTH), lambda b, c: (b * nb + c, 0))
    bwd = pl.BlockSpec((l, RET_WIDTH), lambda b, c: (b * nb + nb - 1 - c, 0))
    st = (batch, nb * nt, pairs, LANES, LANES)
    return pl.pallas_call(
        _ret_state_kernel,
        grid=(batch, nb),
        in_specs=[fwd, fwd, bwd, bwd, _resident(kdf.shape), _resident(kdb.shape), _resident(gl.shape)],
        out_specs=[pl.BlockSpec((1, nt, pairs, LANES, LANES), lambda b, c: (b, c, 0, 0, 0)),
                   pl.BlockSpec((1, nt, pairs, LANES, LANES), lambda b, c: (b, nb - 1 - c, 0, 0, 0))],
        out_shape=[jax.ShapeDtypeStruct(st, BF16)] * 2,
        scratch_shapes=[pltpu.VMEM((pairs, LANES, LANES), F32)] * 2,
        compiler_params=_cparams("parallel", "arbitrary"),
        name="ret_states",
    )(k, v, k, v, kdf, kdb, gl)


def _ret_out_kernel(q_ref, k_ref, v_ref, g_ref, sf_ref, sb_ref, dm_ref, qdf_ref, qdb_ref, ng_ref, o_ref):
    l = RET_CHUNK
    hd = RET_HEAD_DIM
    low = lax.broadcasted_iota(jnp.int32, (l, LANES), 1) < hd

    def block_diag(t):
        zero = jnp.zeros_like(t)
        return jnp.concatenate([jnp.where(low, t, zero), jnp.where(low, zero, t)], axis=0)

    for t in range(q_ref.shape[0] // l):
        rows = slice(t * l, (t + 1) * l)
        q = q_ref[rows, :]
        qf = q.astype(F32)
        q_lf = (qf * qdf_ref[...]).astype(BF16)
        q_lb = (qf * qdb_ref[...]).astype(BF16)
        ys = []
        for p in range(RET_WIDTH // LANES):
            sl = slice(p * LANES, (p + 1) * LANES)
            s = _dot_nt(q[:, sl], block_diag(k_ref[rows, sl]))
            y = _dot((s * dm_ref[p]).astype(BF16), block_diag(v_ref[rows, sl]))
            state = jnp.concatenate([sf_ref[0, t, p], sb_ref[0, t, p]], axis=0)
            y = y + _dot(jnp.concatenate([q_lf[:, sl], q_lb[:, sl]], axis=1), state)
            y2 = y * y
            ms_a = jnp.sum(jnp.where(low, y2, 0.0), axis=-1, keepdims=True) * (1.0 / hd)
            ms_b = jnp.sum(jnp.where(low, 0.0, y2), axis=-1, keepdims=True) * (1.0 / hd)
            ys.append(y * jnp.where(low, lax.rsqrt(ms_a + EPS), lax.rsqrt(ms_b + EPS)))
        y = jnp.concatenate(ys, axis=1) * ng_ref[...]
        o_ref[rows, :] = (_silu(g_ref[rows, :].astype(F32)) * y).astype(o_ref.dtype)


def _ret_out(q, k, v, g, sf, sb, dm, qdf, qdb, ng, batch, seq):
    n = q.shape[0]
    nt = STEP_CHUNKS
    l = RET_CHUNK * nt
    nc = seq // l
    pairs = RET_WIDTH // LANES
    row = pl.BlockSpec((l, RET_WIDTH), lambda b, c: (b * nc + c, 0))
    st = pl.BlockSpec((1, nt, pairs, LANES, LANES), lambda b, c: (b, c, 0, 0, 0))
    return pl.pallas_call(
        _ret_out_kernel,
        grid=(batch, nc),
        in_specs=[row, row, row, row, st, st,
                  _resident(dm.shape), _resident(qdf.shape), _resident(qdb.shape), _resident((1, RET_WIDTH))],
        out_specs=row,
        out_shape=jax.ShapeDtypeStruct((n, RET_WIDTH), BF16),
        compiler_params=_cparams("parallel", "parallel"),
        name="ret_out",
    )(q, k, v, g, sf, sb, dm, qdf, qdb, ng)


def _ret_tables(seq):
    l = RET_CHUNK
    hd = RET_HEAD_DIM
    pos = jnp.arange(seq, dtype=F32)
    inv_freq = ROPE_BASE ** (-jnp.arange(0, hd, 2, dtype=F32) / hd)
    ang = pos[:, None] * inv_freq[None, :]
    cos = jnp.tile(jnp.cos(ang), (1, LANES // (hd // 2)))
    sin = jnp.sin(ang)
    sin_s = jnp.tile(jnp.concatenate([-sin, sin], axis=1), (1, LANES // hd))
    log_gamma = jnp.log1p(-jnp.exp2(-5.0 - jnp.arange(RET_HEADS, dtype=F32)))
    idx = jnp.arange(l, dtype=F32)
    dm = jnp.exp(log_gamma[:, None, None] * jnp.abs(idx[:, None] - idx[None, :]))
    dm = jnp.concatenate([dm[0::2], dm[1::2]], axis=2)
    per_head = lambda t: jnp.repeat(t, hd, axis=1)
    kdf = per_head(jnp.exp(log_gamma[None, :] * (l - 1.0 - idx)[:, None]))
    kdb = per_head(jnp.exp(log_gamma[None, :] * idx[:, None]))
    qdf = per_head(jnp.exp(log_gamma[None, :] * (idx + 1.0)[:, None]))
    qdb = per_head(jnp.exp(log_gamma[None, :] * (l - idx)[:, None]))
    gl = per_head(jnp.exp(log_gamma * l)[None, :])
    return cos, sin_s, dm, kdf, kdb, qdf, qdb, gl


def _ret_branch(q, k, v, g, norm_g, tables, batch, seq):
    _, _, dm, kdf, kdb, qdf, qdb, gl = tables
    sf, sb = _ret_states(k, v, kdf, kdb, gl, batch, seq)
    return _ret_out(q, k, v, g, sf, sb, dm, qdf, qdb, norm_g.reshape(1, RET_WIDTH), batch, seq)


def _merge_kernel(x_ref, ya_ref, yb_ref, yc_ref, g_ref, wgate_ref, bgate_ref, wa_ref, wb_ref, wc_ref, wo_ref,
                  o_ref):
    x = x_ref[...]
    d = x.shape[1]
    h = _rms(x, g_ref[...]).astype(BF16)
    branches = (_dot(ya_ref[...], wa_ref[...]), _dot(yb_ref[...], wb_ref[...]), _dot(yc_ref[...], wc_ref[...]))
    mixed = jnp.zeros(x.shape, F32)
    for i, y in enumerate(branches):
        gate = jax.nn.sigmoid(_dot(h, wgate_ref[:, i * d:(i + 1) * d]) + bgate_ref[:, i * d:(i + 1) * d])
        mixed = mixed + gate * y
    o_ref[...] = x + _dot(mixed.astype(BF16), wo_ref[...])


def _merge(x, ya, yb, yc, g, wgate, bgate, wa, wb, wc, wo):
    n, d = x.shape
    tm = TOKEN_TILE
    row = lambda c: pl.BlockSpec((tm, c), lambda i: (i, 0))
    consts = (g.reshape(1, d), wgate, bgate.reshape(1, -1), wa, wb, wc, wo)
    return pl.pallas_call(
        _merge_kernel,
        grid=(n // tm,),
        in_specs=[row(d), row(ya.shape[1]), row(yb.shape[1]), row(yc.shape[1])]
                 + [_resident(c.shape) for c in consts],
        out_specs=row(d),
        out_shape=jax.ShapeDtypeStruct((n, d), F32),
        compiler_params=_cparams("parallel"),
        name="merge",
    )(x, ya, yb, yc, *consts)


def _head_expander():
    return jnp.tile(jnp.repeat(jnp.eye(2 * SSD_HEADS, dtype=BF16), SSD_HEAD_DIM, axis=1), (3, 1))


def kernel(x, ffn1_norm, ffn1_w_gate, ffn1_w_up, ffn1_w_down, mix_norm, w_in, b_gate, ssd_conv_w, ssd_conv_b, ssd_dt_bias, ssd_a_log, ssd_d, ssd_norm, w_br_ssd, s5_lam_re, s5_lam_im, s5_log_step, s5_b_re, s5_b_im, s5_c_re, s5_c_im, s5_d, s5_glu_wv, s5_glu_wg, w_br_s5, ret_norm, w_br_ret, w_out, ffn2_norm, ffn2_w_gate, ffn2_w_up, ffn2_w_down, final_norm):
    batch, seq, d = x.shape
    depth = w_in.shape[0]
    n = batch * seq
    assert seq % CONV_TILE == 0 and seq % SSD_CHUNK == 0 and seq % RET_CHUNK == 0 and seq % S5_CHUNK == 0
    assert n % TOKEN_TILE == 0 and n % DT_TILE == 0 and n % (S5_ROWS * S5_CHUNK) == 0
    s5_steps = (seq // S5_CHUNK - 1).bit_length()
    assert 2 ** s5_steps == seq // S5_CHUNK
    bf = lambda w: w.astype(BF16)
    e = _head_expander()
    tables = _ret_tables(seq)
    offs = [0]
    for s in IN_PROJ_SIZES:
        offs.append(offs[-1] + s)
    xf = x.reshape(n, d).astype(F32)
    for i in range(depth):
        xf = _ffn(xf, ffn1_norm[i], bf(ffn1_w_gate[i]), bf(ffn1_w_up[i]), bf(ffn1_w_down[i]), final_norm, False)
        w = w_in[i]
        seg = [w[:, offs[j]:offs[j + 1]] for j in range(8)]
        seg[2] = jnp.pad(seg[2], ((0, 0), (0, LANES - seg[2].shape[1])))
        seg[3:4] = [seg[3][:, c:c + LANES] for c in range(0, S5_WIDTH, LANES)]
        z, xbc, dt_raw, u0, u1, u2, q, k, v, g = _inproj(
            xf, mix_norm[i], tables[0], tables[1], [bf(s) for s in seg],
            [BF16, BF16, F32, F32, F32, F32, BF16, BF16, BF16, BF16],
            (None,) * 6 + (1.0, RET_HEAD_DIM ** -0.5, None, None), seq)
        ya = _ssd_branch(z, xbc, dt_raw, ssd_conv_w[i].astype(F32), ssd_conv_b[i].astype(F32), ssd_dt_bias[i],
                         ssd_a_log[i], ssd_d[i], ssd_norm[i].astype(F32), e, batch, seq)
        mats = _s5_matrices(s5_lam_re[i], s5_lam_im[i], s5_log_step[i], s5_b_re[i], s5_b_im[i],
                            s5_c_re[i], s5_c_im[i], s5_d[i], s5_steps)
        yb = _s5_branch((u0, u1, u2), mats, bf(s5_glu_wv[i]), bf(s5_glu_wg[i]), batch)
        yc = _ret_branch(q, k, v, g, ret_norm[i].astype(F32), tables, batch, seq)
        xf = _merge(xf, ya, yb, yc, mix_norm[i], bf(w[:, offs[8]:]), b_gate[i].astype(F32), bf(w_br_ssd[i]),
                    bf(w_br_s5[i]), bf(w_br_ret[i]), bf(w_out[i]))
        xf = _ffn(xf, ffn2_norm[i], bf(ffn2_w_gate[i]), bf(ffn2_w_up[i]), bf(ffn2_w_down[i]), final_norm,
                  i == depth - 1)
    return xf.reshape(batch, seq, d).astype(x.dtype)
```

```python
import functools
import math

import jax
import jax.numpy as jnp
from jax import lax
from jax.experimental import pallas as pl
from jax.experimental.pallas import tpu as pltpu

F32 = jnp.float32
BF16 = jnp.bfloat16

EPS = 1e-6
LOG2E = 1.4426950408889634
SSD_HEADS = 8
SSD_HEAD_DIM = 64
SSD_INNER = 512
SSD_GROUPS = 2
SSD_STATE = 64
SSD_CONV = 5
SSD_CONV_CH = 768
S5_GROUP_CH = 16
S5_GROUPS = 24
S5_WIDTH = 384
S5_STATE = 64
S5_MAX_RE = -1e-4
RET_HEADS = 8
RET_HEAD_DIM = 64
RET_WIDTH = 512
ROPE_BASE = 10000.0
N_BRANCH = 3
IN_PROJ_SIZES = (512, 768, 16, 384, 512, 512, 512, 512, 3072)

V7X_VMEM_BYTES = 64 * 1024 * 1024
VMEM_LIMIT = V7X_VMEM_BYTES - 8 * 1024 * 1024
LANES = 128
BF16_SUBLANES = 16

TOKEN_TILE = 512
WIDE_TOKEN_TILE = 1024
SSD_CHUNK = 128
RET_CHUNK = 128
STEP_CHUNKS = 8
S5_CHUNK = 32
S5_ROWS = 64
CONV_TILE = 512
DT_TILE = 2048


def _cparams(*sem):
    return pltpu.CompilerParams(dimension_semantics=sem, vmem_limit_bytes=VMEM_LIMIT)


def _resident(shape):
    nd = len(shape)
    return pl.BlockSpec(shape, lambda *_: (0,) * nd, pipeline_mode=pl.Buffered(1))


def _rms(x, g):
    return (x * lax.rsqrt(jnp.mean(x * x, axis=-1, keepdims=True) + EPS)) * g


def _silu(x):
    return x * jax.nn.sigmoid(x)


def _dot(a, b):
    return jnp.dot(a, b, preferred_element_type=F32)


def _dot_nt(a, b):
    return lax.dot_general(a, b, (((1,), (1,)), ((), ())), preferred_element_type=F32)


def _dot_tn(a, b):
    return lax.dot_general(a, b, (((0,), (0,)), ((), ())), preferred_element_type=F32)


def _split3(v):
    hi = v.astype(BF16)
    r1 = v - hi.astype(F32)
    mid = r1.astype(BF16)
    lo = (r1 - mid.astype(F32)).astype(BF16)
    return hi, mid, lo


def _expand_heads(v, e3):
    return _dot(jnp.concatenate(_split3(v), axis=1), e3)


def _ffn_kernel(x_ref, g_ref, wg_ref, wu_ref, wd_ref, fg_ref, o_ref, *, final):
    x = x_ref[...]
    h = _rms(x, g_ref[...]).astype(BF16)
    a = _dot(h, wg_ref[...])
    b = _dot(h, wu_ref[...])
    t = (_silu(a) * b).astype(BF16)
    y = x + 0.5 * _dot(t, wd_ref[...])
    if final:
        y = _rms(y, fg_ref[...])
    o_ref[...] = y


def _ffn(x, g, wg, wu, wd, fg, final):
    n, d = x.shape
    f = wg.shape[1]
    tm = TOKEN_TILE
    row = pl.BlockSpec((tm, d), lambda i: (i, 0))
    return pl.pallas_call(
        functools.partial(_ffn_kernel, final=final),
        grid=(n // tm,),
        in_specs=[row, _resident((1, d)), _resident((d, f)), _resident((d, f)), _resident((f, d)),
                  _resident((1, d))],
        out_specs=row,
        out_shape=jax.ShapeDtypeStruct((n, d), F32),
        compiler_params=_cparams("parallel"),
        name="ffn",
    )(x, g.reshape(1, d), wg, wu, wd, fg.reshape(1, d))


def _inproj_kernel(x_ref, g_ref, cos_ref, sin_ref, *refs, rope_scale):
    nw = len(refs) // 2
    h = _rms(x_ref[...], g_ref[...]).astype(BF16)
    for w_ref, o_ref, scale in zip(refs[:nw], refs[nw:], rope_scale):
        y = _dot(h, w_ref[...])
        if scale is not None:
            y = _rotary(y, cos_ref[...], sin_ref[...])
            if scale != 1.0:
                y = y * scale
        o_ref[...] = y.astype(o_ref.dtype)


def _inproj(x, g, cos, sin_s, weights, out_dtypes, rope_scale, seq):
    n, d = x.shape
    tm = WIDE_TOKEN_TILE
    row = lambda c: pl.BlockSpec((tm, c), lambda i: (i, 0))
    pos = pl.BlockSpec((tm, LANES), lambda i: (i % (seq // tm), 0))
    return pl.pallas_call(
        functools.partial(_inproj_kernel, rope_scale=rope_scale),
        grid=(n // tm,),
        in_specs=[row(d), _resident((1, d)), pos, pos] + [_resident(w.shape) for w in weights],
        out_specs=[row(w.shape[1]) for w in weights],
        out_shape=[jax.ShapeDtypeStruct((n, w.shape[1]), dt) for w, dt in zip(weights, out_dtypes)],
        compiler_params=_cparams("parallel"),
        name="inproj",
    )(x, g.reshape(1, d), cos, sin_s, *weights)


def _ssd_conv_kernel(prev_ref, x_ref, next_ref, w_ref, b_ref, xs_ref, bm_ref, cm_ref, *, tiles_per_seq):
    i = pl.program_id(0)
    t = x_ref.shape[0]
    first = (i % tiles_per_seq) == 0
    last = (i % tiles_per_seq) == tiles_per_seq - 1
    x = x_ref[...].astype(F32)
    p = jnp.where(first, 0.0, prev_ref[...].astype(F32))
    nx = jnp.where(last, 0.0, next_ref[...].astype(F32))
    ext = jnp.concatenate([p[8:16], x, nx[0:8]], axis=0)
    rows = t + 16
    acc = jnp.broadcast_to(b_ref[...], x.shape)
    half = SSD_CONV // 2
    for k in range(SSD_CONV):
        shifted = pltpu.roll(ext, (half - k) % rows, 0)
        acc = acc + w_ref[k:k + 1, :] * shifted[8:8 + t]
    y = _silu(acc)
    xs_ref[...] = y[:, :SSD_INNER].astype(xs_ref.dtype)
    bm_ref[...] = y[:, SSD_INNER:SSD_INNER + LANES].astype(bm_ref.dtype)
    cm_ref[...] = y[:, SSD_INNER + LANES:].astype(cm_ref.dtype)


def _ssd_conv(xbc, w, b, seq):
    n, c = xbc.shape
    t = CONV_TILE
    hb = BF16_SUBLANES
    nblk = n // hb
    return pl.pallas_call(
        functools.partial(_ssd_conv_kernel, tiles_per_seq=seq // t),
        grid=(n // t,),
        in_specs=[pl.BlockSpec((hb, c), lambda i: (jnp.maximum(i * (t // hb) - 1, 0), 0)),
                  pl.BlockSpec((t, c), lambda i: (i, 0)),
                  pl.BlockSpec((hb, c), lambda i: (jnp.minimum((i + 1) * (t // hb), nblk - 1), 0)),
                  _resident(w.shape), _resident((1, c))],
        out_specs=[pl.BlockSpec((t, SSD_INNER), lambda i: (i, 0)),
                   pl.BlockSpec((t, LANES), lambda i: (i, 0)),
                   pl.BlockSpec((t, LANES), lambda i: (i, 0))],
        out_shape=[jax.ShapeDtypeStruct((n, SSD_INNER), BF16),
                   jax.ShapeDtypeStruct((n, LANES), BF16),
                   jax.ShapeDtypeStruct((n, LANES), BF16)],
        compiler_params=_cparams("parallel"),
        name="ssd_conv",
    )(xbc, xbc, xbc, w, b.reshape(1, c))


def _ssd_dt_kernel(raw_ref, bias_ref, a_ref, dt_ref, cs_ref, dtc_ref, csc_ref, *, chunk):
    nh = 2 * SSD_HEADS
    x = raw_ref[...].T[:nh] + bias_ref[...]
    dt = jnp.maximum(x, 0.0) + jnp.log1p(jnp.exp(-jnp.abs(x)))
    cs = dt * a_ref[...]
    t = x.shape[1]
    pos = lax.broadcasted_iota(jnp.int32, x.shape, 1) % chunk
    fwd = lax.broadcasted_iota(jnp.int32, x.shape, 0) < SSD_HEADS
    k = 1
    while k < chunk:
        before = jnp.where(pos >= k, pltpu.roll(cs, k, 1), 0.0)
        after = jnp.where(pos < chunk - k, pltpu.roll(cs, t - k, 1), 0.0)
        cs = cs + jnp.where(fwd, before, after)
        k *= 2
    dt_ref[...] = dt
    cs_ref[...] = cs
    both = jnp.concatenate([dt, cs, jnp.zeros((LANES - 2 * nh, t), F32)], axis=0).T
    dtc_ref[...] = both[:, :nh]
    csc_ref[...] = both[:, nh:2 * nh]


def _ssd_dt(raw, bias, a):
    n = raw.shape[0]
    r = 2 * SSD_HEADS
    t = DT_TILE
    rows = pl.BlockSpec((r, t), lambda i: (0, i))
    cols = pl.BlockSpec((t, r), lambda i: (i, 0))
    return pl.pallas_call(
        functools.partial(_ssd_dt_kernel, chunk=SSD_CHUNK),
        grid=(n // t,),
        in_specs=[pl.BlockSpec((t, LANES), lambda i: (i, 0)), _resident((r, 1)), _resident((r, 1))],
        out_specs=[rows, rows, cols, cols],
        out_shape=[jax.ShapeDtypeStruct((r, n), F32)] * 2 + [jax.ShapeDtypeStruct((n, r), F32)] * 2,
        compiler_params=_cparams("parallel"),
        name="ssd_dt",
    )(raw, bias.reshape(r, 1), a.reshape(r, 1))


def _ssd_chunk_state(xs, bm, wexp):
    xw = (xs.astype(F32) * wexp).astype(BF16)
    gw = SSD_INNER // SSD_GROUPS
    parts = [_dot_tn(bm[:, g * SSD_STATE:(g + 1) * SSD_STATE], xw[:, g * gw:(g + 1) * gw])
             for g in range(SSD_GROUPS)]
    return jnp.concatenate(parts, axis=1)


def _ssd_state_kernel(xs_f, bm_f, dtc_f, csc_f, xs_b, bm_b, dtc_b, csc_b, e_ref,
                      sf_out, sb_out, sf, sb):
    c = pl.program_id(1)
    h = SSD_HEADS
    l = SSD_CHUNK
    nt = xs_f.shape[0] // l

    @pl.when(c == 0)
    def _():
        sf[...] = jnp.zeros_like(sf)
        sb[...] = jnp.zeros_like(sb)

    lane = lax.broadcasted_iota(jnp.int32, (1, 2 * h), 1)
    e_f = e_ref[:, :SSD_INNER]
    e_b = e_ref[:, SSD_INNER:]
    for t in range(nt):
        rf = slice(t * l, (t + 1) * l)
        rb = slice((nt - 1 - t) * l, (nt - t) * l)
        sf_out[0, t] = sf[...].astype(sf_out.dtype)
        sb_out[0, nt - 1 - t] = sb[...].astype(sb_out.dtype)
        cs_f = csc_f[rf, :]
        tot_f = cs_f[l - 1:l, :]
        w_f = jnp.exp(tot_f - cs_f) * dtc_f[rf, :]
        cs_b = csc_b[rb, :]
        tot_b = cs_b[0:1, :]
        w_b = jnp.exp(tot_b - cs_b) * dtc_b[rb, :]
        wexp_f = _expand_heads(jnp.where(lane < h, w_f, 0.0), e_f)
        wexp_b = _expand_heads(jnp.where(lane >= h, w_b, 0.0), e_b)
        dec_f = _expand_heads(jnp.broadcast_to(jnp.exp(tot_f), (8, 2 * h)), e_f)[0:1]
        dec_b = _expand_heads(jnp.broadcast_to(jnp.exp(tot_b), (8, 2 * h)), e_b)[0:1]
        sf[...] = sf[...] * dec_f + _ssd_chunk_state(xs_f[rf, :], bm_f[rf, :], wexp_f)
        sb[...] = sb[...] * dec_b + _ssd_chunk_state(xs_b[rb, :], bm_b[rb, :], wexp_b)


def _ssd_states(xs, bm, dt_c, cs_c, e, batch, seq):
    l = SSD_CHUNK
    nt = STEP_CHUNKS
    nb = seq // (l * nt)
    fwd = lambda w: pl.BlockSpec((nt * l, w), lambda b, c: (b * nb + c, 0))
    bwd = lambda w: pl.BlockSpec((nt * l, w), lambda b, c: (b * nb + nb - 1 - c, 0))
    st = (batch, nb * nt, SSD_STATE, SSD_INNER)
    return pl.pallas_call(
        _ssd_state_kernel,
        grid=(batch, nb),
        in_specs=[fwd(SSD_INNER), fwd(LANES), fwd(16), fwd(16),
                  bwd(SSD_INNER), bwd(LANES), bwd(16), bwd(16), _resident(e.shape)],
        out_specs=[pl.BlockSpec((1, nt, SSD_STATE, SSD_INNER), lambda b, c: (b, c, 0, 0)),
                   pl.BlockSpec((1, nt, SSD_STATE, SSD_INNER), lambda b, c: (b, nb - 1 - c, 0, 0))],
        out_shape=[jax.ShapeDtypeStruct(st, BF16)] * 2,
        scratch_shapes=[pltpu.VMEM((SSD_STATE, SSD_INNER), F32)] * 2,
        compiler_params=_cparams("parallel", "arbitrary"),
        name="ssd_states",
    )(xs, bm, dt_c, cs_c, xs, bm, dt_c, cs_c, e)


def _ssd_out_kernel(xs_ref, bm_ref, cm_ref, z_ref, dtc_ref, csc_ref, dtr_ref, csr_ref,
                    sf_ref, sb_ref, e_ref, dskip_ref, ng_ref, o_ref):
    h = SSD_HEADS
    l = SSD_CHUNK
    gw = SSD_INNER // SSD_GROUPS
    ri = lax.broadcasted_iota(jnp.int32, (l, l), 0)
    ci = lax.broadcasted_iota(jnp.int32, (l, l), 1)
    causal = ri >= ci
    diag = ri == ci
    low = lax.broadcasted_iota(jnp.int32, (l, LANES), 1) < LANES // 2

    def block_diag(t):
        zero = jnp.zeros_like(t)
        return jnp.concatenate([jnp.where(low, t, zero), jnp.where(low, zero, t)], axis=0)

    for t in range(xs_ref.shape[0] // l):
        rows = slice(t * l, (t + 1) * l)
        xs = xs_ref[rows, :]
        cm = cm_ref[rows, :]
        csc = csc_ref[rows, :]
        dtr = dtr_ref[:, rows]
        col = csc * LOG2E
        row = jnp.log2(dtr) - csr_ref[:, rows] * LOG2E
        scores = _dot_nt(cm, block_diag(bm_ref[rows, :]))
        ys = []
        for pair in range(h // 2):
            g = pair // (h // 2 // SSD_GROUPS)
            sc = scores[:, g * l:(g + 1) * l]
            ms = []
            for hh in (2 * pair, 2 * pair + 1):
                w = jnp.exp2(jnp.where(causal, col[:, hh:hh + 1] + row[hh:hh + 1, :],
                                       col[:, h + hh:h + hh + 1] + row[h + hh:h + hh + 1, :]))
                w = w + jnp.where(diag, dtr[h + hh:h + hh + 1, :], 0.0)
                ms.append((sc * w).astype(BF16))
            ys.append(_dot(jnp.concatenate(ms, axis=1), block_diag(xs[:, pair * LANES:(pair + 1) * LANES])))
        y = jnp.concatenate(ys, axis=1)
        eo = _expand_heads(jnp.exp(csc), e_ref[...])
        off_f = []
        off_b = []
        for g in range(SSD_GROUPS):
            cg = cm[:, g * SSD_STATE:(g + 1) * SSD_STATE]
            off_f.append(_dot(cg, sf_ref[0, t, :, g * gw:(g + 1) * gw]))
            off_b.append(_dot(cg, sb_ref[0, t, :, g * gw:(g + 1) * gw]))
        y = (y + jnp.concatenate(off_f, axis=1) * eo[:, :SSD_INNER]
             + jnp.concatenate(off_b, axis=1) * eo[:, SSD_INNER:])
        y = y + dskip_ref[...] * xs.astype(F32)
        y = y * _silu(z_ref[rows, :].astype(F32))
        o_ref[rows, :] = _rms(y, ng_ref[...]).astype(o_ref.dtype)


def _ssd_out(xs, bm, cm, z, dt_c, cs_c, dt_r, cs_r, sf, sb, e, dskip, ng, batch, seq):
    n = xs.shape[0]
    nt = STEP_CHUNKS
    l = SSD_CHUNK * nt
    nc = seq // l
    row = lambda w: pl.BlockSpec((l, w), lambda b, c: (b * nc + c, 0))
    col = pl.BlockSpec((16, l), lambda b, c: (0, b * nc + c))
    st = pl.BlockSpec((1, nt, SSD_STATE, SSD_INNER), lambda b, c: (b, c, 0, 0))
    return pl.pallas_call(
        _ssd_out_kernel,
        grid=(batch, nc),
        in_specs=[row(SSD_INNER), row(LANES), row(LANES), row(SSD_INNER), row(16), row(16), col, col,
                  st, st, _resident(e.shape), _resident((1, SSD_INNER)), _resident((1, SSD_INNER))],
        out_specs=row(SSD_INNER),
        out_shape=jax.ShapeDtypeStruct((n, SSD_INNER), BF16),
        compiler_params=_cparams("parallel", "parallel"),
        name="ssd_out",
    )(xs, bm, cm, z, dt_c, cs_c, dt_r, cs_r, sf, sb, e, dskip, ng)


def _ssd_branch(z, xbc, dt_raw, conv_w, conv_b, dt_bias, a_log, d_skip, norm_g, e, batch, seq):
    xs, bm, cm = _ssd_conv(xbc, conv_w, conv_b, seq)
    a = -jnp.exp(a_log.astype(F32)).reshape(2 * SSD_HEADS)
    dt_r, cs_r, dt_c, cs_c = _ssd_dt(dt_raw, dt_bias.reshape(2 * SSD_HEADS), a)
    sf, sb = _ssd_states(xs, bm, dt_c, cs_c, e, batch, seq)
    dskip = jnp.repeat(d_skip.astype(F32), SSD_HEAD_DIM).reshape(1, SSD_INNER)
    return _ssd_out(xs, bm, cm, z, dt_c, cs_c, dt_r, cs_r, sf, sb, e, dskip,
                    norm_g.reshape(1, SSD_INNER), batch, seq)


def _s5_matrices(lam_re, lam_im, log_step, b_re, b_im, c_re, c_im, d_s5, nsteps):
    q = S5_CHUNK
    lr = jnp.minimum(lam_re.astype(F32), S5_MAX_RE)
    li = lam_im.astype(F32)
    step = jnp.exp(log_step.astype(F32))[..., None]
    tau = jnp.arange(q + 1, dtype=F32)[:, None, None, None]
    mag = jnp.exp(lr * step * tau)
    ang = li * step * tau
    pr = mag * jnp.cos(ang)
    pi = mag * jnp.sin(ang)
    den = lr * lr + li * li
    nr = pr[1] - 1.0
    coef_re = ((nr * lr + pi[1] * li) / den)[..., None]
    coef_im = ((pi[1] * lr - nr * li) / den)[..., None]
    br = b_re.astype(F32)
    bi = b_im.astype(F32)
    bb_re = coef_re * br - coef_im * bi
    bb_im = coef_re * bi + coef_im * br
    cr = c_re.astype(F32)
    ci = c_im.astype(F32)

    def to_state(p_r, p_i, d):
        p_r = p_r.transpose(1, 0, 2)[:, :, None, :]
        p_i = p_i.transpose(1, 0, 2)[:, :, None, :]
        b_r = bb_re[d].transpose(0, 2, 1)[:, None]
        b_i = bb_im[d].transpose(0, 2, 1)[:, None]
        return p_r * b_r - p_i * b_i, p_r * b_i + p_i * b_r

    sf_r, sf_i = to_state(pr[:q, 0][::-1], pi[:q, 0][::-1], 0)
    sb_r, sb_i = to_state(pr[:q, 1], pi[:q, 1], 1)
    bmat = jnp.concatenate([sf_r, sb_r, sf_i, sb_i], axis=-1)
    bmat = bmat.reshape(S5_GROUPS, q * S5_GROUP_CH, 4 * S5_STATE)

    def from_state(p_r, p_i, d):
        p_r = p_r.transpose(1, 2, 0)[:, :, :, None]
        p_i = p_i.transpose(1, 2, 0)[:, :, :, None]
        c_r = cr[d].transpose(0, 2, 1)[:, :, None, :]
        c_i = ci[d].transpose(0, 2, 1)[:, :, None, :]
        return c_r * p_r - c_i * p_i, c_r * p_i + c_i * p_r

    gf_r, gf_i = from_state(pr[:q, 0], pi[:q, 0], 0)
    gb_r, gb_i = from_state(pr[:q, 1][::-1], pi[:q, 1][::-1], 1)
    slot_f = lambda t: jnp.pad(t, ((0, 0), (0, 0), (q - 1, 1), (0, 0)))
    slot_b = lambda t: jnp.pad(t, ((0, 0), (0, 0), (0, q), (0, 0)))
    gen_c = jnp.concatenate([slot_f(gf_r), slot_f(gf_i), slot_b(gb_r), slot_b(gb_i)], axis=1)
    gen_c = gen_c.reshape(S5_GROUPS, 4 * S5_STATE, 2 * q * S5_GROUP_CH)
    tr = lambda t: t.transpose(0, 2, 1)
    gen_b = jnp.concatenate([tr(bb_re[0]), -tr(bb_im[0]), tr(bb_re[1]), -tr(bb_im[1])], axis=-1)
    d_gen = jnp.eye(S5_GROUP_CH, dtype=F32)[None] * d_s5.astype(F32)[:, None, :]
    d_gen = jnp.pad(d_gen[:, :, None, :], ((0, 0), (0, 0), (q - 1, q), (0, 0)))
    d_gen = d_gen.reshape(S5_GROUPS, S5_GROUP_CH, 2 * q * S5_GROUP_CH)

    of_r, of_i = from_state(pr[1:q + 1, 0], pi[1:q + 1, 0], 0)
    ob_r, ob_i = from_state(pr[1:q + 1, 1][::-1], pi[1:q + 1, 1][::-1], 1)
    cmat = jnp.concatenate([of_r, ob_r, -of_i, -ob_i], axis=1)
    cmat = cmat.reshape(S5_GROUPS, 4 * S5_STATE, q * S5_GROUP_CH)
    m = (q * 2.0 ** jnp.arange(nsteps, dtype=F32))[:, None, None, None]
    amag = jnp.exp(lr * step * m)
    aang = li * step * m
    lanes = lambda t: jnp.concatenate([t[:, 0], t[:, 1]], axis=-1).transpose(1, 0, 2)
    return (gen_b, gen_c, d_gen, bmat.astype(BF16), cmat.astype(BF16),
            lanes(amag * jnp.cos(aang)), lanes(amag * jnp.sin(aang)))


def _s5_gather_kernel(*refs):
    u_refs, o_ref = refs[:-1], refs[-1]
    q = S5_CHUNK
    rb = o_ref.shape[1]
    per = LANES // S5_GROUP_CH
    slot = lax.broadcasted_iota(jnp.int32, (rb, LANES), 1) // S5_GROUP_CH
    for a in range(S5_GROUPS // per):
        for a2 in range(q // per):
            acc = [None] * per
            for sl in range(per):
                src = u_refs[a][pl.ds(a2 * per + sl, rb, stride=q), :]
                for gl in range(per):
                    piece = src if sl == gl else pltpu.roll(src, ((sl - gl) * S5_GROUP_CH) % LANES, 1)
                    acc[gl] = piece if sl == 0 else jnp.where(slot == sl, piece, acc[gl])
            for gl in range(per):
                o_ref[a * per + gl, :, a2 * LANES:(a2 + 1) * LANES] = acc[gl].astype(o_ref.dtype)


def _s5_gather(u_tiles):
    n = u_tiles[0].shape[0]
    q = S5_CHUNK
    rb = S5_ROWS
    w = q * S5_GROUP_CH
    return pl.pallas_call(
        _s5_gather_kernel,
        grid=(n // (rb * q),),
        in_specs=[pl.BlockSpec((rb * q, LANES), lambda i: (i, 0)) for _ in u_tiles],
        out_specs=pl.BlockSpec((S5_GROUPS, rb, w), lambda i: (0, i, 0)),
        out_shape=jax.ShapeDtypeStruct((S5_GROUPS, n // q, w), BF16),
        compiler_params=_cparams("parallel"),
        name="s5_gather",
    )(*u_tiles)


def _dot_f32(a, b):
    a0, a1, a2 = _split3(a)
    b0, b1, b2 = _split3(b)
    return ((_dot(a0, b0) + (_dot(a0, b1) + _dot(a1, b0)))
            + ((_dot(a0, b2) + _dot(a2, b0)) + _dot(a1, b1)))


def _s5_core_kernel(u_ref, gb_ref, gc_ref, gd_ref, b_ref, c_ref, apr_ref, api_ref, o_ref, toe, *, nc):
    q = S5_CHUNK
    w = q * S5_GROUP_CH
    r = u_ref.shape[1]
    p2 = 2 * S5_STATE
    kc = _dot_f32(gb_ref[0], gc_ref[0]) + gd_ref[0]
    for s in range(q):
        off = (q - 1 - s) * S5_GROUP_CH
        win = kc if off == 0 else pltpu.roll(kc, 2 * w - off, 1)
        toe[s * S5_GROUP_CH:(s + 1) * S5_GROUP_CH, :] = win[:, :w].astype(BF16)
    u = u_ref[0]
    loc = _dot(u, b_ref[0])
    xr, xi = loc[:, :p2], loc[:, p2:]
    row = lax.broadcasted_iota(jnp.int32, (r, p2), 0) % nc
    fwd = lax.broadcasted_iota(jnp.int32, (r, p2), 1) < S5_STATE

    def shift(v, k):
        dn = jnp.where(row >= k, pltpu.roll(v, k, 0), 0.0)
        up = jnp.where(row < nc - k, pltpu.roll(v, r - k, 0), 0.0)
        return jnp.where(fwd, dn, up)

    k, i = 1, 0
    while k < nc:
        sr, si = shift(xr, k), shift(xi, k)
        ar, ai = apr_ref[0, i:i + 1, :], api_ref[0, i:i + 1, :]
        xr, xi = xr + (ar * sr - ai * si), xi + (ar * si + ai * sr)
        k, i = 2 * k, i + 1
    h = jnp.concatenate([shift(xr, 1), shift(xi, 1)], axis=1).astype(BF16)
    y = _dot(u, toe[...]) + _dot(h, c_ref[0])
    o_ref[0] = jax.nn.gelu(y, approximate=True).astype(o_ref.dtype)


def _s5_core(u_g, gen_b, gen_c, d_gen, bmat, cmat, apr, api, batch):
    g, r, w = u_g.shape
    blk = lambda a: pl.BlockSpec((1,) + a.shape[1:], lambda i: (i, 0, 0))
    return pl.pallas_call(
        functools.partial(_s5_core_kernel, nc=r // batch),
        grid=(g,),
        in_specs=[blk(a) for a in (u_g, gen_b, gen_c, d_gen, bmat, cmat, apr, api)],
        out_specs=blk(u_g),
        out_shape=jax.ShapeDtypeStruct((g, r, w), BF16),
        scratch_shapes=[pltpu.VMEM((w, w), BF16)],
        compiler_params=_cparams("parallel"),
        name="s5_core",
    )(u_g, gen_b, gen_c, d_gen, bmat, cmat, apr, api)


def _s5_scatter_kernel(y_ref, wv_ref, wg_ref, o_ref, *nat):
    q = S5_CHUNK
    rb = y_ref.shape[1]
    per = LANES // S5_GROUP_CH
    slot = lax.broadcasted_iota(jnp.int32, (rb, LANES), 1) // S5_GROUP_CH
    for a in range(S5_GROUPS // per):
        for a2 in range(q // per):
            acc = [None] * per
            for gl in range(per):
                src = y_ref[a * per + gl, :, a2 * LANES:(a2 + 1) * LANES].astype(F32)
                for sl in range(per):
                    piece = src if gl == sl else pltpu.roll(src, ((gl - sl) * S5_GROUP_CH) % LANES, 1)
                    acc[sl] = piece if gl == 0 else jnp.where(slot == gl, piece, acc[sl])
            for sl in range(per):
                nat[a][pl.ds(a2 * per + sl, rb, stride=q), :] = acc[sl]
    y = jnp.concatenate([t[...] for t in nat], axis=1).astype(BF16)
    o_ref[...] = (_dot(y, wv_ref[...]) * jax.nn.sigmoid(_dot(y, wg_ref[...]))).astype(o_ref.dtype)


def _s5_scatter(y_g, wv, wg):
    g, r, w = y_g.shape
    q = S5_CHUNK
    rb = S5_ROWS
    return pl.pallas_call(
        _s5_scatter_kernel,
        grid=(r // rb,),
        in_specs=[pl.BlockSpec((g, rb, w), lambda i: (0, i, 0)), _resident(wv.shape), _resident(wg.shape)],
        out_specs=pl.BlockSpec((rb * q, S5_WIDTH), lambda i: (i, 0)),
        out_shape=jax.ShapeDtypeStruct((r * q, S5_WIDTH), BF16),
        scratch_shapes=[pltpu.VMEM((rb * q, LANES), F32)] * (S5_WIDTH // LANES),
        compiler_params=_cparams("parallel"),
        name="s5_scatter",
    )(y_g, wv, wg)


def _s5_branch(u_tiles, mats, wv, wg, batch):
    y_g = _s5_core(_s5_gather(u_tiles), *mats, batch)
    return _s5_scatter(y_g, wv, wg)


def _rotary(t, cos, sin_signed):
    w = t.shape[1]
    half = RET_HEAD_DIM // 2
    lane = lax.broadcasted_iota(jnp.int32, t.shape, 1) % RET_HEAD_DIM
    partner = jnp.where(lane < half, pltpu.roll(t, w - half, 1), pltpu.roll(t, half, 1))
    reps = w // cos.shape[1]
    return t * jnp.tile(cos, (1, reps)) + partner * jnp.tile(sin_signed, (1, reps))


def _ret_state_kernel(k_f, v_f, k_b, v_b, kdf_ref, kdb_ref, gl_ref, sf_out, sb_out, sf, sb):
    c = pl.program_id(1)
    l = RET_CHUNK
    nt = k_f.shape[0] // l
    pairs = RET_WIDTH // LANES

    @pl.when(c == 0)
    def _():
        sf[...] = jnp.zeros_like(sf)
        sb[...] = jnp.zeros_like(sb)

    same_head = (lax.broadcasted_iota(jnp.int32, (LANES, LANES), 0) // RET_HEAD_DIM
                 == lax.broadcasted_iota(jnp.int32, (LANES, LANES), 1) // RET_HEAD_DIM)
    for t in range(nt):
        rf = slice(t * l, (t + 1) * l)
        rb = slice((nt - 1 - t) * l, (nt - t) * l)
        sf_out[0, t] = sf[...].astype(sf_out.dtype)
        sb_out[0, nt - 1 - t] = sb[...].astype(sb_out.dtype)
        kf = (k_f[rf, :].astype(F32) * kdf_ref[...]).astype(BF16)
        kb = (k_b[rb, :].astype(F32) * kdb_ref[...]).astype(BF16)
        for p in range(pairs):
            sl = slice(p * LANES, (p + 1) * LANES)
            new_f = jnp.where(same_head, _dot_tn(kf[:, sl], v_f[rf, sl]), 0.0)
            new_b = jnp.where(same_head, _dot_tn(kb[:, sl], v_b[rb, sl]), 0.0)
            sf[p] = sf[p] * gl_ref[:, sl] + new_f
            sb[p] = sb[p] * gl_ref[:, sl] + new_b


def _ret_states(k, v, kdf, kdb, gl, batch, seq):
    nt = STEP_CHUNKS
    l = RET_CHUNK * nt
    nb = seq // l
    pairs = RET_WIDTH // LANES
    fwd = pl.BlockSpec((l, RET_WIDTH), lambda b, c: (b * nb + c, 0))
    bwd = pl.BlockSpec((l, RET_WIDTH), lambda b, c: (b * nb + nb - 1 - c, 0))
    st = (batch, nb * nt, pairs, LANES, LANES)
    return pl.pallas_call(
        _ret_state_kernel,
        grid=(batch, nb),
        in_specs=[fwd, fwd, bwd, bwd, _resident(kdf.shape), _resident(kdb.shape), _resident(gl.shape)],
        out_specs=[pl.BlockSpec((1, nt, pairs, LANES, LANES), lambda b, c: (b, c, 0, 0, 0)),
                   pl.BlockSpec((1, nt, pairs, LANES, LANES), lambda b, c: (b, nb - 1 - c, 0, 0, 0))],
        out_shape=[jax.ShapeDtypeStruct(st, BF16)] * 2,
        scratch_shapes=[pltpu.VMEM((pairs, LANES, LANES), F32)] * 2,
        compiler_params=_cparams("parallel", "arbitrary"),
        name="ret_states",
    )(k, v, k, v, kdf, kdb, gl)


def _ret_out_kernel(q_ref, k_ref, v_ref, g_ref, sf_ref, sb_ref, dm_ref, qdf_ref, qdb_ref, ng_ref, o_ref):
    l = RET_CHUNK
    hd = RET_HEAD_DIM
    low = lax.broadcasted_iota(jnp.int32, (l, LANES), 1) < hd

    def block_diag(t):
        zero = jnp.zeros_like(t)
        return jnp.concatenate([jnp.where(low, t, zero), jnp.where(low, zero, t)], axis=0)

    for t in range(q_ref.shape[0] // l):
        rows = slice(t * l, (t + 1) * l)
        q = q_ref[rows, :]
        qf = q.astype(F32)
        q_lf = (qf * qdf_ref[...]).astype(BF16)
        q_lb = (qf * qdb_ref[...]).astype(BF16)
        ys = []
        for p in range(RET_WIDTH // LANES):
            sl = slice(p * LANES, (p + 1) * LANES)
            s = _dot_nt(q[:, sl], block_diag(k_ref[rows, sl]))
            y = _dot((s * dm_ref[p]).astype(BF16), block_diag(v_ref[rows, sl]))
            state = jnp.concatenate([sf_ref[0, t, p], sb_ref[0, t, p]], axis=0)
            y = y + _dot(jnp.concatenate([q_lf[:, sl], q_lb[:, sl]], axis=1), state)
            y2 = y * y
            ms_a = jnp.sum(jnp.where(low, y2, 0.0), axis=-1, keepdims=True) * (1.0 / hd)
            ms_b = jnp.sum(jnp.where(low, 0.0, y2), axis=-1, keepdims=True) * (1.0 / hd)
            ys.append(y * jnp.where(low, lax.rsqrt(ms_a + EPS), lax.rsqrt(ms_b + EPS)))
        y = jnp.concatenate(ys, axis=1) * ng_ref[...]
        o_ref[rows, :] = (_silu(g_ref[rows, :].astype(F32)) * y).astype(o_ref.dtype)


def _ret_out(q, k, v, g, sf, sb, dm, qdf, qdb, ng, batch, seq):
    n = q.shape[0]
    nt = STEP_CHUNKS
    l = RET_CHUNK * nt
    nc = seq // l
    pairs = RET_WIDTH // LANES
    row = pl.BlockSpec((l, RET_WIDTH), lambda b, c: (b * nc + c, 0))
    st = pl.BlockSpec((1, nt, pairs, LANES, LANES), lambda b, c: (b, c, 0, 0, 0))
    return pl.pallas_call(
        _ret_out_kernel,
        grid=(batch, nc),
        in_specs=[row, row, row, row, st, st,
                  _resident(dm.shape), _resident(qdf.shape), _resident(qdb.shape), _resident((1, RET_WIDTH))],
        out_specs=row,
        out_shape=jax.ShapeDtypeStruct((n, RET_WIDTH), BF16),
        compiler_params=_cparams("parallel", "parallel"),
        name="ret_out",
    )(q, k, v, g, sf, sb, dm, qdf, qdb, ng)


def _ret_tables(seq):
    l = RET_CHUNK
    hd = RET_HEAD_DIM
    pos = jnp.arange(seq, dtype=F32)
    inv_freq = ROPE_BASE ** (-jnp.arange(0, hd, 2, dtype=F32) / hd)
    ang = pos[:, None] * inv_freq[None, :]
    cos = jnp.tile(jnp.cos(ang), (1, LANES // (hd // 2)))
    sin = jnp.sin(ang)
    sin_s = jnp.tile(jnp.concatenate([-sin, sin], axis=1), (1, LANES // hd))
    log_gamma = jnp.log1p(-jnp.exp2(-5.0 - jnp.arange(RET_HEADS, dtype=F32)))
    idx = jnp.arange(l, dtype=F32)
    dm = jnp.exp(log_gamma[:, None, None] * jnp.abs(idx[:, None] - idx[None, :]))
    dm = jnp.concatenate([dm[0::2], dm[1::2]], axis=2)
    per_head = lambda t: jnp.repeat(t, hd, axis=1)
    kdf = per_head(jnp.exp(log_gamma[None, :] * (l - 1.0 - idx)[:, None]))
    kdb = per_head(jnp.exp(log_gamma[None, :] * idx[:, None]))
    qdf = per_head(jnp.exp(log_gamma[None, :] * (idx + 1.0)[:, None]))
    qdb = per_head(jnp.exp(log_gamma[None, :] * (l - idx)[:, None]))
    gl = per_head(jnp.exp(log_gamma * l)[None, :])
    return cos, sin_s, dm, kdf, kdb, qdf, qdb, gl


def _ret_branch(q, k, v, g, norm_g, tables, batch, seq):
    _, _, dm, kdf, kdb, qdf, qdb, gl = tables
    sf, sb = _ret_states(k, v, kdf, kdb, gl, batch, seq)
    return _ret_out(q, k, v, g, sf, sb, dm, qdf, qdb, norm_g.reshape(1, RET_WIDTH), batch, seq)


def _merge_kernel(x_ref, ya_ref, yb_ref, yc_ref, g_ref, wgate_ref, bgate_ref, wa_ref, wb_ref, wc_ref, wo_ref,
                  o_ref):
    x = x_ref[...]
    d = x.shape[1]
    h = _rms(x, g_ref[...]).astype(BF16)
    branches = (_dot(ya_ref[...], wa_ref[...]), _dot(yb_ref[...], wb_ref[...]), _dot(yc_ref[...], wc_ref[...]))
    mixed = jnp.zeros(x.shape, F32)
    for i, y in enumerate(branches):
        gate = jax.nn.sigmoid(_dot(h, wgate_ref[:, i * d:(i + 1) * d]) + bgate_ref[:, i * d:(i + 1) * d])
        mixed = mixed + gate * y
    o_ref[...] = x + _dot(mixed.astype(BF16), wo_ref[...])


def _merge(x, ya, yb, yc, g, wgate, bgate, wa, wb, wc, wo):
    n, d = x.shape
    tm = WIDE_TOKEN_TILE
    row = lambda c: pl.BlockSpec((tm, c), lambda i: (i, 0))
    consts = (g.reshape(1, d), wgate, bgate.reshape(1, -1), wa, wb, wc, wo)
    return pl.pallas_call(
        _merge_kernel,
        grid=(n // tm,),
        in_specs=[row(d), row(ya.shape[1]), row(yb.shape[1]), row(yc.shape[1])]
                 + [_resident(c.shape) for c in consts],
        out_specs=row(d),
        out_shape=jax.ShapeDtypeStruct((n, d), F32),
        compiler_params=_cparams("parallel"),
        name="merge",
    )(x, ya, yb, yc, *consts)


def _head_expander():
    return jnp.tile(jnp.repeat(jnp.eye(2 * SSD_HEADS, dtype=BF16), SSD_HEAD_DIM, axis=1), (3, 1))


def kernel(x, ffn1_norm, ffn1_w_gate, ffn1_w_up, ffn1_w_down, mix_norm, w_in, b_gate, ssd_conv_w, ssd_conv_b, ssd_dt_bias, ssd_a_log, ssd_d, ssd_norm, w_br_ssd, s5_lam_re, s5_lam_im, s5_log_step, s5_b_re, s5_b_im, s5_c_re, s5_c_im, s5_d, s5_glu_wv, s5_glu_wg, w_br_s5, ret_norm, w_br_ret, w_out, ffn2_norm, ffn2_w_gate, ffn2_w_up, ffn2_w_down, final_norm):
    batch, seq, d = x.shape
    depth = w_in.shape[0]
    n = batch * seq
    assert seq % CONV_TILE == 0 and seq % SSD_CHUNK == 0 and seq % RET_CHUNK == 0 and seq % S5_CHUNK == 0
    assert n % TOKEN_TILE == 0 and n % DT_TILE == 0 and n % (S5_ROWS * S5_CHUNK) == 0
    s5_steps = (seq // S5_CHUNK - 1).bit_length()
    assert 2 ** s5_steps == seq // S5_CHUNK
    bf = lambda w: w.astype(BF16)
    e = _head_expander()
    tables = _ret_tables(seq)
    offs = [0]
    for s in IN_PROJ_SIZES:
        offs.append(offs[-1] + s)
    xf = x.reshape(n, d).astype(F32)
    for i in range(depth):
        xf = _ffn(xf, ffn1_norm[i], bf(ffn1_w_gate[i]), bf(ffn1_w_up[i]), bf(ffn1_w_down[i]), final_norm, False)
        w = w_in[i]
        seg = [w[:, offs[j]:offs[j + 1]] for j in range(8)]
        seg[2] = jnp.pad(seg[2], ((0, 0), (0, LANES - seg[2].shape[1])))
        seg[3:4] = [seg[3][:, c:c + LANES] for c in range(0, S5_WIDTH, LANES)]
        z, xbc, dt_raw, u0, u1, u2, q, k, v, g = _inproj(
            xf, mix_norm[i], tables[0], tables[1], [bf(s) for s in seg],
            [BF16, BF16, F32, F32, F32, F32, BF16, BF16, BF16, BF16],
            (None,) * 6 + (1.0, RET_HEAD_DIM ** -0.5, None, None), seq)
        ya = _ssd_branch(z, xbc, dt_raw, ssd_conv_w[i].astype(F32), ssd_conv_b[i].astype(F32), ssd_dt_bias[i],
                         ssd_a_log[i], ssd_d[i], ssd_norm[i].astype(F32), e, batch, seq)
        mats = _s5_matrices(s5_lam_re[i], s5_lam_im[i], s5_log_step[i], s5_b_re[i], s5_b_im[i],
                            s5_c_re[i], s5_c_im[i], s5_d[i], s5_steps)
        yb = _s5_branch((u0, u1, u2), mats, bf(s5_glu_wv[i]), bf(s5_glu_wg[i]), batch)
        yc = _ret_branch(q, k, v, g, ret_norm[i].astype(F32), tables, batch, seq)
        xf = _merge(xf, ya, yb, yc, mix_norm[i], bf(w[:, offs[8]:]), b_gate[i].astype(F32), bf(w_br_ssd[i]),
                    bf(w_br_s5[i]), bf(w_br_ret[i]), bf(w_out[i]))
        xf = _ffn(xf, ffn2_norm[i], bf(ffn2_w_gate[i]), bf(ffn2_w_up[i]), bf(ffn2_w_down[i]), final_norm,
                  i == depth - 1)
    return xf.reshape(batch, seq, d).astype(x.dtype)
```

```python
import functools
import math

import jax
import jax.numpy as jnp
import numpy as np
from jax import lax
from jax.experimental import pallas as pl
from jax.experimental.pallas import tpu as pltpu

F32 = jnp.float32
BF16 = jnp.bfloat16

EPS = 1e-6
LOG2E = 1.4426950408889634
SSD_HEADS = 8
SSD_HEAD_DIM = 64
SSD_INNER = 512
SSD_GROUPS = 2
SSD_STATE = 64
SSD_CONV = 5
SSD_CONV_CH = 768
S5_GROUP_CH = 16
S5_GROUPS = 24
S5_WIDTH = 384
S5_STATE = 64
S5_MAX_RE = -1e-4
RET_HEADS = 8
RET_HEAD_DIM = 64
RET_WIDTH = 512
ROPE_BASE = 10000.0
N_BRANCH = 3
IN_PROJ_SIZES = (512, 768, 16, 384, 512, 512, 512, 512, 3072)

V7X_VMEM_BYTES = 64 * 1024 * 1024
VMEM_LIMIT = V7X_VMEM_BYTES - 8 * 1024 * 1024
LANES = 128

TOKEN_TILE = 512
WIDE_TOKEN_TILE = 1024
SSD_CHUNK = 128
RET_CHUNK = 128
STEP_CHUNKS = 8
S5_CHUNK = 32
S5_ROWS = 64
DT_TILE = 2048


def _cparams(*sem):
    return pltpu.CompilerParams(dimension_semantics=sem, vmem_limit_bytes=VMEM_LIMIT)


def _resident(shape):
    nd = len(shape)
    return pl.BlockSpec(shape, lambda *_: (0,) * nd, pipeline_mode=pl.Buffered(1))


def _rms(x, g):
    return (x * lax.rsqrt(jnp.mean(x * x, axis=-1, keepdims=True) + EPS)) * g


def _silu(x):
    return x * jax.nn.sigmoid(x)


def _dot(a, b):
    return jnp.dot(a, b, preferred_element_type=F32)


def _dot_nt(a, b):
    return lax.dot_general(a, b, (((1,), (1,)), ((), ())), preferred_element_type=F32)


def _dot_tn(a, b):
    return lax.dot_general(a, b, (((0,), (0,)), ((), ())), preferred_element_type=F32)


def _split3(v):
    hi = v.astype(BF16)
    r1 = v - hi.astype(F32)
    mid = r1.astype(BF16)
    lo = (r1 - mid.astype(F32)).astype(BF16)
    return hi, mid, lo


def _expand_heads(v, e3):
    return _dot(jnp.concatenate(_split3(v), axis=1), e3)


def _ffn_kernel(x_ref, g_ref, wg_ref, wu_ref, wd_ref, fg_ref, o_ref, *, final):
    x = x_ref[...]
    h = _rms(x, g_ref[...]).astype(BF16)
    a = _dot(h, wg_ref[...])
    b = _dot(h, wu_ref[...])
    t = (_silu(a) * b).astype(BF16)
    y = x + 0.5 * _dot(t, wd_ref[...])
    if final:
        y = _rms(y, fg_ref[...])
    o_ref[...] = y


def _ffn(x, g, wg, wu, wd, fg, final):
    n, d = x.shape
    f = wg.shape[1]
    tm = TOKEN_TILE
    row = pl.BlockSpec((tm, d), lambda i: (i, 0))
    return pl.pallas_call(
        functools.partial(_ffn_kernel, final=final),
        grid=(n // tm,),
        in_specs=[row, _resident((1, d)), _resident((d, f)), _resident((d, f)), _resident((f, d)),
                  _resident((1, d))],
        out_specs=row,
        out_shape=jax.ShapeDtypeStruct((n, d), F32),
        compiler_params=_cparams("parallel"),
        name="ffn",
    )(x, g.reshape(1, d), wg, wu, wd, fg.reshape(1, d))


HALO = 16


def _inproj_kernel(xp_ref, x_ref, xn_ref, g_ref, cos_ref, sin_ref, cw_ref, cb_ref, *refs,
                   rope_scale, conv_at, tiles_per_seq):
    nw = len(rope_scale)
    outs = list(refs[nw:])
    pending = []
    i = pl.program_id(0)
    t = x_ref.shape[0]
    he = _rms(jnp.concatenate([xp_ref[...], x_ref[...], xn_ref[...]], axis=0), g_ref[...]).astype(BF16)
    h = he[HALO:HALO + t]
    for j, (w_ref, scale) in enumerate(zip(refs[:nw], rope_scale)):
        if j == conv_at:
            ext = _dot(he, w_ref[...])
            rows = t + 2 * HALO
            r = lax.broadcasted_iota(jnp.int32, (rows, 1), 0)
            lo = jnp.where(i % tiles_per_seq == 0, HALO, 0)
            hi = jnp.where(i % tiles_per_seq == tiles_per_seq - 1, t + HALO, rows)
            ext = jnp.where(r >= lo, jnp.where(r < hi, ext, 0.0), 0.0)

            def conv_tile(o_ref, c_out, c_in, ext=ext, rows=rows):
                cols = slice(c_in, c_in + LANES)
                e = ext[:, cols]
                acc = jnp.broadcast_to(cb_ref[:, cols], (t, LANES))
                half = SSD_CONV // 2
                for k in range(SSD_CONV):
                    shifted = pltpu.roll(e, (half - k) % rows, 0)
                    acc = acc + cw_ref[k:k + 1, cols] * shifted[HALO:HALO + t]
                o_ref[:, c_out:c_out + LANES] = _silu(acc).astype(o_ref.dtype)

            c_in = 0
            for o_ref in (outs.pop(0), outs.pop(0), outs.pop(0)):
                for c_out in range(0, o_ref.shape[1], LANES):
                    pending.append(functools.partial(conv_tile, o_ref, c_out, c_in))
                    c_in += LANES
            continue
        if pending:
            pending.pop(0)()
        y = _dot(h, w_ref[...])
        if scale is not None:
            y = _rotary(y, cos_ref[...], sin_ref[...])
            if scale != 1.0:
                y = y * scale
        o_ref = outs.pop(0)
        o_ref[...] = y.astype(o_ref.dtype)
    for task in pending:
        task()


def _inproj(x, g, cos, sin_s, conv_w, conv_b, weights, out_widths, out_dtypes, rope_scale, conv_at, seq):
    n, d = x.shape
    tm = WIDE_TOKEN_TILE
    hb = tm // HALO
    nblk = n // HALO
    row = lambda c: pl.BlockSpec((tm, c), lambda i: (i, 0))
    pos = pl.BlockSpec((tm, LANES), lambda i: (i % (seq // tm), 0))
    prev = pl.BlockSpec((HALO, d), lambda i: (jnp.maximum(i * hb - 1, 0), 0))
    nxt = pl.BlockSpec((HALO, d), lambda i: (jnp.minimum((i + 1) * hb, nblk - 1), 0))
    return pl.pallas_call(
        functools.partial(_inproj_kernel, rope_scale=rope_scale, conv_at=conv_at, tiles_per_seq=seq // tm),
        grid=(n // tm,),
        in_specs=[prev, row(d), nxt, _resident((1, d)), pos, pos, _resident(conv_w.shape),
                  _resident((1, conv_b.shape[0]))] + [_resident(w.shape) for w in weights],
        out_specs=[row(c) for c in out_widths],
        out_shape=[jax.ShapeDtypeStruct((n, c), dt) for c, dt in zip(out_widths, out_dtypes)],
        compiler_params=_cparams("parallel"),
        name="inproj",
    )(x, x, x, g.reshape(1, d), cos, sin_s, conv_w, conv_b.reshape(1, -1), *weights)


def _ssd_dt_kernel(raw_ref, bias_ref, a_ref, dt_ref, cs_ref, dtc_ref, csc_ref, *, chunk):
    nh = 2 * SSD_HEADS
    x = raw_ref[...].T[:nh] + bias_ref[...]
    dt = jnp.maximum(x, 0.0) + jnp.log1p(jnp.exp(-jnp.abs(x)))
    cs = dt * a_ref[...]
    t = x.shape[1]
    pos = lax.broadcasted_iota(jnp.int32, x.shape, 1) % chunk
    fwd = lax.broadcasted_iota(jnp.int32, x.shape, 0) < SSD_HEADS
    k = 1
    while k < chunk:
        before = jnp.where(pos >= k, pltpu.roll(cs, k, 1), 0.0)
        after = jnp.where(pos < chunk - k, pltpu.roll(cs, t - k, 1), 0.0)
        cs = cs + jnp.where(fwd, before, after)
        k *= 2
    dt_ref[...] = dt
    cs_ref[...] = cs
    both = jnp.concatenate([dt, cs, jnp.zeros((LANES - 2 * nh, t), F32)], axis=0).T
    dtc_ref[...] = both[:, :nh]
    csc_ref[...] = both[:, nh:2 * nh]


def _ssd_dt(raw, bias, a):
    n = raw.shape[0]
    r = 2 * SSD_HEADS
    t = DT_TILE
    rows = pl.BlockSpec((r, t), lambda i: (0, i))
    cols = pl.BlockSpec((t, r), lambda i: (i, 0))
    return pl.pallas_call(
        functools.partial(_ssd_dt_kernel, chunk=SSD_CHUNK),
        grid=(n // t,),
        in_specs=[pl.BlockSpec((t, LANES), lambda i: (i, 0)), _resident((r, 1)), _resident((r, 1))],
        out_specs=[rows, rows, cols, cols],
        out_shape=[jax.ShapeDtypeStruct((r, n), F32)] * 2 + [jax.ShapeDtypeStruct((n, r), F32)] * 2,
        compiler_params=_cparams("parallel"),
        name="ssd_dt",
    )(raw, bias.reshape(r, 1), a.reshape(r, 1))


def _ssd_chunk_state(xs, bm, wexp):
    xw = (xs.astype(F32) * wexp).astype(BF16)
    gw = SSD_INNER // SSD_GROUPS
    parts = [_dot_tn(bm[:, g * SSD_STATE:(g + 1) * SSD_STATE], xw[:, g * gw:(g + 1) * gw])
             for g in range(SSD_GROUPS)]
    return jnp.concatenate(parts, axis=1)


def _ssd_state_kernel(xs_f, bm_f, dtc_f, csc_f, xs_b, bm_b, dtc_b, csc_b, e_ref,
                      sf_out, sb_out, sf, sb):
    c = pl.program_id(1)
    h = SSD_HEADS
    l = SSD_CHUNK
    nt = xs_f.shape[0] // l

    @pl.when(c == 0)
    def _():
        sf[...] = jnp.zeros_like(sf)
        sb[...] = jnp.zeros_like(sb)

    lane = lax.broadcasted_iota(jnp.int32, (1, 2 * h), 1)
    e_f = e_ref[:, :SSD_INNER]
    e_b = e_ref[:, SSD_INNER:]
    for t in range(nt):
        rf = slice(t * l, (t + 1) * l)
        rb = slice((nt - 1 - t) * l, (nt - t) * l)
        sf_out[0, t] = sf[...].astype(sf_out.dtype)
        sb_out[0, nt - 1 - t] = sb[...].astype(sb_out.dtype)
        cs_f = csc_f[rf, :]
        tot_f = cs_f[l - 1:l, :]
        w_f = jnp.exp(tot_f - cs_f) * dtc_f[rf, :]
        cs_b = csc_b[rb, :]
        tot_b = cs_b[0:1, :]
        w_b = jnp.exp(tot_b - cs_b) * dtc_b[rb, :]
        wexp_f = _expand_heads(jnp.where(lane < h, w_f, 0.0), e_f)
        wexp_b = _expand_heads(jnp.where(lane >= h, w_b, 0.0), e_b)
        dec_f = _expand_heads(jnp.broadcast_to(jnp.exp(tot_f), (8, 2 * h)), e_f)[0:1]
        dec_b = _expand_heads(jnp.broadcast_to(jnp.exp(tot_b), (8, 2 * h)), e_b)[0:1]
        sf[...] = sf[...] * dec_f + _ssd_chunk_state(xs_f[rf, :], bm_f[rf, :], wexp_f)
        sb[...] = sb[...] * dec_b + _ssd_chunk_state(xs_b[rb, :], bm_b[rb, :], wexp_b)


def _ssd_states(xs, bm, dt_c, cs_c, e, batch, seq):
    l = SSD_CHUNK
    nt = STEP_CHUNKS
    nb = seq // (l * nt)
    fwd = lambda w: pl.BlockSpec((nt * l, w), lambda b, c: (b * nb + c, 0))
    bwd = lambda w: pl.BlockSpec((nt * l, w), lambda b, c: (b * nb + nb - 1 - c, 0))
    st = (batch, nb * nt, SSD_STATE, SSD_INNER)
    return pl.pallas_call(
        _ssd_state_kernel,
        grid=(batch, nb),
        in_specs=[fwd(SSD_INNER), fwd(LANES), fwd(16), fwd(16),
                  bwd(SSD_INNER), bwd(LANES), bwd(16), bwd(16), _resident(e.shape)],
        out_specs=[pl.BlockSpec((1, nt, SSD_STATE, SSD_INNER), lambda b, c: (b, c, 0, 0)),
                   pl.BlockSpec((1, nt, SSD_STATE, SSD_INNER), lambda b, c: (b, nb - 1 - c, 0, 0))],
        out_shape=[jax.ShapeDtypeStruct(st, BF16)] * 2,
        scratch_shapes=[pltpu.VMEM((SSD_STATE, SSD_INNER), F32)] * 2,
        compiler_params=_cparams("parallel", "arbitrary"),
        name="ssd_states",
    )(xs, bm, dt_c, cs_c, xs, bm, dt_c, cs_c, e)


def _ssd_out_kernel(xs_ref, bm_ref, cm_ref, z_ref, dtc_ref, csc_ref, dtr_ref, csr_ref,
                    sf_ref, sb_ref, e_ref, dskip_ref, ng_ref, o_ref):
    h = SSD_HEADS
    l = SSD_CHUNK
    gw = SSD_INNER // SSD_GROUPS
    ri = lax.broadcasted_iota(jnp.int32, (l, l), 0)
    ci = lax.broadcasted_iota(jnp.int32, (l, l), 1)
    causal = ri >= ci
    diag = ri == ci
    low = lax.broadcasted_iota(jnp.int32, (l, LANES), 1) < LANES // 2

    def block_diag(t):
        zero = jnp.zeros_like(t)
        return jnp.concatenate([jnp.where(low, t, zero), jnp.where(low, zero, t)], axis=0)

    for t in range(xs_ref.shape[0] // l):
        rows = slice(t * l, (t + 1) * l)
        xs = xs_ref[rows, :]
        cm = cm_ref[rows, :]
        csc = csc_ref[rows, :]
        dtr = dtr_ref[:, rows]
        col = csc * LOG2E
        row = jnp.log2(dtr) - csr_ref[:, rows] * LOG2E
        scores = _dot_nt(cm, block_diag(bm_ref[rows, :]))
        ys = []
        for pair in range(h // 2):
            g = pair // (h // 2 // SSD_GROUPS)
            sc = scores[:, g * l:(g + 1) * l]
            ms = []
            for hh in (2 * pair, 2 * pair + 1):
                w = jnp.exp2(jnp.where(causal, col[:, hh:hh + 1] + row[hh:hh + 1, :],
                                       col[:, h + hh:h + hh + 1] + row[h + hh:h + hh + 1, :]))
                w = w + jnp.where(diag, dtr[h + hh:h + hh + 1, :], 0.0)
                ms.append((sc * w).astype(BF16))
            ys.append(_dot(jnp.concatenate(ms, axis=1), block_diag(xs[:, pair * LANES:(pair + 1) * LANES])))
        y = jnp.concatenate(ys, axis=1)
        eo = _expand_heads(jnp.exp(csc), e_ref[...])
        off_f = []
        off_b = []
        for g in range(SSD_GROUPS):
            cg = cm[:, g * SSD_STATE:(g + 1) * SSD_STATE]
            off_f.append(_dot(cg, sf_ref[0, t, :, g * gw:(g + 1) * gw]))
            off_b.append(_dot(cg, sb_ref[0, t, :, g * gw:(g + 1) * gw]))
        y = (y + jnp.concatenate(off_f, axis=1) * eo[:, :SSD_INNER]
             + jnp.concatenate(off_b, axis=1) * eo[:, SSD_INNER:])
        y = y + dskip_ref[...] * xs.astype(F32)
        y = y * _silu(z_ref[rows, :].astype(F32))
        o_ref[rows, :] = _rms(y, ng_ref[...]).astype(o_ref.dtype)


def _ssd_out(xs, bm, cm, z, dt_c, cs_c, dt_r, cs_r, sf, sb, e, dskip, ng, batch, seq):
    n = xs.shape[0]
    nt = STEP_CHUNKS
    l = SSD_CHUNK * nt
    nc = seq // l
    row = lambda w: pl.BlockSpec((l, w), lambda b, c: (b * nc + c, 0))
    col = pl.BlockSpec((16, l), lambda b, c: (0, b * nc + c))
    st = pl.BlockSpec((1, nt, SSD_STATE, SSD_INNER), lambda b, c: (b, c, 0, 0))
    return pl.pallas_call(
        _ssd_out_kernel,
        grid=(batch, nc),
        in_specs=[row(SSD_INNER), row(LANES), row(LANES), row(SSD_INNER), row(16), row(16), col, col,
                  st, st, _resident(e.shape), _resident((1, SSD_INNER)), _resident((1, SSD_INNER))],
        out_specs=row(SSD_INNER),
        out_shape=jax.ShapeDtypeStruct((n, SSD_INNER), BF16),
        compiler_params=_cparams("parallel", "parallel"),
        name="ssd_out",
    )(xs, bm, cm, z, dt_c, cs_c, dt_r, cs_r, sf, sb, e, dskip, ng)


def _ssd_branch(z, xs, bm, cm, dt_raw, dt_bias, a_log, d_skip, norm_g, e, batch, seq):
    a = -jnp.exp(a_log.astype(F32)).reshape(2 * SSD_HEADS)
    dt_r, cs_r, dt_c, cs_c = _ssd_dt(dt_raw, dt_bias.reshape(2 * SSD_HEADS), a)
    sf, sb = _ssd_states(xs, bm, dt_c, cs_c, e, batch, seq)
    dskip = jnp.repeat(d_skip.astype(F32), SSD_HEAD_DIM).reshape(1, SSD_INNER)
    return _ssd_out(xs, bm, cm, z, dt_c, cs_c, dt_r, cs_r, sf, sb, e, dskip,
                    norm_g.reshape(1, SSD_INNER), batch, seq)


def _s5_matrices(lam_re, lam_im, log_step, b_re, b_im, c_re, c_im, d_s5, nsteps):
    q = S5_CHUNK
    lr = jnp.minimum(lam_re.astype(F32), S5_MAX_RE)
    li = lam_im.astype(F32)
    step = jnp.exp(log_step.astype(F32))[..., None]
    tau = jnp.arange(q + 1, dtype=F32)[:, None, None, None]
    mag = jnp.exp(lr * step * tau)
    ang = li * step * tau
    pr = mag * jnp.cos(ang)
    pi = mag * jnp.sin(ang)
    den = lr * lr + li * li
    nr = pr[1] - 1.0
    coef_re = ((nr * lr + pi[1] * li) / den)[..., None]
    coef_im = ((pi[1] * lr - nr * li) / den)[..., None]
    br = b_re.astype(F32)
    bi = b_im.astype(F32)
    bb_re = coef_re * br - coef_im * bi
    bb_im = coef_re * bi + coef_im * br
    cr = c_re.astype(F32)
    ci = c_im.astype(F32)

    def to_state(p_r, p_i, d):
        p_r = p_r.transpose(1, 0, 2)[:, :, None, :]
        p_i = p_i.transpose(1, 0, 2)[:, :, None, :]
        b_r = bb_re[d].transpose(0, 2, 1)[:, None]
        b_i = bb_im[d].transpose(0, 2, 1)[:, None]
        return p_r * b_r - p_i * b_i, p_r * b_i + p_i * b_r

    sf_r, sf_i = to_state(pr[:q, 0][::-1], pi[:q, 0][::-1], 0)
    sb_r, sb_i = to_state(pr[:q, 1], pi[:q, 1], 1)
    bmat = jnp.concatenate([sf_r, sb_r, sf_i, sb_i], axis=-1)
    bmat = bmat.reshape(S5_GROUPS, q * S5_GROUP_CH, 4 * S5_STATE)

    def from_state(p_r, p_i, d, pad=(0, 0)):
        lanes_of_t = lambda p: jnp.repeat(jnp.pad(p.transpose(1, 2, 0), ((0, 0), (0, 0), pad)), S5_GROUP_CH, axis=2)
        p_r, p_i = lanes_of_t(p_r), lanes_of_t(p_i)
        reps = p_r.shape[2] // S5_GROUP_CH
        c_r = jnp.tile(cr[d].transpose(0, 2, 1), (1, 1, reps))
        c_i = jnp.tile(ci[d].transpose(0, 2, 1), (1, 1, reps))
        return c_r * p_r - c_i * p_i, c_r * p_i + c_i * p_r

    gf_r, gf_i = from_state(pr[:q, 0], pi[:q, 0], 0, (q - 1, 1))
    gb_r, gb_i = from_state(pr[:q, 1][::-1], pi[:q, 1][::-1], 1, (0, q))
    gen_c = jnp.concatenate([gf_r, gf_i, gb_r, gb_i], axis=1)
    tr = lambda t: t.transpose(0, 2, 1)
    gen_b = jnp.concatenate([tr(bb_re[0]), -tr(bb_im[0]), tr(bb_re[1]), -tr(bb_im[1])], axis=-1)
    d_gen = jnp.eye(S5_GROUP_CH, dtype=F32)[None] * d_s5.astype(F32)[:, None, :]
    d_gen = jnp.pad(d_gen[:, :, None, :], ((0, 0), (0, 0), (q - 1, q), (0, 0)))
    d_gen = d_gen.reshape(S5_GROUPS, S5_GROUP_CH, 2 * q * S5_GROUP_CH)

    of_r, of_i = from_state(pr[1:q + 1, 0], pi[1:q + 1, 0], 0)
    ob_r, ob_i = from_state(pr[1:q + 1, 1][::-1], pi[1:q + 1, 1][::-1], 1)
    cmat = jnp.concatenate([of_r, ob_r, -of_i, -ob_i], axis=1)
    m = (q * 2.0 ** jnp.arange(nsteps, dtype=F32))[:, None, None, None]
    amag = jnp.exp(lr * step * m)
    aang = li * step * m
    lanes = lambda t: jnp.concatenate([t[:, 0], t[:, 1]], axis=-1).transpose(1, 0, 2)
    return (gen_b, gen_c, d_gen, bmat.astype(BF16), cmat.astype(BF16),
            lanes(amag * jnp.cos(aang)), lanes(amag * jnp.sin(aang)))


def _s5_gather_kernel(*refs):
    u_refs, o_ref = refs[:-1], refs[-1]
    q = S5_CHUNK
    rb = o_ref.shape[1]
    per = LANES // S5_GROUP_CH
    slot = lax.broadcasted_iota(jnp.int32, (rb, LANES), 1) // S5_GROUP_CH
    for a in range(S5_GROUPS // per):
        for a2 in range(q // per):
            acc = [None] * per
            for sl in range(per):
                src = u_refs[a][pl.ds(a2 * per + sl, rb, stride=q), :]
                for gl in range(per):
                    piece = src if sl == gl else pltpu.roll(src, ((sl - gl) * S5_GROUP_CH) % LANES, 1)
                    acc[gl] = piece if sl == 0 else jnp.where(slot == sl, piece, acc[gl])
            for gl in range(per):
                o_ref[a * per + gl, :, a2 * LANES:(a2 + 1) * LANES] = acc[gl].astype(o_ref.dtype)


def _s5_gather(u_tiles):
    n = u_tiles[0].shape[0]
    q = S5_CHUNK
    rb = S5_ROWS
    w = q * S5_GROUP_CH
    return pl.pallas_call(
        _s5_gather_kernel,
        grid=(n // (rb * q),),
        in_specs=[pl.BlockSpec((rb * q, LANES), lambda i: (i, 0)) for _ in u_tiles],
        out_specs=pl.BlockSpec((S5_GROUPS, rb, w), lambda i: (0, i, 0)),
        out_shape=jax.ShapeDtypeStruct((S5_GROUPS, n // q, w), BF16),
        compiler_params=_cparams("parallel"),
        name="s5_gather",
    )(*u_tiles)


def _dot_f32(a, b):
    a0, a1, a2 = _split3(a)
    b0, b1, b2 = _split3(b)
    return ((_dot(a0, b0) + (_dot(a0, b1) + _dot(a1, b0)))
            + ((_dot(a0, b2) + _dot(a2, b0)) + _dot(a1, b1)))


def _s5_core_kernel(u_ref, gb_ref, gc_ref, gd_ref, b_ref, c_ref, apr_ref, api_ref, o_ref, toe, *, nc):
    q = S5_CHUNK
    w = q * S5_GROUP_CH
    r = u_ref.shape[1]
    p2 = 2 * S5_STATE
    kc = _dot_f32(gb_ref[0], gc_ref[0]) + gd_ref[0]
    for s in range(q):
        off = (q - 1 - s) * S5_GROUP_CH
        win = kc if off == 0 else pltpu.roll(kc, 2 * w - off, 1)
        toe[s * S5_GROUP_CH:(s + 1) * S5_GROUP_CH, :] = win[:, :w].astype(BF16)
    u = u_ref[0]
    loc = _dot(u, b_ref[0])
    xr, xi = loc[:, :p2], loc[:, p2:]
    row = lax.broadcasted_iota(jnp.int32, (r, p2), 0) % nc
    fwd = lax.broadcasted_iota(jnp.int32, (r, p2), 1) < S5_STATE

    def shift(v, k):
        dn = jnp.where(row >= k, pltpu.roll(v, k, 0), 0.0)
        up = jnp.where(row < nc - k, pltpu.roll(v, r - k, 0), 0.0)
        return jnp.where(fwd, dn, up)

    k, i = 1, 0
    while k < nc:
        sr, si = shift(xr, k), shift(xi, k)
        ar, ai = apr_ref[0, i:i + 1, :], api_ref[0, i:i + 1, :]
        xr, xi = xr + (ar * sr - ai * si), xi + (ar * si + ai * sr)
        k, i = 2 * k, i + 1
    h = jnp.concatenate([shift(xr, 1), shift(xi, 1)], axis=1).astype(BF16)
    y = _dot(u, toe[...]) + _dot(h, c_ref[0])
    o_ref[0] = jax.nn.gelu(y, approximate=True).astype(o_ref.dtype)


def _s5_core(u_g, gen_b, gen_c, d_gen, bmat, cmat, apr, api, batch):
    g, r, w = u_g.shape
    blk = lambda a: pl.BlockSpec((1,) + a.shape[1:], lambda i: (i, 0, 0))
    return pl.pallas_call(
        functools.partial(_s5_core_kernel, nc=r // batch),
        grid=(g,),
        in_specs=[blk(a) for a in (u_g, gen_b, gen_c, d_gen, bmat, cmat, apr, api)],
        out_specs=blk(u_g),
        out_shape=jax.ShapeDtypeStruct((g, r, w), BF16),
        scratch_shapes=[pltpu.VMEM((w, w), BF16)],
        compiler_params=_cparams("parallel"),
        name="s5_core",
    )(u_g, gen_b, gen_c, d_gen, bmat, cmat, apr, api)


def _s5_ungather_part(y_ref, nat_a, a, a2):
    q = S5_CHUNK
    rb = y_ref.shape[1]
    per = LANES // S5_GROUP_CH
    slot = lax.broadcasted_iota(jnp.int32, (rb, LANES), 1) // S5_GROUP_CH
    acc = [None] * per
    for gl in range(per):
        src = y_ref[a * per + gl, :, a2 * LANES:(a2 + 1) * LANES].astype(F32)
        for sl in range(per):
            piece = src if gl == sl else pltpu.roll(src, ((gl - sl) * S5_GROUP_CH) % LANES, 1)
            acc[sl] = piece if gl == 0 else jnp.where(slot == gl, piece, acc[sl])
    for sl in range(per):
        nat_a[pl.ds(a2 * per + sl, rb, stride=q), :] = acc[sl]


def _s5_branch(u_tiles, mats, batch):
    return _s5_core(_s5_gather(u_tiles), *mats, batch)


def _rotary(t, cos, sin_signed):
    w = t.shape[1]
    half = RET_HEAD_DIM // 2
    lane = lax.broadcasted_iota(jnp.int32, t.shape, 1) % RET_HEAD_DIM
    partner = jnp.where(lane < half, pltpu.roll(t, w - half, 1), pltpu.roll(t, half, 1))
    reps = w // cos.shape[1]
    return t * jnp.tile(cos, (1, reps)) + partner * jnp.tile(sin_signed, (1, reps))


def _ret_state_kernel(k_f, v_f, k_b, v_b, kdf_ref, kdb_ref, gl_ref, sf_out, sb_out, sf, sb):
    c = pl.program_id(1)
    l = RET_CHUNK
    nt = k_f.shape[0] // l
    pairs = RET_WIDTH // LANES

    @pl.when(c == 0)
    def _():
        sf[...] = jnp.zeros_like(sf)
        sb[...] = jnp.zeros_like(sb)

    same_head = (lax.broadcasted_iota(jnp.int32, (LANES, LANES), 0) // RET_HEAD_DIM
                 == lax.broadcasted_iota(jnp.int32, (LANES, LANES), 1) // RET_HEAD_DIM)
    for t in range(nt):
        rf = slice(t * l, (t + 1) * l)
        rb = slice((nt - 1 - t) * l, (nt - t) * l)
        sf_out[0, t] = sf[...].astype(sf_out.dtype)
        sb_out[0, nt - 1 - t] = sb[...].astype(sb_out.dtype)
        kf = (k_f[rf, :].astype(F32) * kdf_ref[...]).astype(BF16)
        kb = (k_b[rb, :].astype(F32) * kdb_ref[...]).astype(BF16)
        for p in range(pairs):
            sl = slice(p * LANES, (p + 1) * LANES)
            new_f = jnp.where(same_head, _dot_tn(kf[:, sl], v_f[rf, sl]), 0.0)
            new_b = jnp.where(same_head, _dot_tn(kb[:, sl], v_b[rb, sl]), 0.0)
            sf[p] = sf[p] * gl_ref[:, sl] + new_f
            sb[p] = sb[p] * gl_ref[:, sl] + new_b


def _ret_states(k, v, kdf, kdb, gl, batch, seq):
    nt = STEP_CHUNKS
    l = RET_CHUNK * nt
    nb = seq // l
    pairs = RET_WIDTH // LANES
    fwd = pl.BlockSpec((l, RET_WIDTH), lambda b, c: (b * nb + c, 0))
    bwd = pl.BlockSpec((l, RET_WIDTH), lambda b, c: (b * nb + nb - 1 - c, 0))
    st = (batch, nb * nt, pairs, LANES, LANES)
    return pl.pallas_call(
        _ret_state_kernel,
        grid=(batch, nb),
        in_specs=[fwd, fwd, bwd, bwd, _resident(kdf.shape), _resident(kdb.shape), _resident(gl.shape)],
        out_specs=[pl.BlockSpec((1, nt, pairs, LANES, LANES), lambda b, c: (b, c, 0, 0, 0)),
                   pl.BlockSpec((1, nt, pairs, LANES, LANES), lambda b, c: (b, nb - 1 - c, 0, 0, 0))],
        out_shape=[jax.ShapeDtypeStruct(st, BF16)] * 2,
        scratch_shapes=[pltpu.VMEM((pairs, LANES, LANES), F32)] * 2,
        compiler_params=_cparams("parallel", "arbitrary"),
        name="ret_states",
    )(k, v, k, v, kdf, kdb, gl)


def _ret_out_kernel(q_ref, k_ref, v_ref, g_ref, sf_ref, sb_ref, dm_ref, qdf_ref, qdb_ref, ng_ref, o_ref):
    l = RET_CHUNK
    hd = RET_HEAD_DIM
    low = lax.broadcasted_iota(jnp.int32, (l, LANES), 1) < hd

    def block_diag(t):
        zero = jnp.zeros_like(t)
        return jnp.concatenate([jnp.where(low, t, zero), jnp.where(low, zero, t)], axis=0)

    for t in range(q_ref.shape[0] // l):
        rows = slice(t * l, (t + 1) * l)
        q = q_ref[rows, :]
        qf = q.astype(F32)
        q_lf = (qf * qdf_ref[...]).astype(BF16)
        q_lb = (qf * qdb_ref[...]).astype(BF16)
        ys = []
        for p in range(RET_WIDTH // LANES):
            sl = slice(p * LANES, (p + 1) * LANES)
            s = _dot_nt(q[:, sl], block_diag(k_ref[rows, sl]))
            y = _dot((s * dm_ref[p]).astype(BF16), block_diag(v_ref[rows, sl]))
            state = jnp.concatenate([sf_ref[0, t, p], sb_ref[0, t, p]], axis=0)
            y = y + _dot(jnp.concatenate([q_lf[:, sl], q_lb[:, sl]], axis=1), state)
            y2 = y * y
            ms_a = jnp.sum(jnp.where(low, y2, 0.0), axis=-1, keepdims=True) * (1.0 / hd)
            ms_b = jnp.sum(jnp.where(low, 0.0, y2), axis=-1, keepdims=True) * (1.0 / hd)
            ys.append(y * jnp.where(low, lax.rsqrt(ms_a + EPS), lax.rsqrt(ms_b + EPS)))
        y = jnp.concatenate(ys, axis=1) * ng_ref[...]
        o_ref[rows, :] = (_silu(g_ref[rows, :].astype(F32)) * y).astype(o_ref.dtype)


def _ret_out(q, k, v, g, sf, sb, dm, qdf, qdb, ng, batch, seq):
    n = q.shape[0]
    nt = STEP_CHUNKS
    l = RET_CHUNK * nt
    nc = seq // l
    pairs = RET_WIDTH // LANES
    row = pl.BlockSpec((l, RET_WIDTH), lambda b, c: (b * nc + c, 0))
    st = pl.BlockSpec((1, nt, pairs, LANES, LANES), lambda b, c: (b, c, 0, 0, 0))
    return pl.pallas_call(
        _ret_out_kernel,
        grid=(batch, nc),
        in_specs=[row, row, row, row, st, st,
                  _resident(dm.shape), _resident(qdf.shape), _resident(qdb.shape), _resident((1, RET_WIDTH))],
        out_specs=row,
        out_shape=jax.ShapeDtypeStruct((n, RET_WIDTH), BF16),
        compiler_params=_cparams("parallel", "parallel"),
        name="ret_out",
    )(q, k, v, g, sf, sb, dm, qdf, qdb, ng)


def _ret_tables(seq):
    l = RET_CHUNK
    hd = RET_HEAD_DIM
    pos = np.arange(seq, dtype=np.float64)
    inv_freq = ROPE_BASE ** (-np.arange(0, hd, 2, dtype=np.float64) / hd)
    ang = pos[:, None] * inv_freq[None, :]
    cos = np.tile(np.cos(ang), (1, LANES // (hd // 2)))
    sin = np.sin(ang)
    sin_s = np.tile(np.concatenate([-sin, sin], axis=1), (1, LANES // hd))
    log_gamma = np.log1p(-np.exp2(-5.0 - np.arange(RET_HEADS, dtype=np.float64)))
    idx = np.arange(l, dtype=np.float64)
    dm = np.exp(log_gamma[:, None, None] * np.abs(idx[:, None] - idx[None, :]))
    dm = np.concatenate([dm[0::2], dm[1::2]], axis=2)
    per_head = lambda t: np.repeat(t, hd, axis=1)
    kdf = per_head(np.exp(log_gamma[None, :] * (l - 1.0 - idx)[:, None]))
    kdb = per_head(np.exp(log_gamma[None, :] * idx[:, None]))
    qdf = per_head(np.exp(log_gamma[None, :] * (idx + 1.0)[:, None]))
    qdb = per_head(np.exp(log_gamma[None, :] * (l - idx)[:, None]))
    gl = per_head(np.exp(log_gamma * l)[None, :])
    return tuple(jnp.asarray(t, F32) for t in (cos, sin_s, dm, kdf, kdb, qdf, qdb, gl))


def _ret_branch(q, k, v, g, norm_g, tables, batch, seq):
    _, _, dm, kdf, kdb, qdf, qdb, gl = tables
    sf, sb = _ret_states(k, v, kdf, kdb, gl, batch, seq)
    return _ret_out(q, k, v, g, sf, sb, dm, qdf, qdb, norm_g.reshape(1, RET_WIDTH), batch, seq)


def _merge_kernel(x_ref, ya_ref, yg_ref, yc_ref, g_ref, wgate_ref, bgate_ref, wa_ref, wv_ref, wgg_ref, wb_ref,
                  wc_ref, wo_ref, o_ref, *nat):
    x = x_ref[...]
    d = x.shape[1]
    h = _rms(x, g_ref[...]).astype(BF16)
    gates = []
    parts = S5_CHUNK * S5_GROUP_CH // LANES
    gw = d // parts
    for i in range(N_BRANCH):
        cols = []
        for a2 in range(parts):
            _s5_ungather_part(yg_ref, nat[i], i, a2)
            c0 = i * d + a2 * gw
            cols.append(jax.nn.sigmoid(_dot(h, wgate_ref[:, c0:c0 + gw]) + bgate_ref[:, c0:c0 + gw]))
        gates.append(jnp.concatenate(cols, axis=1))
    y5 = jnp.concatenate([t[...] for t in nat], axis=1).astype(BF16)
    yb = (_dot(y5, wv_ref[...]) * jax.nn.sigmoid(_dot(y5, wgg_ref[...]))).astype(BF16)
    branches = (_dot(ya_ref[...], wa_ref[...]), _dot(yb, wb_ref[...]), _dot(yc_ref[...], wc_ref[...]))
    mixed = gates[0] * branches[0] + gates[1] * branches[1] + gates[2] * branches[2]
    o_ref[...] = x + _dot(mixed.astype(BF16), wo_ref[...])


def _merge(x, ya, y_g, yc, g, wgate, bgate, wa, wv, wgg, wb, wc, wo):
    n, d = x.shape
    tm = WIDE_TOKEN_TILE
    row = lambda c: pl.BlockSpec((tm, c), lambda i: (i, 0))
    s5 = pl.BlockSpec((y_g.shape[0], tm // S5_CHUNK, y_g.shape[2]), lambda i: (0, i, 0))
    consts = (g.reshape(1, d), wgate, bgate.reshape(1, -1), wa, wv, wgg, wb, wc, wo)
    return pl.pallas_call(
        _merge_kernel,
        grid=(n // tm,),
        in_specs=[row(d), row(ya.shape[1]), s5, row(yc.shape[1])] + [_resident(c.shape) for c in consts],
        out_specs=row(d),
        out_shape=jax.ShapeDtypeStruct((n, d), F32),
        scratch_shapes=[pltpu.VMEM((tm, LANES), F32)] * (S5_WIDTH // LANES),
        compiler_params=_cparams("parallel"),
        name="merge",
    )(x, ya, y_g, yc, *consts)


def _head_expander():
    return jnp.asarray(np.tile(np.repeat(np.eye(2 * SSD_HEADS), SSD_HEAD_DIM, axis=1), (3, 1)), BF16)


def kernel(x, ffn1_norm, ffn1_w_gate, ffn1_w_up, ffn1_w_down, mix_norm, w_in, b_gate, ssd_conv_w, ssd_conv_b, ssd_dt_bias, ssd_a_log, ssd_d, ssd_norm, w_br_ssd, s5_lam_re, s5_lam_im, s5_log_step, s5_b_re, s5_b_im, s5_c_re, s5_c_im, s5_d, s5_glu_wv, s5_glu_wg, w_br_s5, ret_norm, w_br_ret, w_out, ffn2_norm, ffn2_w_gate, ffn2_w_up, ffn2_w_down, final_norm):
    batch, seq, d = x.shape
    depth = w_in.shape[0]
    n = batch * seq
    assert seq % WIDE_TOKEN_TILE == 0 and seq % (STEP_CHUNKS * SSD_CHUNK) == 0 and seq % (STEP_CHUNKS * RET_CHUNK) == 0
    assert n % TOKEN_TILE == 0 and n % DT_TILE == 0 and n % (S5_ROWS * S5_CHUNK) == 0
    s5_steps = (seq // S5_CHUNK - 1).bit_length()
    assert 2 ** s5_steps == seq // S5_CHUNK
    bf = lambda w: w.astype(BF16)
    e = _head_expander()
    tables = _ret_tables(seq)
    offs = [0]
    for s in IN_PROJ_SIZES:
        offs.append(offs[-1] + s)
    xf = x.reshape(n, d).astype(F32)
    for i in range(depth):
        xf = _ffn(xf, ffn1_norm[i], bf(ffn1_w_gate[i]), bf(ffn1_w_up[i]), bf(ffn1_w_down[i]), final_norm, False)
        w = w_in[i]
        seg = [w[:, offs[j]:offs[j + 1]] for j in range(8)]
        seg[2] = jnp.pad(seg[2], ((0, 0), (0, LANES - seg[2].shape[1])))
        seg[3:4] = [seg[3][:, c:c + LANES] for c in range(0, S5_WIDTH, LANES)]
        widths = [s.shape[1] for s in seg]
        widths[1:2] = [SSD_INNER, SSD_GROUPS * SSD_STATE, SSD_GROUPS * SSD_STATE]
        z, xs, bm, cm, dt_raw, u0, u1, u2, q, k, v, g = _inproj(
            xf, mix_norm[i], tables[0], tables[1], ssd_conv_w[i].astype(F32), ssd_conv_b[i].astype(F32),
            [bf(s) for s in seg], widths,
            [BF16, BF16, BF16, BF16, F32, F32, F32, F32, BF16, BF16, BF16, BF16],
            (None,) * 6 + (1.0, RET_HEAD_DIM ** -0.5, None, None), 1, seq)
        ya = _ssd_branch(z, xs, bm, cm, dt_raw, ssd_dt_bias[i], ssd_a_log[i], ssd_d[i],
                         ssd_norm[i].astype(F32), e, batch, seq)
        mats = _s5_matrices(s5_lam_re[i], s5_lam_im[i], s5_log_step[i], s5_b_re[i], s5_b_im[i],
                            s5_c_re[i], s5_c_im[i], s5_d[i], s5_steps)
        yb = _s5_branch((u0, u1, u2), mats, batch)
        yc = _ret_branch(q, k, v, g, ret_norm[i].astype(F32), tables, batch, seq)
        xf = _merge(xf, ya, yb, yc, mix_norm[i], bf(w[:, offs[8]:]), b_gate[i].astype(F32), bf(w_br_ssd[i]),
                    bf(s5_glu_wv[i]), bf(s5_glu_wg[i]), bf(w_br_s5[i]), bf(w_br_ret[i]), bf(w_out[i]))
        xf = _ffn(xf, ffn2_norm[i], bf(ffn2_w_gate[i]), bf(ffn2_w_up[i]), bf(ffn2_w_down[i]), final_norm,
                  i == depth - 1)
    return xf.reshape(batch, seq, d).astype(x.dtype)
```

```python
import functools
import math

import jax
import jax.numpy as jnp
import numpy as np
from jax import lax
from jax.experimental import pallas as pl
from jax.experimental.pallas import tpu as pltpu

F32 = jnp.float32
BF16 = jnp.bfloat16

EPS = 1e-6
LOG2E = 1.4426950408889634
SSD_HEADS = 8
SSD_HEAD_DIM = 64
SSD_INNER = 512
SSD_GROUPS = 2
SSD_STATE = 64
SSD_CONV = 5
SSD_CONV_CH = 768
S5_GROUP_CH = 16
S5_GROUPS = 24
S5_WIDTH = 384
S5_STATE = 64
S5_MAX_RE = -1e-4
RET_HEADS = 8
RET_HEAD_DIM = 64
RET_WIDTH = 512
ROPE_BASE = 10000.0
N_BRANCH = 3
IN_PROJ_SIZES = (512, 768, 16, 384, 512, 512, 512, 512, 3072)

V7X_VMEM_BYTES = 64 * 1024 * 1024
VMEM_LIMIT = V7X_VMEM_BYTES - 8 * 1024 * 1024
LANES = 128

TOKEN_TILE = 512
WIDE_TOKEN_TILE = 1024
SSD_CHUNK = 128
RET_CHUNK = 128
STEP_CHUNKS = 8
S5_CHUNK = 32
DT_TILE = 2048


def _cparams(*sem):
    return pltpu.CompilerParams(dimension_semantics=sem, vmem_limit_bytes=VMEM_LIMIT)


def _resident(shape):
    nd = len(shape)
    return pl.BlockSpec(shape, lambda *_: (0,) * nd, pipeline_mode=pl.Buffered(1))


def _resident_layer(stacked, layer):
    nd = stacked.ndim - 1
    return pl.BlockSpec((None,) + stacked.shape[1:], lambda *_: (layer,) + (0,) * nd,
                        pipeline_mode=pl.Buffered(1))


def _rms(x, g):
    return (x * lax.rsqrt(jnp.mean(x * x, axis=-1, keepdims=True) + EPS)) * g


def _silu(x):
    return x * jax.nn.sigmoid(x)


def _dot(a, b):
    return jnp.dot(a, b, preferred_element_type=F32)


def _dot_nt(a, b):
    return lax.dot_general(a, b, (((1,), (1,)), ((), ())), preferred_element_type=F32)


def _dot_tn(a, b):
    return lax.dot_general(a, b, (((0,), (0,)), ((), ())), preferred_element_type=F32)


def _split3(v):
    hi = v.astype(BF16)
    r1 = v - hi.astype(F32)
    mid = r1.astype(BF16)
    lo = (r1 - mid.astype(F32)).astype(BF16)
    return hi, mid, lo


def _expand_heads(v, e3):
    return _dot(jnp.concatenate(_split3(v), axis=1), e3)


def _ffn_kernel(x_ref, g_ref, wg_ref, wu_ref, wd_ref, fg_ref, o_ref, *, final):
    x = x_ref[...]
    h = _rms(x, g_ref[...]).astype(BF16)
    a = _dot(h, wg_ref[...])
    b = _dot(h, wu_ref[...])
    t = (_silu(a) * b).astype(BF16)
    y = x + 0.5 * _dot(t, wd_ref[...])
    if final:
        y = _rms(y, fg_ref[...])
    o_ref[...] = y


def _ffn(x, g, wg, wu, wd, fg, final, layer):
    n, d = x.shape
    tm = TOKEN_TILE
    row = pl.BlockSpec((tm, d), lambda i: (i, 0))
    return pl.pallas_call(
        functools.partial(_ffn_kernel, final=final),
        grid=(n // tm,),
        in_specs=[row, _resident((1, d)), _resident_layer(wg, layer), _resident_layer(wu, layer),
                  _resident_layer(wd, layer), _resident((1, d))],
        out_specs=row,
        out_shape=jax.ShapeDtypeStruct((n, d), F32),
        compiler_params=_cparams("parallel"),
        name="ffn",
    )(x, g.reshape(1, d), wg, wu, wd, fg.reshape(1, d))


HALO = 16


def _inproj_kernel(xp_ref, x_ref, xn_ref, g_ref, cos_ref, sin_ref, cw_ref, cb_ref, *refs,
                   rope_scale, conv_at, gather_at, tiles_per_seq):
    nw = len(rope_scale)
    u_tiles = refs[len(refs) - len(gather_at):]
    outs = list(refs[nw:len(refs) - len(gather_at)])
    pending = []
    i = pl.program_id(0)
    t = x_ref.shape[0]
    he = _rms(jnp.concatenate([xp_ref[...], x_ref[...], xn_ref[...]], axis=0), g_ref[...]).astype(BF16)
    h = he[HALO:HALO + t]
    for j, (w_ref, scale) in enumerate(zip(refs[:nw], rope_scale)):
        if j == conv_at:
            ext = _dot(he, w_ref[...])
            rows = t + 2 * HALO
            r = lax.broadcasted_iota(jnp.int32, (rows, 1), 0)
            lo = jnp.where(i % tiles_per_seq == 0, HALO, 0)
            hi = jnp.where(i % tiles_per_seq == tiles_per_seq - 1, t + HALO, rows)
            ext = jnp.where(r >= lo, jnp.where(r < hi, ext, 0.0), 0.0)

            def conv_tile(o_ref, c_out, c_in, ext=ext, rows=rows):
                cols = slice(c_in, c_in + LANES)
                e = ext[:, cols]
                acc = jnp.broadcast_to(cb_ref[:, cols], (t, LANES))
                half = SSD_CONV // 2
                for k in range(SSD_CONV):
                    shifted = pltpu.roll(e, (half - k) % rows, 0)
                    acc = acc + cw_ref[k:k + 1, cols] * shifted[HALO:HALO + t]
                o_ref[:, c_out:c_out + LANES] = _silu(acc).astype(o_ref.dtype)

            c_in = 0
            for o_ref in (outs.pop(0), outs.pop(0), outs.pop(0)):
                for c_out in range(0, o_ref.shape[1], LANES):
                    pending.append(functools.partial(conv_tile, o_ref, c_out, c_in))
                    c_in += LANES
            continue
        if pending:
            pending.pop(0)()
        y = _dot(h, w_ref[...])
        if scale is not None:
            y = _rotary(y, cos_ref[...], sin_ref[...])
            if scale != 1.0:
                y = y * scale
        if j in gather_at:
            u_tiles[gather_at.index(j)][...] = y
            if j == gather_at[-1]:
                _s5_gather_kernel(*u_tiles, outs.pop(0))
            continue
        o_ref = outs.pop(0)
        o_ref[...] = y.astype(o_ref.dtype)
    for task in pending:
        task()


def _inproj(x, g, cos, sin_s, conv_w, conv_b, weights, out_widths, out_dtypes, rope_scale, conv_at, gather_at,
            seq):
    n, d = x.shape
    tm = WIDE_TOKEN_TILE
    s5w = S5_CHUNK * S5_GROUP_CH
    s5_spec = pl.BlockSpec((S5_GROUPS, tm // S5_CHUNK, s5w), lambda i: (0, i, 0))
    hb = tm // HALO
    nblk = n // HALO
    row = lambda c: pl.BlockSpec((tm, c), lambda i: (i, 0))
    pos = pl.BlockSpec((tm, LANES), lambda i: (i % (seq // tm), 0))
    prev = pl.BlockSpec((HALO, d), lambda i: (jnp.maximum(i * hb - 1, 0), 0))
    nxt = pl.BlockSpec((HALO, d), lambda i: (jnp.minimum((i + 1) * hb, nblk - 1), 0))
    return pl.pallas_call(
        functools.partial(_inproj_kernel, rope_scale=rope_scale, conv_at=conv_at, gather_at=gather_at,
                          tiles_per_seq=seq // tm),
        grid=(n // tm,),
        in_specs=[prev, row(d), nxt, _resident((1, d)), pos, pos, _resident(conv_w.shape),
                  _resident((1, conv_b.shape[0]))] + [_resident(w.shape) for w in weights],
        out_specs=[s5_spec if c is None else row(c) for c in out_widths],
        out_shape=[jax.ShapeDtypeStruct((S5_GROUPS, n // S5_CHUNK, s5w) if c is None else (n, c), dt)
                   for c, dt in zip(out_widths, out_dtypes)],
        scratch_shapes=[pltpu.VMEM((tm, LANES), F32)] * len(gather_at),
        compiler_params=_cparams("parallel"),
        name="inproj",
    )(x, x, x, g.reshape(1, d), cos, sin_s, conv_w, conv_b.reshape(1, -1), *weights)


def _ssd_dt_kernel(raw_ref, bias_ref, a_ref, dt_ref, cs_ref, dtc_ref, csc_ref, *, chunk):
    nh = 2 * SSD_HEADS
    x = raw_ref[...].T[:nh] + bias_ref[...]
    dt = jnp.maximum(x, 0.0) + jnp.log1p(jnp.exp(-jnp.abs(x)))
    cs = dt * a_ref[...]
    t = x.shape[1]
    pos = lax.broadcasted_iota(jnp.int32, x.shape, 1) % chunk
    fwd = lax.broadcasted_iota(jnp.int32, x.shape, 0) < SSD_HEADS
    k = 1
    while k < chunk:
        before = jnp.where(pos >= k, pltpu.roll(cs, k, 1), 0.0)
        after = jnp.where(pos < chunk - k, pltpu.roll(cs, t - k, 1), 0.0)
        cs = cs + jnp.where(fwd, before, after)
        k *= 2
    dt_ref[...] = dt
    cs_ref[...] = cs
    both = jnp.concatenate([dt, cs, jnp.zeros((LANES - 2 * nh, t), F32)], axis=0).T
    dtc_ref[...] = both[:, :nh]
    csc_ref[...] = both[:, nh:2 * nh]


def _ssd_dt(raw, bias, a):
    n = raw.shape[0]
    r = 2 * SSD_HEADS
    t = DT_TILE
    rows = pl.BlockSpec((r, t), lambda i: (0, i))
    cols = pl.BlockSpec((t, r), lambda i: (i, 0))
    return pl.pallas_call(
        functools.partial(_ssd_dt_kernel, chunk=SSD_CHUNK),
        grid=(n // t,),
        in_specs=[pl.BlockSpec((t, LANES), lambda i: (i, 0)), _resident((r, 1)), _resident((r, 1))],
        out_specs=[rows, rows, cols, cols],
        out_shape=[jax.ShapeDtypeStruct((r, n), F32)] * 2 + [jax.ShapeDtypeStruct((n, r), F32)] * 2,
        compiler_params=_cparams("parallel"),
        name="ssd_dt",
    )(raw, bias.reshape(r, 1), a.reshape(r, 1))


def _ssd_chunk_state(xs, bm, wexp):
    xw = (xs.astype(F32) * wexp).astype(BF16)
    gw = SSD_INNER // SSD_GROUPS
    parts = [_dot_tn(bm[:, g * SSD_STATE:(g + 1) * SSD_STATE], xw[:, g * gw:(g + 1) * gw])
             for g in range(SSD_GROUPS)]
    return jnp.concatenate(parts, axis=1)


def _ssd_state_kernel(xs_f, bm_f, dtc_f, csc_f, xs_b, bm_b, dtc_b, csc_b, e_ref,
                      sf_out, sb_out, sf, sb):
    c = pl.program_id(1)
    h = SSD_HEADS
    l = SSD_CHUNK
    nt = xs_f.shape[0] // l

    @pl.when(c == 0)
    def _():
        sf[...] = jnp.zeros_like(sf)
        sb[...] = jnp.zeros_like(sb)

    lane = lax.broadcasted_iota(jnp.int32, (1, 2 * h), 1)
    e_f = e_ref[:, :SSD_INNER]
    e_b = e_ref[:, SSD_INNER:]
    for t in range(nt):
        rf = slice(t * l, (t + 1) * l)
        rb = slice((nt - 1 - t) * l, (nt - t) * l)
        sf_out[0, t] = sf[...].astype(sf_out.dtype)
        sb_out[0, nt - 1 - t] = sb[...].astype(sb_out.dtype)
        cs_f = csc_f[rf, :]
        tot_f = cs_f[l - 1:l, :]
        w_f = jnp.exp(tot_f - cs_f) * dtc_f[rf, :]
        cs_b = csc_b[rb, :]
        tot_b = cs_b[0:1, :]
        w_b = jnp.exp(tot_b - cs_b) * dtc_b[rb, :]
        wexp_f = _expand_heads(jnp.where(lane < h, w_f, 0.0), e_f)
        wexp_b = _expand_heads(jnp.where(lane >= h, w_b, 0.0), e_b)
        dec_f = _expand_heads(jnp.broadcast_to(jnp.exp(tot_f), (8, 2 * h)), e_f)[0:1]
        dec_b = _expand_heads(jnp.broadcast_to(jnp.exp(tot_b), (8, 2 * h)), e_b)[0:1]
        sf[...] = sf[...] * dec_f + _ssd_chunk_state(xs_f[rf, :], bm_f[rf, :], wexp_f)
        sb[...] = sb[...] * dec_b + _ssd_chunk_state(xs_b[rb, :], bm_b[rb, :], wexp_b)


def _ssd_states(xs, bm, dt_c, cs_c, e, batch, seq):
    l = SSD_CHUNK
    nt = STEP_CHUNKS
    nb = seq // (l * nt)
    fwd = lambda w: pl.BlockSpec((nt * l, w), lambda b, c: (b * nb + c, 0))
    bwd = lambda w: pl.BlockSpec((nt * l, w), lambda b, c: (b * nb + nb - 1 - c, 0))
    st = (batch, nb * nt, SSD_STATE, SSD_INNER)
    return pl.pallas_call(
        _ssd_state_kernel,
        grid=(batch, nb),
        in_specs=[fwd(SSD_INNER), fwd(LANES), fwd(16), fwd(16),
                  bwd(SSD_INNER), bwd(LANES), bwd(16), bwd(16), _resident(e.shape)],
        out_specs=[pl.BlockSpec((1, nt, SSD_STATE, SSD_INNER), lambda b, c: (b, c, 0, 0)),
                   pl.BlockSpec((1, nt, SSD_STATE, SSD_INNER), lambda b, c: (b, nb - 1 - c, 0, 0))],
        out_shape=[jax.ShapeDtypeStruct(st, BF16)] * 2,
        scratch_shapes=[pltpu.VMEM((SSD_STATE, SSD_INNER), F32)] * 2,
        compiler_params=_cparams("parallel", "arbitrary"),
        name="ssd_states",
    )(xs, bm, dt_c, cs_c, xs, bm, dt_c, cs_c, e)


def _ssd_out_kernel(xs_ref, bm_ref, cm_ref, z_ref, dtc_ref, csc_ref, dtr_ref, csr_ref,
                    sf_ref, sb_ref, e_ref, dskip_ref, ng_ref, o_ref):
    h = SSD_HEADS
    l = SSD_CHUNK
    gw = SSD_INNER // SSD_GROUPS
    ri = lax.broadcasted_iota(jnp.int32, (l, l), 0)
    ci = lax.broadcasted_iota(jnp.int32, (l, l), 1)
    causal = ri >= ci
    diag = ri == ci
    low = lax.broadcasted_iota(jnp.int32, (l, LANES), 1) < LANES // 2

    def block_diag(t):
        zero = jnp.zeros_like(t)
        return jnp.concatenate([jnp.where(low, t, zero), jnp.where(low, zero, t)], axis=0)

    for t in range(xs_ref.shape[0] // l):
        rows = slice(t * l, (t + 1) * l)
        xs = xs_ref[rows, :]
        cm = cm_ref[rows, :]
        csc = csc_ref[rows, :]
        dtr = dtr_ref[:, rows]
        col = csc * LOG2E
        row = jnp.log2(dtr) - csr_ref[:, rows] * LOG2E
        scores = _dot_nt(cm, block_diag(bm_ref[rows, :]))
        ys = []
        for pair in range(h // 2):
            g = pair // (h // 2 // SSD_GROUPS)
            sc = scores[:, g * l:(g + 1) * l]
            ms = []
            for hh in (2 * pair, 2 * pair + 1):
                w = jnp.exp2(jnp.where(causal, col[:, hh:hh + 1] + row[hh:hh + 1, :],
                                       col[:, h + hh:h + hh + 1] + row[h + hh:h + hh + 1, :]))
                w = w + jnp.where(diag, dtr[h + hh:h + hh + 1, :], 0.0)
                ms.append((sc * w).astype(BF16))
            ys.append(_dot(jnp.concatenate(ms, axis=1), block_diag(xs[:, pair * LANES:(pair + 1) * LANES])))
        y = jnp.concatenate(ys, axis=1)
        eo = _expand_heads(jnp.exp(csc), e_ref[...])
        off_f = []
        off_b = []
        for g in range(SSD_GROUPS):
            cg = cm[:, g * SSD_STATE:(g + 1) * SSD_STATE]
            off_f.append(_dot(cg, sf_ref[0, t, :, g * gw:(g + 1) * gw]))
            off_b.append(_dot(cg, sb_ref[0, t, :, g * gw:(g + 1) * gw]))
        y = (y + jnp.concatenate(off_f, axis=1) * eo[:, :SSD_INNER]
             + jnp.concatenate(off_b, axis=1) * eo[:, SSD_INNER:])
        y = y + dskip_ref[...] * xs.astype(F32)
        y = y * _silu(z_ref[rows, :].astype(F32))
        o_ref[rows, :] = _rms(y, ng_ref[...]).astype(o_ref.dtype)


def _ssd_out(xs, bm, cm, z, dt_c, cs_c, dt_r, cs_r, sf, sb, e, dskip, ng, batch, seq):
    n = xs.shape[0]
    nt = STEP_CHUNKS
    l = SSD_CHUNK * nt
    nc = seq // l
    row = lambda w: pl.BlockSpec((l, w), lambda b, c: (b * nc + c, 0))
    col = pl.BlockSpec((16, l), lambda b, c: (0, b * nc + c))
    st = pl.BlockSpec((1, nt, SSD_STATE, SSD_INNER), lambda b, c: (b, c, 0, 0))
    return pl.pallas_call(
        _ssd_out_kernel,
        grid=(batch, nc),
        in_specs=[row(SSD_INNER), row(LANES), row(LANES), row(SSD_INNER), row(16), row(16), col, col,
                  st, st, _resident(e.shape), _resident((1, SSD_INNER)), _resident((1, SSD_INNER))],
        out_specs=row(SSD_INNER),
        out_shape=jax.ShapeDtypeStruct((n, SSD_INNER), BF16),
        compiler_params=_cparams("parallel", "parallel"),
        name="ssd_out",
    )(xs, bm, cm, z, dt_c, cs_c, dt_r, cs_r, sf, sb, e, dskip, ng)


def _ssd_branch(z, xs, bm, cm, dt_raw, dt_bias, a_log, d_skip, norm_g, e, batch, seq):
    a = -jnp.exp(a_log.astype(F32)).reshape(2 * SSD_HEADS)
    dt_r, cs_r, dt_c, cs_c = _ssd_dt(dt_raw, dt_bias.reshape(2 * SSD_HEADS), a)
    sf, sb = _ssd_states(xs, bm, dt_c, cs_c, e, batch, seq)
    dskip = jnp.repeat(d_skip.astype(F32), SSD_HEAD_DIM).reshape(1, SSD_INNER)
    return _ssd_out(xs, bm, cm, z, dt_c, cs_c, dt_r, cs_r, sf, sb, e, dskip,
                    norm_g.reshape(1, SSD_INNER), batch, seq)


def _s5_matrices(lam_re, lam_im, log_step, b_re, b_im, c_re, c_im, d_s5, nsteps):
    q = S5_CHUNK
    lr = jnp.minimum(lam_re.astype(F32), S5_MAX_RE)
    li = lam_im.astype(F32)
    step = jnp.exp(log_step.astype(F32))[..., None]
    tau = jnp.arange(q + 1, dtype=F32)[:, None, None, None]
    mag = jnp.exp(lr * step * tau)
    ang = li * step * tau
    pr = mag * jnp.cos(ang)
    pi = mag * jnp.sin(ang)
    den = lr * lr + li * li
    nr = pr[1] - 1.0
    coef_re = ((nr * lr + pi[1] * li) / den)[..., None]
    coef_im = ((pi[1] * lr - nr * li) / den)[..., None]
    br = b_re.astype(F32)
    bi = b_im.astype(F32)
    bb_re = coef_re * br - coef_im * bi
    bb_im = coef_re * bi + coef_im * br
    cr = c_re.astype(F32)
    ci = c_im.astype(F32)

    def to_state(p_r, p_i, d):
        p_r = p_r.transpose(1, 0, 2)[:, :, None, :]
        p_i = p_i.transpose(1, 0, 2)[:, :, None, :]
        b_r = bb_re[d].transpose(0, 2, 1)[:, None]
        b_i = bb_im[d].transpose(0, 2, 1)[:, None]
        return p_r * b_r - p_i * b_i, p_r * b_i + p_i * b_r

    sf_r, sf_i = to_state(pr[:q, 0][::-1], pi[:q, 0][::-1], 0)
    sb_r, sb_i = to_state(pr[:q, 1], pi[:q, 1], 1)
    bmat = jnp.concatenate([sf_r, sb_r, sf_i, sb_i], axis=-1)
    bmat = bmat.reshape(S5_GROUPS, q * S5_GROUP_CH, 4 * S5_STATE)

    def from_state(p_r, p_i, d, pad=(0, 0)):
        lanes_of_t = lambda p: jnp.repeat(jnp.pad(p.transpose(1, 2, 0), ((0, 0), (0, 0), pad)), S5_GROUP_CH, axis=2)
        p_r, p_i = lanes_of_t(p_r), lanes_of_t(p_i)
        reps = p_r.shape[2] // S5_GROUP_CH
        c_r = jnp.tile(cr[d].transpose(0, 2, 1), (1, 1, reps))
        c_i = jnp.tile(ci[d].transpose(0, 2, 1), (1, 1, reps))
        return c_r * p_r - c_i * p_i, c_r * p_i + c_i * p_r

    gf_r, gf_i = from_state(pr[:q, 0], pi[:q, 0], 0, (q - 1, 1))
    gb_r, gb_i = from_state(pr[:q, 1][::-1], pi[:q, 1][::-1], 1, (0, q))
    gen_c = jnp.concatenate([gf_r, gf_i, gb_r, gb_i], axis=1)
    tr = lambda t: t.transpose(0, 2, 1)
    gen_b = jnp.concatenate([tr(bb_re[0]), -tr(bb_im[0]), tr(bb_re[1]), -tr(bb_im[1])], axis=-1)
    d_gen = jnp.eye(S5_GROUP_CH, dtype=F32)[None] * d_s5.astype(F32)[:, None, :]
    d_gen = jnp.pad(d_gen[:, :, None, :], ((0, 0), (0, 0), (q - 1, q), (0, 0)))
    d_gen = d_gen.reshape(S5_GROUPS, S5_GROUP_CH, 2 * q * S5_GROUP_CH)

    of_r, of_i = from_state(pr[1:q + 1, 0], pi[1:q + 1, 0], 0)
    ob_r, ob_i = from_state(pr[1:q + 1, 1][::-1], pi[1:q + 1, 1][::-1], 1)
    cmat = jnp.concatenate([of_r, ob_r, -of_i, -ob_i], axis=1)
    m = (q * 2.0 ** jnp.arange(nsteps, dtype=F32))[:, None, None, None]
    amag = jnp.exp(lr * step * m)
    aang = li * step * m
    lanes = lambda t: jnp.concatenate([t[:, 0], t[:, 1]], axis=-1).transpose(1, 0, 2)
    return (gen_b, gen_c, d_gen, bmat.astype(BF16), cmat.astype(BF16),
            lanes(amag * jnp.cos(aang)), lanes(amag * jnp.sin(aang)))


def _s5_gather_kernel(*refs):
    u_refs, o_ref = refs[:-1], refs[-1]
    q = S5_CHUNK
    rb = o_ref.shape[1]
    per = LANES // S5_GROUP_CH
    slot = lax.broadcasted_iota(jnp.int32, (rb, LANES), 1) // S5_GROUP_CH
    for a in range(S5_GROUPS // per):
        for a2 in range(q // per):
            acc = [None] * per
            for sl in range(per):
                src = u_refs[a][pl.ds(a2 * per + sl, rb, stride=q), :]
                for gl in range(per):
                    piece = src if sl == gl else pltpu.roll(src, ((sl - gl) * S5_GROUP_CH) % LANES, 1)
                    acc[gl] = piece if sl == 0 else jnp.where(slot == sl, piece, acc[gl])
            for gl in range(per):
                o_ref[a * per + gl, :, a2 * LANES:(a2 + 1) * LANES] = acc[gl].astype(o_ref.dtype)


def _dot_f32(a, b):
    a0, a1, a2 = _split3(a)
    b0, b1, b2 = _split3(b)
    return ((_dot(a0, b0) + (_dot(a0, b1) + _dot(a1, b0)))
            + ((_dot(a0, b2) + _dot(a2, b0)) + _dot(a1, b1)))


def _s5_core_kernel(u_ref, gb_ref, gc_ref, gd_ref, b_ref, c_ref, apr_ref, api_ref, o_ref, toe, *, nc):
    q = S5_CHUNK
    w = q * S5_GROUP_CH
    r = u_ref.shape[1]
    p2 = 2 * S5_STATE
    kc = _dot_f32(gb_ref[0], gc_ref[0]) + gd_ref[0]
    for s in range(q):
        off = (q - 1 - s) * S5_GROUP_CH
        win = kc if off == 0 else pltpu.roll(kc, 2 * w - off, 1)
        toe[s * S5_GROUP_CH:(s + 1) * S5_GROUP_CH, :] = win[:, :w].astype(BF16)
    u = u_ref[0]
    loc = _dot(u, b_ref[0])
    xr, xi = loc[:, :p2], loc[:, p2:]
    row = lax.broadcasted_iota(jnp.int32, (r, p2), 0) % nc
    fwd = lax.broadcasted_iota(jnp.int32, (r, p2), 1) < S5_STATE

    def shift(v, k):
        dn = jnp.where(row >= k, pltpu.roll(v, k, 0), 0.0)
        up = jnp.where(row < nc - k, pltpu.roll(v, r - k, 0), 0.0)
        return jnp.where(fwd, dn, up)

    k, i = 1, 0
    while k < nc:
        sr, si = shift(xr, k), shift(xi, k)
        ar, ai = apr_ref[0, i:i + 1, :], api_ref[0, i:i + 1, :]
        xr, xi = xr + (ar * sr - ai * si), xi + (ar * si + ai * sr)
        k, i = 2 * k, i + 1
    h = jnp.concatenate([shift(xr, 1), shift(xi, 1)], axis=1).astype(BF16)
    y = _dot(u, toe[...]) + _dot(h, c_ref[0])
    o_ref[0] = jax.nn.gelu(y, approximate=True).astype(o_ref.dtype)


def _s5_core(u_g, gen_b, gen_c, d_gen, bmat, cmat, apr, api, batch):
    g, r, w = u_g.shape
    blk = lambda a: pl.BlockSpec((1,) + a.shape[1:], lambda i: (i, 0, 0))
    return pl.pallas_call(
        functools.partial(_s5_core_kernel, nc=r // batch),
        grid=(g,),
        in_specs=[blk(a) for a in (u_g, gen_b, gen_c, d_gen, bmat, cmat, apr, api)],
        out_specs=blk(u_g),
        out_shape=jax.ShapeDtypeStruct((g, r, w), BF16),
        scratch_shapes=[pltpu.VMEM((w, w), BF16)],
        compiler_params=_cparams("parallel"),
        name="s5_core",
    )(u_g, gen_b, gen_c, d_gen, bmat, cmat, apr, api)


def _s5_ungather_part(y_ref, nat_a, a, a2):
    q = S5_CHUNK
    rb = y_ref.shape[1]
    per = LANES // S5_GROUP_CH
    slot = lax.broadcasted_iota(jnp.int32, (rb, LANES), 1) // S5_GROUP_CH
    acc = [None] * per
    for gl in range(per):
        src = y_ref[a * per + gl, :, a2 * LANES:(a2 + 1) * LANES].astype(F32)
        for sl in range(per):
            piece = src if gl == sl else pltpu.roll(src, ((gl - sl) * S5_GROUP_CH) % LANES, 1)
            acc[sl] = piece if gl == 0 else jnp.where(slot == gl, piece, acc[sl])
    for sl in range(per):
        nat_a[pl.ds(a2 * per + sl, rb, stride=q), :] = acc[sl]


def _rotary(t, cos, sin_signed):
    w = t.shape[1]
    half = RET_HEAD_DIM // 2
    lane = lax.broadcasted_iota(jnp.int32, t.shape, 1) % RET_HEAD_DIM
    partner = jnp.where(lane < half, pltpu.roll(t, w - half, 1), pltpu.roll(t, half, 1))
    reps = w // cos.shape[1]
    return t * jnp.tile(cos, (1, reps)) + partner * jnp.tile(sin_signed, (1, reps))


def _ret_state_kernel(k_f, v_f, k_b, v_b, kdf_ref, kdb_ref, gl_ref, sf_out, sb_out, sf, sb):
    c = pl.program_id(1)
    l = RET_CHUNK
    nt = k_f.shape[0] // l
    pairs = RET_WIDTH // LANES

    @pl.when(c == 0)
    def _():
        sf[...] = jnp.zeros_like(sf)
        sb[...] = jnp.zeros_like(sb)

    same_head = (lax.broadcasted_iota(jnp.int32, (LANES, LANES), 0) // RET_HEAD_DIM
                 == lax.broadcasted_iota(jnp.int32, (LANES, LANES), 1) // RET_HEAD_DIM)
    for t in range(nt):
        rf = slice(t * l, (t + 1) * l)
        rb = slice((nt - 1 - t) * l, (nt - t) * l)
        sf_out[0, t] = sf[...].astype(sf_out.dtype)
        sb_out[0, nt - 1 - t] = sb[...].astype(sb_out.dtype)
        kf = (k_f[rf, :].astype(F32) * kdf_ref[...]).astype(BF16)
        kb = (k_b[rb, :].astype(F32) * kdb_ref[...]).astype(BF16)
        for p in range(pairs):
            sl = slice(p * LANES, (p + 1) * LANES)
            new_f = jnp.where(same_head, _dot_tn(kf[:, sl], v_f[rf, sl]), 0.0)
            new_b = jnp.where(same_head, _dot_tn(kb[:, sl], v_b[rb, sl]), 0.0)
            sf[p] = sf[p] * gl_ref[:, sl] + new_f
            sb[p] = sb[p] * gl_ref[:, sl] + new_b


def _ret_states(k, v, kdf, kdb, gl, batch, seq):
    nt = STEP_CHUNKS
    l = RET_CHUNK * nt
    nb = seq // l
    pairs = RET_WIDTH // LANES
    fwd = pl.BlockSpec((l, RET_WIDTH), lambda b, c: (b * nb + c, 0))
    bwd = pl.BlockSpec((l, RET_WIDTH), lambda b, c: (b * nb + nb - 1 - c, 0))
    st = (batch, nb * nt, pairs, LANES, LANES)
    return pl.pallas_call(
        _ret_state_kernel,
        grid=(batch, nb),
        in_specs=[fwd, fwd, bwd, bwd, _resident(kdf.shape), _resident(kdb.shape), _resident(gl.shape)],
        out_specs=[pl.BlockSpec((1, nt, pairs, LANES, LANES), lambda b, c: (b, c, 0, 0, 0)),
                   pl.BlockSpec((1, nt, pairs, LANES, LANES), lambda b, c: (b, nb - 1 - c, 0, 0, 0))],
        out_shape=[jax.ShapeDtypeStruct(st, BF16)] * 2,
        scratch_shapes=[pltpu.VMEM((pairs, LANES, LANES), F32)] * 2,
        compiler_params=_cparams("parallel", "arbitrary"),
        name="ret_states",
    )(k, v, k, v, kdf, kdb, gl)


def _ret_out_kernel(q_ref, k_ref, v_ref, g_ref, sf_ref, sb_ref, dm_ref, qdf_ref, qdb_ref, ng_ref, o_ref):
    l = RET_CHUNK
    hd = RET_HEAD_DIM
    low = lax.broadcasted_iota(jnp.int32, (l, LANES), 1) < hd

    def block_diag(t):
        zero = jnp.zeros_like(t)
        return jnp.concatenate([jnp.where(low, t, zero), jnp.where(low, zero, t)], axis=0)

    for t in range(q_ref.shape[0] // l):
        rows = slice(t * l, (t + 1) * l)
        q = q_ref[rows, :]
        qf = q.astype(F32)
        q_lf = (qf * qdf_ref[...]).astype(BF16)
        q_lb = (qf * qdb_ref[...]).astype(BF16)
        ys = []
        for p in range(RET_WIDTH // LANES):
            sl = slice(p * LANES, (p + 1) * LANES)
            s = _dot_nt(q[:, sl], block_diag(k_ref[rows, sl]))
            y = _dot((s * dm_ref[p]).astype(BF16), block_diag(v_ref[rows, sl]))
            state = jnp.concatenate([sf_ref[0, t, p], sb_ref[0, t, p]], axis=0)
            y = y + _dot(jnp.concatenate([q_lf[:, sl], q_lb[:, sl]], axis=1), state)
            y2 = y * y
            ms_a = jnp.sum(jnp.where(low, y2, 0.0), axis=-1, keepdims=True) * (1.0 / hd)
            ms_b = jnp.sum(jnp.where(low, 0.0, y2), axis=-1, keepdims=True) * (1.0 / hd)
            ys.append(y * jnp.where(low, lax.rsqrt(ms_a + EPS), lax.rsqrt(ms_b + EPS)))
        y = jnp.concatenate(ys, axis=1) * ng_ref[...]
        o_ref[rows, :] = (_silu(g_ref[rows, :].astype(F32)) * y).astype(o_ref.dtype)


def _ret_out(q, k, v, g, sf, sb, dm, qdf, qdb, ng, batch, seq):
    n = q.shape[0]
    nt = STEP_CHUNKS
    l = RET_CHUNK * nt
    nc = seq // l
    pairs = RET_WIDTH // LANES
    row = pl.BlockSpec((l, RET_WIDTH), lambda b, c: (b * nc + c, 0))
    st = pl.BlockSpec((1, nt, pairs, LANES, LANES), lambda b, c: (b, c, 0, 0, 0))
    return pl.pallas_call(
        _ret_out_kernel,
        grid=(batch, nc),
        in_specs=[row, row, row, row, st, st,
                  _resident(dm.shape), _resident(qdf.shape), _resident(qdb.shape), _resident((1, RET_WIDTH))],
        out_specs=row,
        out_shape=jax.ShapeDtypeStruct((n, RET_WIDTH), BF16),
        compiler_params=_cparams("parallel", "parallel"),
        name="ret_out",
    )(q, k, v, g, sf, sb, dm, qdf, qdb, ng)


def _ret_tables(seq):
    l = RET_CHUNK
    hd = RET_HEAD_DIM
    pos = np.arange(seq, dtype=np.float64)
    inv_freq = ROPE_BASE ** (-np.arange(0, hd, 2, dtype=np.float64) / hd)
    ang = pos[:, None] * inv_freq[None, :]
    cos = np.tile(np.cos(ang), (1, LANES // (hd // 2)))
    sin = np.sin(ang)
    sin_s = np.tile(np.concatenate([-sin, sin], axis=1), (1, LANES // hd))
    log_gamma = np.log1p(-np.exp2(-5.0 - np.arange(RET_HEADS, dtype=np.float64)))
    idx = np.arange(l, dtype=np.float64)
    dm = np.exp(log_gamma[:, None, None] * np.abs(idx[:, None] - idx[None, :]))
    dm = np.concatenate([dm[0::2], dm[1::2]], axis=2)
    per_head = lambda t: np.repeat(t, hd, axis=1)
    kdf = per_head(np.exp(log_gamma[None, :] * (l - 1.0 - idx)[:, None]))
    kdb = per_head(np.exp(log_gamma[None, :] * idx[:, None]))
    qdf = per_head(np.exp(log_gamma[None, :] * (idx + 1.0)[:, None]))
    qdb = per_head(np.exp(log_gamma[None, :] * (l - idx)[:, None]))
    gl = per_head(np.exp(log_gamma * l)[None, :])
    return tuple(jnp.asarray(t, F32) for t in (cos, sin_s, dm, kdf, kdb, qdf, qdb, gl))


def _ret_branch(q, k, v, g, norm_g, tables, batch, seq):
    _, _, dm, kdf, kdb, qdf, qdb, gl = tables
    sf, sb = _ret_states(k, v, kdf, kdb, gl, batch, seq)
    return _ret_out(q, k, v, g, sf, sb, dm, qdf, qdb, norm_g.reshape(1, RET_WIDTH), batch, seq)


def _merge_kernel(x_ref, ya_ref, yg_ref, yc_ref, g_ref, wgate_ref, bgate_ref, wa_ref, wv_ref, wgg_ref, wb_ref,
                  wc_ref, wo_ref, o_ref, *nat):
    x = x_ref[...]
    d = x.shape[1]
    h = _rms(x, g_ref[...]).astype(BF16)
    gates = []
    parts = S5_CHUNK * S5_GROUP_CH // LANES
    gw = d // parts
    for i in range(N_BRANCH):
        cols = []
        for a2 in range(parts):
            _s5_ungather_part(yg_ref, nat[i], i, a2)
            c0 = i * d + a2 * gw
            cols.append(jax.nn.sigmoid(_dot(h, wgate_ref[:, c0:c0 + gw]) + bgate_ref[:, c0:c0 + gw]))
        gates.append(jnp.concatenate(cols, axis=1))
    y5 = jnp.concatenate([t[...] for t in nat], axis=1).astype(BF16)
    yb = (_dot(y5, wv_ref[...]) * jax.nn.sigmoid(_dot(y5, wgg_ref[...]))).astype(BF16)
    branches = (_dot(ya_ref[...], wa_ref[...]), _dot(yb, wb_ref[...]), _dot(yc_ref[...], wc_ref[...]))
    mixed = gates[0] * branches[0] + gates[1] * branches[1] + gates[2] * branches[2]
    o_ref[...] = x + _dot(mixed.astype(BF16), wo_ref[...])


def _merge(x, ya, y_g, yc, g, wgate, bgate, stacked, layer):
    n, d = x.shape
    tm = WIDE_TOKEN_TILE
    row = lambda c: pl.BlockSpec((tm, c), lambda i: (i, 0))
    s5 = pl.BlockSpec((y_g.shape[0], tm // S5_CHUNK, y_g.shape[2]), lambda i: (0, i, 0))
    consts = (g.reshape(1, d), wgate, bgate.reshape(1, -1)) + tuple(stacked)
    return pl.pallas_call(
        _merge_kernel,
        grid=(n // tm,),
        in_specs=[row(d), row(ya.shape[1]), s5, row(yc.shape[1])] + [_resident(c.shape) for c in consts[:3]]
                 + [_resident_layer(w, layer) for w in stacked],
        out_specs=row(d),
        out_shape=jax.ShapeDtypeStruct((n, d), F32),
        scratch_shapes=[pltpu.VMEM((tm, LANES), F32)] * (S5_WIDTH // LANES),
        compiler_params=_cparams("parallel"),
        name="merge",
    )(x, ya, y_g, yc, *consts)


def _head_expander():
    return jnp.asarray(np.tile(np.repeat(np.eye(2 * SSD_HEADS), SSD_HEAD_DIM, axis=1), (3, 1)), BF16)


def kernel(x, ffn1_norm, ffn1_w_gate, ffn1_w_up, ffn1_w_down, mix_norm, w_in, b_gate, ssd_conv_w, ssd_conv_b, ssd_dt_bias, ssd_a_log, ssd_d, ssd_norm, w_br_ssd, s5_lam_re, s5_lam_im, s5_log_step, s5_b_re, s5_b_im, s5_c_re, s5_c_im, s5_d, s5_glu_wv, s5_glu_wg, w_br_s5, ret_norm, w_br_ret, w_out, ffn2_norm, ffn2_w_gate, ffn2_w_up, ffn2_w_down, final_norm):
    batch, seq, d = x.shape
    depth = w_in.shape[0]
    n = batch * seq
    assert seq % WIDE_TOKEN_TILE == 0 and seq % (STEP_CHUNKS * SSD_CHUNK) == 0 and seq % (STEP_CHUNKS * RET_CHUNK) == 0
    assert n % TOKEN_TILE == 0 and n % DT_TILE == 0
    s5_steps = (seq // S5_CHUNK - 1).bit_length()
    assert 2 ** s5_steps == seq // S5_CHUNK
    bf = lambda w: w.astype(BF16)
    e = _head_expander()
    tables = _ret_tables(seq)
    offs = [0]
    for s in IN_PROJ_SIZES:
        offs.append(offs[-1] + s)
    xf = x.reshape(n, d).astype(F32)
    ffn1 = (bf(ffn1_w_gate), bf(ffn1_w_up), bf(ffn1_w_down))
    ffn2 = (bf(ffn2_w_gate), bf(ffn2_w_up), bf(ffn2_w_down))
    merge_w = (bf(w_br_ssd), bf(s5_glu_wv), bf(s5_glu_wg), bf(w_br_s5), bf(w_br_ret), bf(w_out))
    for i in range(depth):
        xf = _ffn(xf, ffn1_norm[i], *ffn1, final_norm, False, i)
        w = w_in[i]
        seg = [w[:, offs[j]:offs[j + 1]] for j in range(8)]
        seg[2] = jnp.pad(seg[2], ((0, 0), (0, LANES - seg[2].shape[1])))
        seg[3:4] = [seg[3][:, c:c + LANES] for c in range(0, S5_WIDTH, LANES)]
        widths = [SSD_INNER, SSD_INNER, SSD_GROUPS * SSD_STATE, SSD_GROUPS * SSD_STATE, LANES, None,
                  RET_WIDTH, RET_WIDTH, RET_WIDTH, RET_WIDTH]
        z, xs, bm, cm, dt_raw, u_g, q, k, v, g = _inproj(
            xf, mix_norm[i], tables[0], tables[1], ssd_conv_w[i].astype(F32), ssd_conv_b[i].astype(F32),
            [bf(s) for s in seg], widths, [BF16, BF16, BF16, BF16, F32, BF16, BF16, BF16, BF16, BF16],
            (None,) * 6 + (1.0, RET_HEAD_DIM ** -0.5, None, None), 1, (3, 4, 5), seq)
        ya = _ssd_branch(z, xs, bm, cm, dt_raw, ssd_dt_bias[i], ssd_a_log[i], ssd_d[i],
                         ssd_norm[i].astype(F32), e, batch, seq)
        mats = _s5_matrices(s5_lam_re[i], s5_lam_im[i], s5_log_step[i], s5_b_re[i], s5_b_im[i],
                            s5_c_re[i], s5_c_im[i], s5_d[i], s5_steps)
        yb = _s5_core(u_g, *mats, batch)
        yc = _ret_branch(q, k, v, g, ret_norm[i].astype(F32), tables, batch, seq)
        xf = _merge(xf, ya, yb, yc, mix_norm[i], bf(w[:, offs[8]:]), b_gate[i].astype(F32), merge_w, i)
        xf = _ffn(xf, ffn2_norm[i], *ffn2, final_norm, i == depth - 1, i)
    return xf.reshape(batch, seq, d).astype(x.dtype)
```

```python
import functools
import math

import jax
import jax.numpy as jnp
import numpy as np
from jax import lax
from jax.experimental import pallas as pl
from jax.experimental.pallas import tpu as pltpu

F32 = jnp.float32
BF16 = jnp.bfloat16

EPS = 1e-6
LOG2E = 1.4426950408889634
SSD_HEADS = 8
SSD_HEAD_DIM = 64
SSD_INNER = 512
SSD_GROUPS = 2
SSD_STATE = 64
SSD_CONV = 5
SSD_CONV_CH = 768
S5_GROUP_CH = 16
S5_GROUPS = 24
S5_WIDTH = 384
S5_STATE = 64
S5_MAX_RE = -1e-4
RET_HEADS = 8
RET_HEAD_DIM = 64
RET_WIDTH = 512
ROPE_BASE = 10000.0
N_BRANCH = 3
IN_PROJ_SIZES = (512, 768, 16, 384, 512, 512, 512, 512, 3072)

V7X_VMEM_BYTES = 64 * 1024 * 1024
VMEM_LIMIT = V7X_VMEM_BYTES - 8 * 1024 * 1024
LANES = 128

TOKEN_TILE = 512
WIDE_TOKEN_TILE = 1024
SSD_CHUNK = 128
RET_CHUNK = 128
STEP_CHUNKS = 8
S5_CHUNK = 32


def _cparams(*sem):
    return pltpu.CompilerParams(dimension_semantics=sem, vmem_limit_bytes=VMEM_LIMIT)


def _resident(shape):
    nd = len(shape)
    return pl.BlockSpec(shape, lambda *_: (0,) * nd, pipeline_mode=pl.Buffered(1))


def _resident_layer(stacked, layer):
    nd = stacked.ndim - 1
    return pl.BlockSpec((None,) + stacked.shape[1:], lambda *_: (layer,) + (0,) * nd,
                        pipeline_mode=pl.Buffered(1))


def _rms(x, g):
    return (x * lax.rsqrt(jnp.mean(x * x, axis=-1, keepdims=True) + EPS)) * g


def _silu(x):
    return x * jax.nn.sigmoid(x)


def _dot(a, b):
    return jnp.dot(a, b, preferred_element_type=F32)


def _dot_nt(a, b):
    return lax.dot_general(a, b, (((1,), (1,)), ((), ())), preferred_element_type=F32)


def _dot_tn(a, b):
    return lax.dot_general(a, b, (((0,), (0,)), ((), ())), preferred_element_type=F32)


def _split3(v):
    hi = v.astype(BF16)
    r1 = v - hi.astype(F32)
    mid = r1.astype(BF16)
    lo = (r1 - mid.astype(F32)).astype(BF16)
    return hi, mid, lo


def _expand_heads(v, e3):
    return _dot(jnp.concatenate(_split3(v), axis=1), e3)


def _ffn_kernel(x_ref, g_ref, wg_ref, wu_ref, wd_ref, fg_ref, o_ref, *, final):
    x = x_ref[...]
    h = _rms(x, g_ref[...]).astype(BF16)
    a = _dot(h, wg_ref[...])
    b = _dot(h, wu_ref[...])
    t = (_silu(a) * b).astype(BF16)
    y = x + 0.5 * _dot(t, wd_ref[...])
    if final:
        y = _rms(y, fg_ref[...])
    o_ref[...] = y


def _ffn(x, g, wg, wu, wd, fg, final, layer):
    n, d = x.shape
    tm = TOKEN_TILE
    row = pl.BlockSpec((tm, d), lambda i: (i, 0))
    return pl.pallas_call(
        functools.partial(_ffn_kernel, final=final),
        grid=(n // tm,),
        in_specs=[row, _resident((1, d)), _resident_layer(wg, layer), _resident_layer(wu, layer),
                  _resident_layer(wd, layer), _resident((1, d))],
        out_specs=row,
        out_shape=jax.ShapeDtypeStruct((n, d), F32),
        compiler_params=_cparams("parallel"),
        name="ffn",
    )(x, g.reshape(1, d), wg, wu, wd, fg.reshape(1, d))


HALO = 16


def _inproj_kernel(xp_ref, x_ref, xn_ref, g_ref, cos_ref, sin_ref, cw_ref, cb_ref, dtb_ref, dta_ref, *refs,
                   rope_scale, conv_at, dt_at, gather_at, tiles_per_seq):
    nw = len(rope_scale)
    u_tiles = refs[len(refs) - len(gather_at):]
    outs = list(refs[nw:len(refs) - len(gather_at)])
    pending = []
    i = pl.program_id(0)
    t = x_ref.shape[0]
    he = _rms(jnp.concatenate([xp_ref[...], x_ref[...], xn_ref[...]], axis=0), g_ref[...]).astype(BF16)
    h = he[HALO:HALO + t]
    for j, (w_ref, scale) in enumerate(zip(refs[:nw], rope_scale)):
        if j == conv_at:
            ext = _dot(he, w_ref[...])
            rows = t + 2 * HALO
            r = lax.broadcasted_iota(jnp.int32, (rows, 1), 0)
            lo = jnp.where(i % tiles_per_seq == 0, HALO, 0)
            hi = jnp.where(i % tiles_per_seq == tiles_per_seq - 1, t + HALO, rows)
            ext = jnp.where(r >= lo, jnp.where(r < hi, ext, 0.0), 0.0)

            def conv_tile(o_ref, c_out, c_in, ext=ext, rows=rows):
                cols = slice(c_in, c_in + LANES)
                e = ext[:, cols]
                acc = jnp.broadcast_to(cb_ref[:, cols], (t, LANES))
                half = SSD_CONV // 2
                for k in range(SSD_CONV):
                    shifted = pltpu.roll(e, (half - k) % rows, 0)
                    acc = acc + cw_ref[k:k + 1, cols] * shifted[HALO:HALO + t]
                o_ref[:, c_out:c_out + LANES] = _silu(acc).astype(o_ref.dtype)

            c_in = 0
            for o_ref in (outs.pop(0), outs.pop(0), outs.pop(0)):
                for c_out in range(0, o_ref.shape[1], LANES):
                    pending.append(functools.partial(conv_tile, o_ref, c_out, c_in))
                    c_in += LANES
            continue
        if pending:
            pending.pop(0)()
        y = _dot(h, w_ref[...])
        if scale is not None:
            y = _rotary(y, cos_ref[...], sin_ref[...])
            if scale != 1.0:
                y = y * scale
        if j == dt_at:
            u_tiles[0][...] = y
            _ssd_dt_body(u_tiles[0][...], dtb_ref, dta_ref, outs.pop(0), outs.pop(0), outs.pop(0), outs.pop(0),
                         SSD_CHUNK)
            continue
        if j in gather_at:
            u_tiles[gather_at.index(j)][...] = y
            if j == gather_at[-1]:
                _s5_gather_kernel(*u_tiles, outs.pop(0))
            continue
        o_ref = outs.pop(0)
        o_ref[...] = y.astype(o_ref.dtype)
    for task in pending:
        task()


def _inproj(x, g, cos, sin_s, conv_w, conv_b, dt_bias, dt_a, weights, layer, outputs, rope_scale, conv_at, dt_at,
            gather_at, seq):
    n, d = x.shape
    tm = WIDE_TOKEN_TILE
    nh = 2 * SSD_HEADS
    s5w = S5_CHUNK * S5_GROUP_CH
    hb = tm // HALO
    nblk = n // HALO
    row = lambda c: pl.BlockSpec((tm, c), lambda i: (i, 0))
    pos = pl.BlockSpec((tm, LANES), lambda i: (i % (seq // tm), 0))
    prev = pl.BlockSpec((HALO, d), lambda i: (jnp.maximum(i * hb - 1, 0), 0))
    nxt = pl.BlockSpec((HALO, d), lambda i: (jnp.minimum((i + 1) * hb, nblk - 1), 0))
    special = {"s5": (pl.BlockSpec((S5_GROUPS, tm // S5_CHUNK, s5w), lambda i: (0, i, 0)),
                      (S5_GROUPS, n // S5_CHUNK, s5w)),
               "dt_rows": (pl.BlockSpec((nh, tm), lambda i: (0, i)), (nh, n)),
               "dt_cols": (pl.BlockSpec((tm, nh), lambda i: (i, 0)), (n, nh))}
    specs = [special[k][0] if k in special else row(k) for k, _ in outputs]
    shapes = [jax.ShapeDtypeStruct(special[k][1] if k in special else (n, k), dt) for k, dt in outputs]
    return pl.pallas_call(
        functools.partial(_inproj_kernel, rope_scale=rope_scale, conv_at=conv_at, dt_at=dt_at,
                          gather_at=gather_at, tiles_per_seq=seq // tm),
        grid=(n // tm,),
        in_specs=[prev, row(d), nxt, _resident((1, d)), pos, pos, _resident(conv_w.shape),
                  _resident((1, conv_b.shape[0])), _resident((nh, 1)), _resident((nh, 1))]
                 + [_resident_layer(w, layer) for w in weights],
        out_specs=specs,
        out_shape=shapes,
        scratch_shapes=[pltpu.VMEM((tm, LANES), F32)] * len(gather_at),
        compiler_params=_cparams("parallel"),
        name="inproj",
    )(x, x, x, g.reshape(1, d), cos, sin_s, conv_w, conv_b.reshape(1, -1), dt_bias.reshape(nh, 1),
      dt_a.reshape(nh, 1), *weights)


def _ssd_dt_body(raw, bias_ref, a_ref, dt_ref, cs_ref, dtc_ref, csc_ref, chunk):
    nh = 2 * SSD_HEADS
    x = raw.T[:nh] + bias_ref[...]
    dt = jnp.maximum(x, 0.0) + jnp.log1p(jnp.exp(-jnp.abs(x)))
    cs = dt * a_ref[...]
    t = x.shape[1]
    pos = lax.broadcasted_iota(jnp.int32, x.shape, 1) % chunk
    fwd = lax.broadcasted_iota(jnp.int32, x.shape, 0) < SSD_HEADS
    k = 1
    while k < chunk:
        before = jnp.where(pos >= k, pltpu.roll(cs, k, 1), 0.0)
        after = jnp.where(pos < chunk - k, pltpu.roll(cs, t - k, 1), 0.0)
        cs = cs + jnp.where(fwd, before, after)
        k *= 2
    dt_ref[...] = dt
    cs_ref[...] = cs
    both = jnp.concatenate([dt, cs, jnp.zeros((LANES - 2 * nh, t), F32)], axis=0).T
    dtc_ref[...] = both[:, :nh]
    csc_ref[...] = both[:, nh:2 * nh]


def _ssd_chunk_state(xs, bm, wexp):
    xw = (xs.astype(F32) * wexp).astype(BF16)
    gw = SSD_INNER // SSD_GROUPS
    parts = [_dot_tn(bm[:, g * SSD_STATE:(g + 1) * SSD_STATE], xw[:, g * gw:(g + 1) * gw])
             for g in range(SSD_GROUPS)]
    return jnp.concatenate(parts, axis=1)


def _ssd_state_kernel(xs_f, bm_f, dtc_f, csc_f, xs_b, bm_b, dtc_b, csc_b, e_ref,
                      sf_out, sb_out, sf, sb):
    c = pl.program_id(1)
    h = SSD_HEADS
    l = SSD_CHUNK
    nt = xs_f.shape[0] // l

    @pl.when(c == 0)
    def _():
        sf[...] = jnp.zeros_like(sf)
        sb[...] = jnp.zeros_like(sb)

    lane = lax.broadcasted_iota(jnp.int32, (1, 2 * h), 1)
    e_f = e_ref[:, :SSD_INNER]
    e_b = e_ref[:, SSD_INNER:]
    for t in range(nt):
        rf = slice(t * l, (t + 1) * l)
        rb = slice((nt - 1 - t) * l, (nt - t) * l)
        sf_out[0, t] = sf[...].astype(sf_out.dtype)
        sb_out[0, nt - 1 - t] = sb[...].astype(sb_out.dtype)
        cs_f = csc_f[rf, :]
        tot_f = cs_f[l - 1:l, :]
        w_f = jnp.exp(tot_f - cs_f) * dtc_f[rf, :]
        cs_b = csc_b[rb, :]
        tot_b = cs_b[0:1, :]
        w_b = jnp.exp(tot_b - cs_b) * dtc_b[rb, :]
        wexp_f = _expand_heads(jnp.where(lane < h, w_f, 0.0), e_f)
        wexp_b = _expand_heads(jnp.where(lane >= h, w_b, 0.0), e_b)
        dec_f = _expand_heads(jnp.broadcast_to(jnp.exp(tot_f), (8, 2 * h)), e_f)[0:1]
        dec_b = _expand_heads(jnp.broadcast_to(jnp.exp(tot_b), (8, 2 * h)), e_b)[0:1]
        sf[...] = sf[...] * dec_f + _ssd_chunk_state(xs_f[rf, :], bm_f[rf, :], wexp_f)
        sb[...] = sb[...] * dec_b + _ssd_chunk_state(xs_b[rb, :], bm_b[rb, :], wexp_b)


def _ssd_states(xs, bm, dt_c, cs_c, e, batch, seq):
    l = SSD_CHUNK
    nt = STEP_CHUNKS
    nb = seq // (l * nt)
    fwd = lambda w: pl.BlockSpec((nt * l, w), lambda b, c: (b * nb + c, 0))
    bwd = lambda w: pl.BlockSpec((nt * l, w), lambda b, c: (b * nb + nb - 1 - c, 0))
    st = (batch, nb * nt, SSD_STATE, SSD_INNER)
    return pl.pallas_call(
        _ssd_state_kernel,
        grid=(batch, nb),
        in_specs=[fwd(SSD_INNER), fwd(LANES), fwd(16), fwd(16),
                  bwd(SSD_INNER), bwd(LANES), bwd(16), bwd(16), _resident(e.shape)],
        out_specs=[pl.BlockSpec((1, nt, SSD_STATE, SSD_INNER), lambda b, c: (b, c, 0, 0)),
                   pl.BlockSpec((1, nt, SSD_STATE, SSD_INNER), lambda b, c: (b, nb - 1 - c, 0, 0))],
        out_shape=[jax.ShapeDtypeStruct(st, BF16)] * 2,
        scratch_shapes=[pltpu.VMEM((SSD_STATE, SSD_INNER), F32)] * 2,
        compiler_params=_cparams("parallel", "arbitrary"),
        name="ssd_states",
    )(xs, bm, dt_c, cs_c, xs, bm, dt_c, cs_c, e)


def _ssd_out_kernel(xs_ref, bm_ref, cm_ref, z_ref, dtc_ref, csc_ref, dtr_ref, csr_ref,
                    sf_ref, sb_ref, e_ref, dskip_ref, ng_ref, o_ref):
    h = SSD_HEADS
    l = SSD_CHUNK
    gw = SSD_INNER // SSD_GROUPS
    ri = lax.broadcasted_iota(jnp.int32, (l, l), 0)
    ci = lax.broadcasted_iota(jnp.int32, (l, l), 1)
    causal = ri >= ci
    diag = ri == ci
    low = lax.broadcasted_iota(jnp.int32, (l, LANES), 1) < LANES // 2

    def block_diag(t):
        zero = jnp.zeros_like(t)
        return jnp.concatenate([jnp.where(low, t, zero), jnp.where(low, zero, t)], axis=0)

    for t in range(xs_ref.shape[0] // l):
        rows = slice(t * l, (t + 1) * l)
        xs = xs_ref[rows, :]
        cm = cm_ref[rows, :]
        csc = csc_ref[rows, :]
        dtr = dtr_ref[:, rows]
        col = csc * LOG2E
        row = jnp.log2(dtr) - csr_ref[:, rows] * LOG2E
        scores = _dot_nt(cm, block_diag(bm_ref[rows, :]))
        ys = []
        for pair in range(h // 2):
            g = pair // (h // 2 // SSD_GROUPS)
            sc = scores[:, g * l:(g + 1) * l]
            ms = []
            for hh in (2 * pair, 2 * pair + 1):
                w = jnp.exp2(jnp.where(causal, col[:, hh:hh + 1] + row[hh:hh + 1, :],
                                       col[:, h + hh:h + hh + 1] + row[h + hh:h + hh + 1, :]))
                w = w + jnp.where(diag, dtr[h + hh:h + hh + 1, :], 0.0)
                ms.append((sc * w).astype(BF16))
            ys.append(_dot(jnp.concatenate(ms, axis=1), block_diag(xs[:, pair * LANES:(pair + 1) * LANES])))
        y = jnp.concatenate(ys, axis=1)
        eo = _expand_heads(jnp.exp(csc), e_ref[...])
        off_f = []
        off_b = []
        for g in range(SSD_GROUPS):
            cg = cm[:, g * SSD_STATE:(g + 1) * SSD_STATE]
            off_f.append(_dot(cg, sf_ref[0, t, :, g * gw:(g + 1) * gw]))
            off_b.append(_dot(cg, sb_ref[0, t, :, g * gw:(g + 1) * gw]))
        y = (y + jnp.concatenate(off_f, axis=1) * eo[:, :SSD_INNER]
             + jnp.concatenate(off_b, axis=1) * eo[:, SSD_INNER:])
        y = y + dskip_ref[...] * xs.astype(F32)
        y = y * _silu(z_ref[rows, :].astype(F32))
        o_ref[rows, :] = _rms(y, ng_ref[...]).astype(o_ref.dtype)


def _ssd_out(xs, bm, cm, z, dt_c, cs_c, dt_r, cs_r, sf, sb, e, dskip, ng, batch, seq):
    n = xs.shape[0]
    nt = STEP_CHUNKS
    l = SSD_CHUNK * nt
    nc = seq // l
    row = lambda w: pl.BlockSpec((l, w), lambda b, c: (b * nc + c, 0))
    col = pl.BlockSpec((16, l), lambda b, c: (0, b * nc + c))
    st = pl.BlockSpec((1, nt, SSD_STATE, SSD_INNER), lambda b, c: (b, c, 0, 0))
    return pl.pallas_call(
        _ssd_out_kernel,
        grid=(batch, nc),
        in_specs=[row(SSD_INNER), row(LANES), row(LANES), row(SSD_INNER), row(16), row(16), col, col,
                  st, st, _resident(e.shape), _resident((1, SSD_INNER)), _resident((1, SSD_INNER))],
        out_specs=row(SSD_INNER),
        out_shape=jax.ShapeDtypeStruct((n, SSD_INNER), BF16),
        compiler_params=_cparams("parallel", "parallel"),
        name="ssd_out",
    )(xs, bm, cm, z, dt_c, cs_c, dt_r, cs_r, sf, sb, e, dskip, ng)


def _ssd_branch(z, xs, bm, cm, dt_r, cs_r, dt_c, cs_c, d_skip, norm_g, e, batch, seq):
    sf, sb = _ssd_states(xs, bm, dt_c, cs_c, e, batch, seq)
    dskip = jnp.repeat(d_skip.astype(F32), SSD_HEAD_DIM).reshape(1, SSD_INNER)
    return _ssd_out(xs, bm, cm, z, dt_c, cs_c, dt_r, cs_r, sf, sb, e, dskip,
                    norm_g.reshape(1, SSD_INNER), batch, seq)


def _s5_matrices(lam_re, lam_im, log_step, b_re, b_im, c_re, c_im, d_s5, nsteps):
    q = S5_CHUNK
    lr = jnp.minimum(lam_re.astype(F32), S5_MAX_RE)
    li = lam_im.astype(F32)
    step = jnp.exp(log_step.astype(F32))[..., None]
    tau = jnp.arange(q + 1, dtype=F32)[:, None, None, None]
    mag = jnp.exp(lr * step * tau)
    ang = li * step * tau
    pr = mag * jnp.cos(ang)
    pi = mag * jnp.sin(ang)
    den = lr * lr + li * li
    nr = pr[1] - 1.0
    coef_re = ((nr * lr + pi[1] * li) / den)[..., None]
    coef_im = ((pi[1] * lr - nr * li) / den)[..., None]
    br = b_re.astype(F32)
    bi = b_im.astype(F32)
    bb_re = coef_re * br - coef_im * bi
    bb_im = coef_re * bi + coef_im * br
    cr = c_re.astype(F32)
    ci = c_im.astype(F32)

    def to_state(p_r, p_i, d):
        p_r = p_r.transpose(1, 0, 2)[:, :, None, :]
        p_i = p_i.transpose(1, 0, 2)[:, :, None, :]
        b_r = bb_re[d].transpose(0, 2, 1)[:, None]
        b_i = bb_im[d].transpose(0, 2, 1)[:, None]
        return p_r * b_r - p_i * b_i, p_r * b_i + p_i * b_r

    sf_r, sf_i = to_state(pr[:q, 0][::-1], pi[:q, 0][::-1], 0)
    sb_r, sb_i = to_state(pr[:q, 1], pi[:q, 1], 1)
    bmat = jnp.concatenate([sf_r, sb_r, sf_i, sb_i], axis=-1)
    bmat = bmat.reshape(S5_GROUPS, q * S5_GROUP_CH, 4 * S5_STATE)

    def from_state(p_r, p_i, d, pad=(0, 0)):
        lanes_of_t = lambda p: jnp.repeat(jnp.pad(p.transpose(1, 2, 0), ((0, 0), (0, 0), pad)), S5_GROUP_CH, axis=2)
        p_r, p_i = lanes_of_t(p_r), lanes_of_t(p_i)
        reps = p_r.shape[2] // S5_GROUP_CH
        c_r = jnp.tile(cr[d].transpose(0, 2, 1), (1, 1, reps))
        c_i = jnp.tile(ci[d].transpose(0, 2, 1), (1, 1, reps))
        return c_r * p_r - c_i * p_i, c_r * p_i + c_i * p_r

    gf_r, gf_i = from_state(pr[:q, 0], pi[:q, 0], 0, (q - 1, 1))
    gb_r, gb_i = from_state(pr[:q, 1][::-1], pi[:q, 1][::-1], 1, (0, q))
    gen_c = jnp.concatenate([gf_r, gf_i, gb_r, gb_i], axis=1)
    tr = lambda t: t.transpose(0, 2, 1)
    gen_b = jnp.concatenate([tr(bb_re[0]), -tr(bb_im[0]), tr(bb_re[1]), -tr(bb_im[1])], axis=-1)
    d_gen = jnp.eye(S5_GROUP_CH, dtype=F32)[None] * d_s5.astype(F32)[:, None, :]
    d_gen = jnp.pad(d_gen[:, :, None, :], ((0, 0), (0, 0), (q - 1, q), (0, 0)))
    d_gen = d_gen.reshape(S5_GROUPS, S5_GROUP_CH, 2 * q * S5_GROUP_CH)

    of_r, of_i = from_state(pr[1:q + 1, 0], pi[1:q + 1, 0], 0)
    ob_r, ob_i = from_state(pr[1:q + 1, 1][::-1], pi[1:q + 1, 1][::-1], 1)
    cmat = jnp.concatenate([of_r, ob_r, -of_i, -ob_i], axis=1)
    m = (q * 2.0 ** jnp.arange(nsteps, dtype=F32))[:, None, None, None]
    amag = jnp.exp(lr * step * m)
    aang = li * step * m
    lanes = lambda t: jnp.concatenate([t[:, 0], t[:, 1]], axis=-1).transpose(1, 0, 2)
    return (gen_b, gen_c, d_gen, bmat.astype(BF16), cmat.astype(BF16),
            lanes(amag * jnp.cos(aang)), lanes(amag * jnp.sin(aang)))


def _s5_gather_kernel(*refs):
    u_refs, o_ref = refs[:-1], refs[-1]
    q = S5_CHUNK
    rb = o_ref.shape[1]
    per = LANES // S5_GROUP_CH
    slot = lax.broadcasted_iota(jnp.int32, (rb, LANES), 1) // S5_GROUP_CH
    for a in range(S5_GROUPS // per):
        for a2 in range(q // per):
            acc = [None] * per
            for sl in range(per):
                src = u_refs[a][pl.ds(a2 * per + sl, rb, stride=q), :]
                for gl in range(per):
                    piece = src if sl == gl else pltpu.roll(src, ((sl - gl) * S5_GROUP_CH) % LANES, 1)
                    acc[gl] = piece if sl == 0 else jnp.where(slot == sl, piece, acc[gl])
            for gl in range(per):
                o_ref[a * per + gl, :, a2 * LANES:(a2 + 1) * LANES] = acc[gl].astype(o_ref.dtype)


def _dot_f32(a, b):
    a0, a1, a2 = _split3(a)
    b0, b1, b2 = _split3(b)
    return ((_dot(a0, b0) + (_dot(a0, b1) + _dot(a1, b0)))
            + ((_dot(a0, b2) + _dot(a2, b0)) + _dot(a1, b1)))


def _s5_core_kernel(u_ref, gb_ref, gc_ref, gd_ref, b_ref, c_ref, apr_ref, api_ref, o_ref, toe, *, nc):
    q = S5_CHUNK
    w = q * S5_GROUP_CH
    r = u_ref.shape[1]
    p2 = 2 * S5_STATE
    kc = _dot_f32(gb_ref[0], gc_ref[0]) + gd_ref[0]
    for s in range(q):
        off = (q - 1 - s) * S5_GROUP_CH
        win = kc if off == 0 else pltpu.roll(kc, 2 * w - off, 1)
        toe[s * S5_GROUP_CH:(s + 1) * S5_GROUP_CH, :] = win[:, :w].astype(BF16)
    u = u_ref[0]
    loc = _dot(u, b_ref[0])
    xr, xi = loc[:, :p2], loc[:, p2:]
    row = lax.broadcasted_iota(jnp.int32, (r, p2), 0) % nc
    fwd = lax.broadcasted_iota(jnp.int32, (r, p2), 1) < S5_STATE

    def shift(v, k):
        dn = jnp.where(row >= k, pltpu.roll(v, k, 0), 0.0)
        up = jnp.where(row < nc - k, pltpu.roll(v, r - k, 0), 0.0)
        return jnp.where(fwd, dn, up)

    k, i = 1, 0
    while k < nc:
        sr, si = shift(xr, k), shift(xi, k)
        ar, ai = apr_ref[0, i:i + 1, :], api_ref[0, i:i + 1, :]
        xr, xi = xr + (ar * sr - ai * si), xi + (ar * si + ai * sr)
        k, i = 2 * k, i + 1
    h = jnp.concatenate([shift(xr, 1), shift(xi, 1)], axis=1).astype(BF16)
    y = _dot(u, toe[...]) + _dot(h, c_ref[0])
    o_ref[0] = jax.nn.gelu(y, approximate=True).astype(o_ref.dtype)


def _s5_core(u_g, mats, layer, batch):
    g, r, w = u_g.shape
    blk = pl.BlockSpec((1, r, w), lambda i: (i, 0, 0))
    per_group = lambda a: pl.BlockSpec((None, 1) + a.shape[2:], lambda i: (layer, i, 0, 0))
    return pl.pallas_call(
        functools.partial(_s5_core_kernel, nc=r // batch),
        grid=(g,),
        in_specs=[blk] + [per_group(a) for a in mats],
        out_specs=blk,
        out_shape=jax.ShapeDtypeStruct((g, r, w), BF16),
        scratch_shapes=[pltpu.VMEM((w, w), BF16)],
        compiler_params=_cparams("parallel"),
        name="s5_core",
    )(u_g, *mats)


def _s5_ungather_part(y_ref, nat_a, a, a2):
    q = S5_CHUNK
    rb = y_ref.shape[1]
    per = LANES // S5_GROUP_CH
    slot = lax.broadcasted_iota(jnp.int32, (rb, LANES), 1) // S5_GROUP_CH
    acc = [None] * per
    for gl in range(per):
        src = y_ref[a * per + gl, :, a2 * LANES:(a2 + 1) * LANES].astype(F32)
        for sl in range(per):
            piece = src if gl == sl else pltpu.roll(src, ((gl - sl) * S5_GROUP_CH) % LANES, 1)
            acc[sl] = piece if gl == 0 else jnp.where(slot == gl, piece, acc[sl])
    for sl in range(per):
        nat_a[pl.ds(a2 * per + sl, rb, stride=q), :] = acc[sl]


def _rotary(t, cos, sin_signed):
    w = t.shape[1]
    half = RET_HEAD_DIM // 2
    lane = lax.broadcasted_iota(jnp.int32, t.shape, 1) % RET_HEAD_DIM
    partner = jnp.where(lane < half, pltpu.roll(t, w - half, 1), pltpu.roll(t, half, 1))
    reps = w // cos.shape[1]
    return t * jnp.tile(cos, (1, reps)) + partner * jnp.tile(sin_signed, (1, reps))


def _ret_state_kernel(k_f, v_f, k_b, v_b, kdf_ref, kdb_ref, gl_ref, sf_out, sb_out, sf, sb):
    c = pl.program_id(1)
    l = RET_CHUNK
    nt = k_f.shape[0] // l
    pairs = RET_WIDTH // LANES

    @pl.when(c == 0)
    def _():
        sf[...] = jnp.zeros_like(sf)
        sb[...] = jnp.zeros_like(sb)

    same_head = (lax.broadcasted_iota(jnp.int32, (LANES, LANES), 0) // RET_HEAD_DIM
                 == lax.broadcasted_iota(jnp.int32, (LANES, LANES), 1) // RET_HEAD_DIM)
    for t in range(nt):
        rf = slice(t * l, (t + 1) * l)
        rb = slice((nt - 1 - t) * l, (nt - t) * l)
        sf_out[0, t] = sf[...].astype(sf_out.dtype)
        sb_out[0, nt - 1 - t] = sb[...].astype(sb_out.dtype)
        kf = (k_f[rf, :].astype(F32) * kdf_ref[...]).astype(BF16)
        kb = (k_b[rb, :].astype(F32) * kdb_ref[...]).astype(BF16)
        for p in range(pairs):
            sl = slice(p * LANES, (p + 1) * LANES)
            new_f = jnp.where(same_head, _dot_tn(kf[:, sl], v_f[rf, sl]), 0.0)
            new_b = jnp.where(same_head, _dot_tn(kb[:, sl], v_b[rb, sl]), 0.0)
            sf[p] = sf[p] * gl_ref[:, sl] + new_f
            sb[p] = sb[p] * gl_ref[:, sl] + new_b


def _ret_states(k, v, kdf, kdb, gl, batch, seq):
    nt = STEP_CHUNKS
    l = RET_CHUNK * nt
    nb = seq // l
    pairs = RET_WIDTH // LANES
    fwd = pl.BlockSpec((l, RET_WIDTH), lambda b, c: (b * nb + c, 0))
    bwd = pl.BlockSpec((l, RET_WIDTH), lambda b, c: (b * nb + nb - 1 - c, 0))
    st = (batch, nb * nt, pairs, LANES, LANES)
    return pl.pallas_call(
        _ret_state_kernel,
        grid=(batch, nb),
        in_specs=[fwd, fwd, bwd, bwd, _resident(kdf.shape), _resident(kdb.shape), _resident(gl.shape)],
        out_specs=[pl.BlockSpec((1, nt, pairs, LANES, LANES), lambda b, c: (b, c, 0, 0, 0)),
                   pl.BlockSpec((1, nt, pairs, LANES, LANES), lambda b, c: (b, nb - 1 - c, 0, 0, 0))],
        out_shape=[jax.ShapeDtypeStruct(st, BF16)] * 2,
        scratch_shapes=[pltpu.VMEM((pairs, LANES, LANES), F32)] * 2,
        compiler_params=_cparams("parallel", "arbitrary"),
        name="ret_states",
    )(k, v, k, v, kdf, kdb, gl)


def _ret_out_kernel(q_ref, k_ref, v_ref, g_ref, sf_ref, sb_ref, dm_ref, qdf_ref, qdb_ref, ng_ref, o_ref):
    l = RET_CHUNK
    hd = RET_HEAD_DIM
    low = lax.broadcasted_iota(jnp.int32, (l, LANES), 1) < hd

    def block_diag(t):
        zero = jnp.zeros_like(t)
        return jnp.concatenate([jnp.where(low, t, zero), jnp.where(low, zero, t)], axis=0)

    for t in range(q_ref.shape[0] // l):
        rows = slice(t * l, (t + 1) * l)
        q = q_ref[rows, :]
        qf = q.astype(F32)
        q_lf = (qf * qdf_ref[...]).astype(BF16)
        q_lb = (qf * qdb_ref[...]).astype(BF16)
        ys = []
        for p in range(RET_WIDTH // LANES):
            sl = slice(p * LANES, (p + 1) * LANES)
            s = _dot_nt(q[:, sl], block_diag(k_ref[rows, sl]))
            y = _dot((s * dm_ref[p]).astype(BF16), block_diag(v_ref[rows, sl]))
            state = jnp.concatenate([sf_ref[0, t, p], sb_ref[0, t, p]], axis=0)
            y = y + _dot(jnp.concatenate([q_lf[:, sl], q_lb[:, sl]], axis=1), state)
            y2 = y * y
            ms_a = jnp.sum(jnp.where(low, y2, 0.0), axis=-1, keepdims=True) * (1.0 / hd)
            ms_b = jnp.sum(jnp.where(low, 0.0, y2), axis=-1, keepdims=True) * (1.0 / hd)
            ys.append(y * jnp.where(low, lax.rsqrt(ms_a + EPS), lax.rsqrt(ms_b + EPS)))
        y = jnp.concatenate(ys, axis=1) * ng_ref[...]
        o_ref[rows, :] = (_silu(g_ref[rows, :].astype(F32)) * y).astype(o_ref.dtype)


def _ret_out(q, k, v, g, sf, sb, dm, qdf, qdb, ng, batch, seq):
    n = q.shape[0]
    nt = STEP_CHUNKS
    l = RET_CHUNK * nt
    nc = seq // l
    pairs = RET_WIDTH // LANES
    row = pl.BlockSpec((l, RET_WIDTH), lambda b, c: (b * nc + c, 0))
    st = pl.BlockSpec((1, nt, pairs, LANES, LANES), lambda b, c: (b, c, 0, 0, 0))
    return pl.pallas_call(
        _ret_out_kernel,
        grid=(batch, nc),
        in_specs=[row, row, row, row, st, st,
                  _resident(dm.shape), _resident(qdf.shape), _resident(qdb.shape), _resident((1, RET_WIDTH))],
        out_specs=row,
        out_shape=jax.ShapeDtypeStruct((n, RET_WIDTH), BF16),
        compiler_params=_cparams("parallel", "parallel"),
        name="ret_out",
    )(q, k, v, g, sf, sb, dm, qdf, qdb, ng)


def _ret_tables(seq):
    l = RET_CHUNK
    hd = RET_HEAD_DIM
    pos = np.arange(seq, dtype=np.float64)
    inv_freq = ROPE_BASE ** (-np.arange(0, hd, 2, dtype=np.float64) / hd)
    ang = pos[:, None] * inv_freq[None, :]
    cos = np.tile(np.cos(ang), (1, LANES // (hd // 2)))
    sin = np.sin(ang)
    sin_s = np.tile(np.concatenate([-sin, sin], axis=1), (1, LANES // hd))
    log_gamma = np.log1p(-np.exp2(-5.0 - np.arange(RET_HEADS, dtype=np.float64)))
    idx = np.arange(l, dtype=np.float64)
    dm = np.exp(log_gamma[:, None, None] * np.abs(idx[:, None] - idx[None, :]))
    dm = np.concatenate([dm[0::2], dm[1::2]], axis=2)
    per_head = lambda t: np.repeat(t, hd, axis=1)
    kdf = per_head(np.exp(log_gamma[None, :] * (l - 1.0 - idx)[:, None]))
    kdb = per_head(np.exp(log_gamma[None, :] * idx[:, None]))
    qdf = per_head(np.exp(log_gamma[None, :] * (idx + 1.0)[:, None]))
    qdb = per_head(np.exp(log_gamma[None, :] * (l - idx)[:, None]))
    gl = per_head(np.exp(log_gamma * l)[None, :])
    return tuple(jnp.asarray(t, F32) for t in (cos, sin_s, dm, kdf, kdb, qdf, qdb, gl))


def _ret_branch(q, k, v, g, norm_g, tables, batch, seq):
    _, _, dm, kdf, kdb, qdf, qdb, gl = tables
    sf, sb = _ret_states(k, v, kdf, kdb, gl, batch, seq)
    return _ret_out(q, k, v, g, sf, sb, dm, qdf, qdb, norm_g.reshape(1, RET_WIDTH), batch, seq)


def _merge_kernel(x_ref, ya_ref, yg_ref, yc_ref, g_ref, bgate_ref, wgate_ref, wa_ref, wv_ref, wgg_ref, wb_ref,
                  wc_ref, wo_ref, o_ref, *nat):
    x = x_ref[...]
    d = x.shape[1]
    h = _rms(x, g_ref[...]).astype(BF16)
    gates = []
    parts = S5_CHUNK * S5_GROUP_CH // LANES
    gw = d // parts
    for i in range(N_BRANCH):
        cols = []
        for a2 in range(parts):
            _s5_ungather_part(yg_ref, nat[i], i, a2)
            c0 = i * d + a2 * gw
            cols.append(jax.nn.sigmoid(_dot(h, wgate_ref[:, c0:c0 + gw]) + bgate_ref[:, c0:c0 + gw]))
        gates.append(jnp.concatenate(cols, axis=1))
    y5 = jnp.concatenate([t[...] for t in nat], axis=1).astype(BF16)
    yb = (_dot(y5, wv_ref[...]) * jax.nn.sigmoid(_dot(y5, wgg_ref[...]))).astype(BF16)
    branches = (_dot(ya_ref[...], wa_ref[...]), _dot(yb, wb_ref[...]), _dot(yc_ref[...], wc_ref[...]))
    mixed = gates[0] * branches[0] + gates[1] * branches[1] + gates[2] * branches[2]
    o_ref[...] = x + _dot(mixed.astype(BF16), wo_ref[...])


def _merge(x, ya, y_g, yc, g, bgate, stacked, layer):
    n, d = x.shape
    tm = WIDE_TOKEN_TILE
    row = lambda c: pl.BlockSpec((tm, c), lambda i: (i, 0))
    s5 = pl.BlockSpec((y_g.shape[0], tm // S5_CHUNK, y_g.shape[2]), lambda i: (0, i, 0))
    consts = (g.reshape(1, d), bgate.reshape(1, -1)) + tuple(stacked)
    return pl.pallas_call(
        _merge_kernel,
        grid=(n // tm,),
        in_specs=[row(d), row(ya.shape[1]), s5, row(yc.shape[1])] + [_resident(c.shape) for c in consts[:2]]
                 + [_resident_layer(w, layer) for w in stacked],
        out_specs=row(d),
        out_shape=jax.ShapeDtypeStruct((n, d), F32),
        scratch_shapes=[pltpu.VMEM((tm, LANES), F32)] * (S5_WIDTH // LANES),
        compiler_params=_cparams("parallel"),
        name="merge",
    )(x, ya, y_g, yc, *consts)


def _head_expander():
    return jnp.asarray(np.tile(np.repeat(np.eye(2 * SSD_HEADS), SSD_HEAD_DIM, axis=1), (3, 1)), BF16)


def kernel(x, ffn1_norm, ffn1_w_gate, ffn1_w_up, ffn1_w_down, mix_norm, w_in, b_gate, ssd_conv_w, ssd_conv_b, ssd_dt_bias, ssd_a_log, ssd_d, ssd_norm, w_br_ssd, s5_lam_re, s5_lam_im, s5_log_step, s5_b_re, s5_b_im, s5_c_re, s5_c_im, s5_d, s5_glu_wv, s5_glu_wg, w_br_s5, ret_norm, w_br_ret, w_out, ffn2_norm, ffn2_w_gate, ffn2_w_up, ffn2_w_down, final_norm):
    batch, seq, d = x.shape
    depth = w_in.shape[0]
    n = batch * seq
    assert seq % WIDE_TOKEN_TILE == 0 and seq % (STEP_CHUNKS * SSD_CHUNK) == 0 and seq % (STEP_CHUNKS * RET_CHUNK) == 0
    assert n % TOKEN_TILE == 0
    s5_steps = (seq // S5_CHUNK - 1).bit_length()
    assert 2 ** s5_steps == seq // S5_CHUNK
    bf = lambda w: w.astype(BF16)
    e = _head_expander()
    tables = _ret_tables(seq)
    offs = [0]
    for s in IN_PROJ_SIZES:
        offs.append(offs[-1] + s)
    xf = x.reshape(n, d).astype(F32)
    ffn1 = (bf(ffn1_w_gate), bf(ffn1_w_up), bf(ffn1_w_down))
    ffn2 = (bf(ffn2_w_gate), bf(ffn2_w_up), bf(ffn2_w_down))
    merge_w = (bf(w_br_ssd), bf(s5_glu_wv), bf(s5_glu_wg), bf(w_br_s5), bf(w_br_ret), bf(w_out))
    w_in_bf = bf(w_in)
    seg = [w_in_bf[:, :, offs[j]:offs[j + 1]] for j in range(8)]
    seg[2] = jnp.pad(seg[2], ((0, 0), (0, 0), (0, LANES - seg[2].shape[2])))
    seg[3:4] = [seg[3][:, :, c:c + LANES] for c in range(0, S5_WIDTH, LANES)]
    merge_w = (w_in_bf[:, :, offs[8]:],) + merge_w
    rope_scale = (None,) * 6 + (1.0, RET_HEAD_DIM ** -0.5, None, None)
    outputs = [(SSD_INNER, BF16), (SSD_INNER, BF16), (SSD_GROUPS * SSD_STATE, BF16), (SSD_GROUPS * SSD_STATE, BF16),
               ("dt_rows", F32), ("dt_rows", F32), ("dt_cols", F32), ("dt_cols", F32), ("s5", BF16),
               (RET_WIDTH, BF16), (RET_WIDTH, BF16), (RET_WIDTH, BF16), (RET_WIDTH, BF16)]
    dt_a = -jnp.exp(ssd_a_log.astype(F32))
    s5_mats = jax.vmap(functools.partial(_s5_matrices, nsteps=s5_steps))(
        s5_lam_re, s5_lam_im, s5_log_step, s5_b_re, s5_b_im, s5_c_re, s5_c_im, s5_d)
    for i in range(depth):
        xf = _ffn(xf, ffn1_norm[i], *ffn1, final_norm, False, i)
        z, xs, bm, cm, dt_r, cs_r, dt_c, cs_c, u_g, q, k, v, g = _inproj(
            xf, mix_norm[i], tables[0], tables[1], ssd_conv_w[i].astype(F32), ssd_conv_b[i].astype(F32),
            ssd_dt_bias[i].astype(F32), dt_a[i], seg, i, outputs, rope_scale, 1, 2, (3, 4, 5), seq)
        ya = _ssd_branch(z, xs, bm, cm, dt_r, cs_r, dt_c, cs_c, ssd_d[i], ssd_norm[i].astype(F32), e, batch, seq)
        yb = _s5_core(u_g, s5_mats, i, batch)
        yc = _ret_branch(q, k, v, g, ret_norm[i].astype(F32), tables, batch, seq)
        xf = _merge(xf, ya, yb, yc, mix_norm[i], b_gate[i].astype(F32), merge_w, i)
        xf = _ffn(xf, ffn2_norm[i], *ffn2, final_norm, i == depth - 1, i)
    return xf.reshape(batch, seq, d).astype(x.dtype)
```

```python
import functools
import math

import jax
import jax.numpy as jnp
import numpy as np
from jax import lax
from jax.experimental import pallas as pl
from jax.experimental.pallas import tpu as pltpu

F32 = jnp.float32
BF16 = jnp.bfloat16

EPS = 1e-6
LOG2E = 1.4426950408889634
SSD_HEADS = 8
SSD_HEAD_DIM = 64
SSD_INNER = 512
SSD_GROUPS = 2
SSD_STATE = 64
SSD_CONV = 5
SSD_CONV_CH = 768
S5_GROUP_CH = 16
S5_GROUPS = 24
S5_WIDTH = 384
S5_STATE = 64
S5_MAX_RE = -1e-4
RET_HEADS = 8
RET_HEAD_DIM = 64
RET_WIDTH = 512
ROPE_BASE = 10000.0
N_BRANCH = 3
IN_PROJ_SIZES = (512, 768, 16, 384, 512, 512, 512, 512, 3072)

V7X_VMEM_BYTES = 64 * 1024 * 1024
VMEM_LIMIT = V7X_VMEM_BYTES - 8 * 1024 * 1024
LANES = 128

TOKEN_TILE = 512
WIDE_TOKEN_TILE = 1024
SSD_CHUNK = 128
RET_CHUNK = 128
STEP_CHUNKS = 8
S5_CHUNK = 32


def _cparams(*sem):
    return pltpu.CompilerParams(dimension_semantics=sem, vmem_limit_bytes=VMEM_LIMIT)


def _resident(shape):
    nd = len(shape)
    return pl.BlockSpec(shape, lambda *_: (0,) * nd, pipeline_mode=pl.Buffered(1))


def _resident_layer(stacked, layer):
    nd = stacked.ndim - 1
    return pl.BlockSpec((None,) + stacked.shape[1:], lambda *_: (layer,) + (0,) * nd,
                        pipeline_mode=pl.Buffered(1))


def _rms(x, g):
    return (x * lax.rsqrt(jnp.mean(x * x, axis=-1, keepdims=True) + EPS)) * g


def _silu(x):
    return x * jax.nn.sigmoid(x)


def _dot(a, b):
    return jnp.dot(a, b, preferred_element_type=F32)


def _dot_nt(a, b):
    return lax.dot_general(a, b, (((1,), (1,)), ((), ())), preferred_element_type=F32)


def _dot_tn(a, b):
    return lax.dot_general(a, b, (((0,), (0,)), ((), ())), preferred_element_type=F32)


def _split3(v):
    hi = v.astype(BF16)
    r1 = v - hi.astype(F32)
    mid = r1.astype(BF16)
    lo = (r1 - mid.astype(F32)).astype(BF16)
    return hi, mid, lo


def _expand_heads(v, e3):
    return _dot(jnp.concatenate(_split3(v), axis=1), e3)


def _ffn_kernel(x_ref, g_ref, wg_ref, wu_ref, wd_ref, fg_ref, o_ref, *, final):
    x = x_ref[...]
    h = _rms(x, g_ref[...]).astype(BF16)
    a = _dot(h, wg_ref[...])
    b = _dot(h, wu_ref[...])
    t = (_silu(a) * b).astype(BF16)
    y = x + 0.5 * _dot(t, wd_ref[...])
    if final:
        y = _rms(y, fg_ref[...])
    o_ref[...] = y


def _ffn(x, g, wg, wu, wd, fg, final, layer):
    n, d = x.shape
    tm = TOKEN_TILE
    row = pl.BlockSpec((tm, d), lambda i: (i, 0))
    return pl.pallas_call(
        functools.partial(_ffn_kernel, final=final),
        grid=(n // tm,),
        in_specs=[row, _resident((1, d)), _resident_layer(wg, layer), _resident_layer(wu, layer),
                  _resident_layer(wd, layer), _resident((1, d))],
        out_specs=row,
        out_shape=jax.ShapeDtypeStruct((n, d), F32),
        compiler_params=_cparams("parallel"),
        name="ffn",
    )(x, g.reshape(1, d), wg, wu, wd, fg.reshape(1, d))


HALO = 16


def _inproj_kernel(xp_ref, x_ref, xn_ref, g_ref, cos_ref, sin_ref, cw_ref, cb_ref, dtb_ref, dta_ref, *refs,
                   rope_scale, conv_at, dt_at, gather_at, tiles_per_seq):
    nw = len(rope_scale)
    u_tiles = refs[len(refs) - len(gather_at):]
    outs = list(refs[nw:len(refs) - len(gather_at)])
    pending = []
    i = pl.program_id(0)
    t = x_ref.shape[0]
    he = _rms(jnp.concatenate([xp_ref[...], x_ref[...], xn_ref[...]], axis=0), g_ref[...]).astype(BF16)
    h = he[HALO:HALO + t]
    for j, (w_ref, scale) in enumerate(zip(refs[:nw], rope_scale)):
        if j == conv_at:
            ext = _dot(he, w_ref[...])
            rows = t + 2 * HALO
            r = lax.broadcasted_iota(jnp.int32, (rows, 1), 0)
            lo = jnp.where(i % tiles_per_seq == 0, HALO, 0)
            hi = jnp.where(i % tiles_per_seq == tiles_per_seq - 1, t + HALO, rows)
            ext = jnp.where(r >= lo, jnp.where(r < hi, ext, 0.0), 0.0)

            def conv_tile(o_ref, c_out, c_in, ext=ext, rows=rows):
                cols = slice(c_in, c_in + LANES)
                e = ext[:, cols]
                acc = jnp.broadcast_to(cb_ref[:, cols], (t, LANES))
                half = SSD_CONV // 2
                for k in range(SSD_CONV):
                    shifted = pltpu.roll(e, (half - k) % rows, 0)
                    acc = acc + cw_ref[k:k + 1, cols] * shifted[HALO:HALO + t]
                o_ref[:, c_out:c_out + LANES] = _silu(acc).astype(o_ref.dtype)

            c_in = 0
            for o_ref in (outs.pop(0), outs.pop(0), outs.pop(0)):
                for c_out in range(0, o_ref.shape[1], LANES):
                    pending.append(functools.partial(conv_tile, o_ref, c_out, c_in))
                    c_in += LANES
            continue
        if pending:
            pending.pop(0)()
        y = _dot(h, w_ref[...])
        if scale is not None:
            y = _rotary(y, cos_ref[...], sin_ref[...])
            if scale != 1.0:
                y = y * scale
        if j == dt_at:
            u_tiles[0][...] = y
            _ssd_dt_body(u_tiles[0][...], dtb_ref, dta_ref, outs.pop(0), outs.pop(0), outs.pop(0), outs.pop(0),
                         SSD_CHUNK)
            continue
        if j in gather_at:
            u_tiles[gather_at.index(j)][...] = y
            if j == gather_at[-1]:
                _s5_gather_kernel(*u_tiles, outs.pop(0))
            continue
        o_ref = outs.pop(0)
        o_ref[...] = y.astype(o_ref.dtype)
    for task in pending:
        task()


def _inproj(x, g, cos, sin_s, conv_w, conv_b, dt_bias, dt_a, weights, layer, outputs, rope_scale, conv_at, dt_at,
            gather_at, seq):
    n, d = x.shape
    tm = WIDE_TOKEN_TILE
    nh = 2 * SSD_HEADS
    s5w = S5_CHUNK * S5_GROUP_CH
    hb = tm // HALO
    nblk = n // HALO
    row = lambda c: pl.BlockSpec((tm, c), lambda i: (i, 0))
    pos = pl.BlockSpec((tm, LANES), lambda i: (i % (seq // tm), 0))
    prev = pl.BlockSpec((HALO, d), lambda i: (jnp.maximum(i * hb - 1, 0), 0))
    nxt = pl.BlockSpec((HALO, d), lambda i: (jnp.minimum((i + 1) * hb, nblk - 1), 0))
    special = {"s5": (pl.BlockSpec((S5_GROUPS, tm // S5_CHUNK, s5w), lambda i: (0, i, 0)),
                      (S5_GROUPS, n // S5_CHUNK, s5w)),
               "dt_rows": (pl.BlockSpec((nh, tm), lambda i: (0, i)), (nh, n)),
               "dt_cols": (pl.BlockSpec((tm, nh), lambda i: (i, 0)), (n, nh))}
    specs = [special[k][0] if k in special else row(k) for k, _ in outputs]
    shapes = [jax.ShapeDtypeStruct(special[k][1] if k in special else (n, k), dt) for k, dt in outputs]
    return pl.pallas_call(
        functools.partial(_inproj_kernel, rope_scale=rope_scale, conv_at=conv_at, dt_at=dt_at,
                          gather_at=gather_at, tiles_per_seq=seq // tm),
        grid=(n // tm,),
        in_specs=[prev, row(d), nxt, _resident((1, d)), pos, pos, _resident(conv_w.shape),
                  _resident((1, conv_b.shape[0])), _resident((nh, 1)), _resident((nh, 1))]
                 + [_resident_layer(w, layer) for w in weights],
        out_specs=specs,
        out_shape=shapes,
        scratch_shapes=[pltpu.VMEM((tm, LANES), F32)] * len(gather_at),
        compiler_params=_cparams("parallel"),
        name="inproj",
    )(x, x, x, g.reshape(1, d), cos, sin_s, conv_w, conv_b.reshape(1, -1), dt_bias.reshape(nh, 1),
      dt_a.reshape(nh, 1), *weights)


def _ssd_dt_body(raw, bias_ref, a_ref, dt_ref, cs_ref, dtc_ref, csc_ref, chunk):
    nh = 2 * SSD_HEADS
    x = raw.T[:nh] + bias_ref[...]
    dt = jnp.maximum(x, 0.0) + jnp.log1p(jnp.exp(-jnp.abs(x)))
    cs = dt * a_ref[...]
    t = x.shape[1]
    pos = lax.broadcasted_iota(jnp.int32, x.shape, 1) % chunk
    fwd = lax.broadcasted_iota(jnp.int32, x.shape, 0) < SSD_HEADS
    k = 1
    while k < chunk:
        before = jnp.where(pos >= k, pltpu.roll(cs, k, 1), 0.0)
        after = jnp.where(pos < chunk - k, pltpu.roll(cs, t - k, 1), 0.0)
        cs = cs + jnp.where(fwd, before, after)
        k *= 2
    dt_ref[...] = dt
    cs_ref[...] = cs
    both = jnp.concatenate([dt, cs, jnp.zeros((LANES - 2 * nh, t), F32)], axis=0).T
    dtc_ref[...] = both[:, :nh]
    csc_ref[...] = both[:, nh:2 * nh]


def _ssd_chunk_state(xs, bm, wexp):
    xw = (xs.astype(F32) * wexp).astype(BF16)
    gw = SSD_INNER // SSD_GROUPS
    parts = [_dot_tn(bm[:, g * SSD_STATE:(g + 1) * SSD_STATE], xw[:, g * gw:(g + 1) * gw])
             for g in range(SSD_GROUPS)]
    return jnp.concatenate(parts, axis=1)


def _ssd_state_kernel(xs_f, bm_f, dtc_f, csc_f, xs_b, bm_b, dtc_b, csc_b, e_ref,
                      sf_out, sb_out, sf, sb):
    c = pl.program_id(1)
    h = SSD_HEADS
    l = SSD_CHUNK
    nt = xs_f.shape[0] // l

    @pl.when(c == 0)
    def _():
        sf[...] = jnp.zeros_like(sf)
        sb[...] = jnp.zeros_like(sb)

    lane = lax.broadcasted_iota(jnp.int32, (1, 2 * h), 1)
    e_f = e_ref[:, :SSD_INNER]
    e_b = e_ref[:, SSD_INNER:]
    for t in range(nt):
        rf = slice(t * l, (t + 1) * l)
        rb = slice((nt - 1 - t) * l, (nt - t) * l)
        sf_out[0, t] = sf[...].astype(sf_out.dtype)
        sb_out[0, nt - 1 - t] = sb[...].astype(sb_out.dtype)
        cs_f = csc_f[rf, :]
        tot_f = cs_f[l - 1:l, :]
        w_f = jnp.exp(tot_f - cs_f) * dtc_f[rf, :]
        cs_b = csc_b[rb, :]
        tot_b = cs_b[0:1, :]
        w_b = jnp.exp(tot_b - cs_b) * dtc_b[rb, :]
        wexp_f = _expand_heads(jnp.where(lane < h, w_f, 0.0), e_f)
        wexp_b = _expand_heads(jnp.where(lane >= h, w_b, 0.0), e_b)
        dec_f = _expand_heads(jnp.broadcast_to(jnp.exp(tot_f), (8, 2 * h)), e_f)[0:1]
        dec_b = _expand_heads(jnp.broadcast_to(jnp.exp(tot_b), (8, 2 * h)), e_b)[0:1]
        sf[...] = sf[...] * dec_f + _ssd_chunk_state(xs_f[rf, :], bm_f[rf, :], wexp_f)
        sb[...] = sb[...] * dec_b + _ssd_chunk_state(xs_b[rb, :], bm_b[rb, :], wexp_b)


def _ssd_states(xs, bm, dt_c, cs_c, e, batch, seq):
    l = SSD_CHUNK
    nt = STEP_CHUNKS
    nb = seq // (l * nt)
    fwd = lambda w: pl.BlockSpec((nt * l, w), lambda b, c: (b * nb + c, 0))
    bwd = lambda w: pl.BlockSpec((nt * l, w), lambda b, c: (b * nb + nb - 1 - c, 0))
    st = (batch, nb * nt, SSD_STATE, SSD_INNER)
    return pl.pallas_call(
        _ssd_state_kernel,
        grid=(batch, nb),
        in_specs=[fwd(SSD_INNER), fwd(LANES), fwd(16), fwd(16),
                  bwd(SSD_INNER), bwd(LANES), bwd(16), bwd(16), _resident(e.shape)],
        out_specs=[pl.BlockSpec((1, nt, SSD_STATE, SSD_INNER), lambda b, c: (b, c, 0, 0)),
                   pl.BlockSpec((1, nt, SSD_STATE, SSD_INNER), lambda b, c: (b, nb - 1 - c, 0, 0))],
        out_shape=[jax.ShapeDtypeStruct(st, BF16)] * 2,
        scratch_shapes=[pltpu.VMEM((SSD_STATE, SSD_INNER), F32)] * 2,
        compiler_params=_cparams("parallel", "arbitrary"),
        name="ssd_states",
    )(xs, bm, dt_c, cs_c, xs, bm, dt_c, cs_c, e)


def _ssd_out_kernel(xs_ref, bm_ref, cm_ref, z_ref, dtc_ref, csc_ref, dtr_ref, csr_ref,
                    sf_ref, sb_ref, e_ref, dskip_ref, ng_ref, o_ref):
    h = SSD_HEADS
    l = SSD_CHUNK
    gw = SSD_INNER // SSD_GROUPS
    ri = lax.broadcasted_iota(jnp.int32, (l, l), 0)
    ci = lax.broadcasted_iota(jnp.int32, (l, l), 1)
    causal = ri >= ci
    diag = ri == ci
    low = lax.broadcasted_iota(jnp.int32, (l, LANES), 1) < LANES // 2

    def block_diag(t):
        zero = jnp.zeros_like(t)
        return jnp.concatenate([jnp.where(low, t, zero), jnp.where(low, zero, t)], axis=0)

    for t in range(xs_ref.shape[0] // l):
        rows = slice(t * l, (t + 1) * l)
        xs = xs_ref[rows, :]
        cm = cm_ref[rows, :]
        csc = csc_ref[rows, :]
        dtr = dtr_ref[:, rows]
        col = csc * LOG2E
        row = jnp.log2(dtr) - csr_ref[:, rows] * LOG2E
        scores = _dot_nt(cm, block_diag(bm_ref[rows, :]))
        ys = []
        for pair in range(h // 2):
            g = pair // (h // 2 // SSD_GROUPS)
            sc = scores[:, g * l:(g + 1) * l]
            ms = []
            for hh in (2 * pair, 2 * pair + 1):
                w = jnp.exp2(jnp.where(causal, col[:, hh:hh + 1] + row[hh:hh + 1, :],
                                       col[:, h + hh:h + hh + 1] + row[h + hh:h + hh + 1, :]))
                w = w + jnp.where(diag, dtr[h + hh:h + hh + 1, :], 0.0)
                ms.append((sc * w).astype(BF16))
            ys.append(_dot(jnp.concatenate(ms, axis=1), block_diag(xs[:, pair * LANES:(pair + 1) * LANES])))
        y = jnp.concatenate(ys, axis=1)
        eo = _expand_heads(jnp.exp(csc), e_ref[...])
        off_f = []
        off_b = []
        for g in range(SSD_GROUPS):
            cg = cm[:, g * SSD_STATE:(g + 1) * SSD_STATE]
            off_f.append(_dot(cg, sf_ref[0, t, :, g * gw:(g + 1) * gw]))
            off_b.append(_dot(cg, sb_ref[0, t, :, g * gw:(g + 1) * gw]))
        y = (y + jnp.concatenate(off_f, axis=1) * eo[:, :SSD_INNER]
             + jnp.concatenate(off_b, axis=1) * eo[:, SSD_INNER:])
        y = y + dskip_ref[...] * xs.astype(F32)
        y = y * _silu(z_ref[rows, :].astype(F32))
        o_ref[rows, :] = _rms(y, ng_ref[...]).astype(o_ref.dtype)


def _ssd_out(xs, bm, cm, z, dt_c, cs_c, dt_r, cs_r, sf, sb, e, dskip, ng, batch, seq):
    n = xs.shape[0]
    nt = STEP_CHUNKS
    l = SSD_CHUNK * nt
    nc = seq // l
    row = lambda w: pl.BlockSpec((l, w), lambda b, c: (b * nc + c, 0))
    col = pl.BlockSpec((16, l), lambda b, c: (0, b * nc + c))
    st = pl.BlockSpec((1, nt, SSD_STATE, SSD_INNER), lambda b, c: (b, c, 0, 0))
    return pl.pallas_call(
        _ssd_out_kernel,
        grid=(batch, nc),
        in_specs=[row(SSD_INNER), row(LANES), row(LANES), row(SSD_INNER), row(16), row(16), col, col,
                  st, st, _resident(e.shape), _resident((1, SSD_INNER)), _resident((1, SSD_INNER))],
        out_specs=row(SSD_INNER),
        out_shape=jax.ShapeDtypeStruct((n, SSD_INNER), BF16),
        compiler_params=_cparams("parallel", "parallel"),
        name="ssd_out",
    )(xs, bm, cm, z, dt_c, cs_c, dt_r, cs_r, sf, sb, e, dskip, ng)


def _ssd_branch(z, xs, bm, cm, dt_r, cs_r, dt_c, cs_c, d_skip, norm_g, e, batch, seq):
    sf, sb = _ssd_states(xs, bm, dt_c, cs_c, e, batch, seq)
    dskip = jnp.repeat(d_skip.astype(F32), SSD_HEAD_DIM).reshape(1, SSD_INNER)
    return _ssd_out(xs, bm, cm, z, dt_c, cs_c, dt_r, cs_r, sf, sb, e, dskip,
                    norm_g.reshape(1, SSD_INNER), batch, seq)


def _s5_matrices(lam_re, lam_im, log_step, b_re, b_im, c_re, c_im, d_s5, nsteps):
    q = S5_CHUNK
    lr = jnp.minimum(lam_re.astype(F32), S5_MAX_RE)
    li = lam_im.astype(F32)
    step = jnp.exp(log_step.astype(F32))[..., None]
    tau = jnp.arange(q + 1, dtype=F32)[:, None, None, None]
    mag = jnp.exp(lr * step * tau)
    ang = li * step * tau
    pr = mag * jnp.cos(ang)
    pi = mag * jnp.sin(ang)
    den = lr * lr + li * li
    nr = pr[1] - 1.0
    coef_re = ((nr * lr + pi[1] * li) / den)[..., None]
    coef_im = ((pi[1] * lr - nr * li) / den)[..., None]
    br = b_re.astype(F32)
    bi = b_im.astype(F32)
    bb_re = coef_re * br - coef_im * bi
    bb_im = coef_re * bi + coef_im * br
    cr = c_re.astype(F32)
    ci = c_im.astype(F32)

    def to_state(p_r, p_i, d):
        p_r = p_r.transpose(1, 0, 2)[:, :, None, :]
        p_i = p_i.transpose(1, 0, 2)[:, :, None, :]
        b_r = bb_re[d].transpose(0, 2, 1)[:, None]
        b_i = bb_im[d].transpose(0, 2, 1)[:, None]
        return p_r * b_r - p_i * b_i, p_r * b_i + p_i * b_r

    sf_r, sf_i = to_state(pr[:q, 0][::-1], pi[:q, 0][::-1], 0)
    sb_r, sb_i = to_state(pr[:q, 1], pi[:q, 1], 1)
    bmat = jnp.concatenate([sf_r, sb_r, sf_i, sb_i], axis=-1)
    bmat = bmat.reshape(S5_GROUPS, q * S5_GROUP_CH, 4 * S5_STATE)

    def from_state(p_r, p_i, d, pad=(0, 0)):
        lanes_of_t = lambda p: jnp.repeat(jnp.pad(p.transpose(1, 2, 0), ((0, 0), (0, 0), pad)), S5_GROUP_CH, axis=2)
        p_r, p_i = lanes_of_t(p_r), lanes_of_t(p_i)
        reps = p_r.shape[2] // S5_GROUP_CH
        c_r = jnp.tile(cr[d].transpose(0, 2, 1), (1, 1, reps))
        c_i = jnp.tile(ci[d].transpose(0, 2, 1), (1, 1, reps))
        return c_r * p_r - c_i * p_i, c_r * p_i + c_i * p_r

    gf_r, gf_i = from_state(pr[:q, 0], pi[:q, 0], 0, (q - 1, 1))
    gb_r, gb_i = from_state(pr[:q, 1][::-1], pi[:q, 1][::-1], 1, (0, q))
    gen_c = jnp.concatenate([gf_r, gf_i, gb_r, gb_i], axis=1)
    tr = lambda t: t.transpose(0, 2, 1)
    gen_b = jnp.concatenate([tr(bb_re[0]), -tr(bb_im[0]), tr(bb_re[1]), -tr(bb_im[1])], axis=-1)
    lane = lax.broadcasted_iota(jnp.int32, (S5_GROUP_CH, 2 * q * S5_GROUP_CH), 1)
    row = lax.broadcasted_iota(jnp.int32, (S5_GROUP_CH, 2 * q * S5_GROUP_CH), 0)
    d_gen = jnp.where(lane == (q - 1) * S5_GROUP_CH + row, d_s5.astype(F32)[:, :, None], 0.0)

    of_r, of_i = from_state(pr[1:q + 1, 0], pi[1:q + 1, 0], 0)
    ob_r, ob_i = from_state(pr[1:q + 1, 1][::-1], pi[1:q + 1, 1][::-1], 1)
    cmat = jnp.concatenate([of_r, ob_r, -of_i, -ob_i], axis=1)
    m = (q * 2.0 ** jnp.arange(nsteps, dtype=F32))[:, None, None, None]
    amag = jnp.exp(lr * step * m)
    aang = li * step * m
    lanes = lambda t: jnp.concatenate([t[:, 0], t[:, 1]], axis=-1).transpose(1, 0, 2)
    return (gen_b, gen_c, d_gen, bmat.astype(BF16), cmat.astype(BF16),
            lanes(amag * jnp.cos(aang)), lanes(amag * jnp.sin(aang)))


def _s5_gather_kernel(*refs):
    u_refs, o_ref = refs[:-1], refs[-1]
    q = S5_CHUNK
    rb = o_ref.shape[1]
    per = LANES // S5_GROUP_CH
    slot = lax.broadcasted_iota(jnp.int32, (rb, LANES), 1) // S5_GROUP_CH
    for a in range(S5_GROUPS // per):
        for a2 in range(q // per):
            acc = [None] * per
            for sl in range(per):
                src = u_refs[a][pl.ds(a2 * per + sl, rb, stride=q), :]
                for gl in range(per):
                    piece = src if sl == gl else pltpu.roll(src, ((sl - gl) * S5_GROUP_CH) % LANES, 1)
                    acc[gl] = piece if sl == 0 else jnp.where(slot == sl, piece, acc[gl])
            for gl in range(per):
                o_ref[a * per + gl, :, a2 * LANES:(a2 + 1) * LANES] = acc[gl].astype(o_ref.dtype)


def _dot_f32(a, b):
    a0, a1, a2 = _split3(a)
    b0, b1, b2 = _split3(b)
    return ((_dot(a0, b0) + (_dot(a0, b1) + _dot(a1, b0)))
            + ((_dot(a0, b2) + _dot(a2, b0)) + _dot(a1, b1)))


def _s5_core_kernel(u_ref, gb_ref, gc_ref, gd_ref, b_ref, c_ref, apr_ref, api_ref, o_ref, toe, *, nc):
    q = S5_CHUNK
    w = q * S5_GROUP_CH
    r = u_ref.shape[1]
    p2 = 2 * S5_STATE
    kc = _dot_f32(gb_ref[0], gc_ref[0]) + gd_ref[0]
    for s in range(q):
        off = (q - 1 - s) * S5_GROUP_CH
        win = kc if off == 0 else pltpu.roll(kc, 2 * w - off, 1)
        toe[s * S5_GROUP_CH:(s + 1) * S5_GROUP_CH, :] = win[:, :w].astype(BF16)
    u = u_ref[0]
    loc = _dot(u, b_ref[0])
    xr, xi = loc[:, :p2], loc[:, p2:]
    row = lax.broadcasted_iota(jnp.int32, (r, p2), 0) % nc
    fwd = lax.broadcasted_iota(jnp.int32, (r, p2), 1) < S5_STATE

    def shift(v, k):
        dn = jnp.where(row >= k, pltpu.roll(v, k, 0), 0.0)
        up = jnp.where(row < nc - k, pltpu.roll(v, r - k, 0), 0.0)
        return jnp.where(fwd, dn, up)

    k, i = 1, 0
    while k < nc:
        sr, si = shift(xr, k), shift(xi, k)
        ar, ai = apr_ref[0, i:i + 1, :], api_ref[0, i:i + 1, :]
        xr, xi = xr + (ar * sr - ai * si), xi + (ar * si + ai * sr)
        k, i = 2 * k, i + 1
    h = jnp.concatenate([shift(xr, 1), shift(xi, 1)], axis=1).astype(BF16)
    y = _dot(u, toe[...]) + _dot(h, c_ref[0])
    o_ref[0] = jax.nn.gelu(y, approximate=True).astype(o_ref.dtype)


def _s5_core(u_g, mats, layer, batch):
    g, r, w = u_g.shape
    blk = pl.BlockSpec((1, r, w), lambda i: (i, 0, 0))
    per_group = lambda a: pl.BlockSpec((None, 1) + a.shape[2:], lambda i: (layer, i, 0, 0))
    return pl.pallas_call(
        functools.partial(_s5_core_kernel, nc=r // batch),
        grid=(g,),
        in_specs=[blk] + [per_group(a) for a in mats],
        out_specs=blk,
        out_shape=jax.ShapeDtypeStruct((g, r, w), BF16),
        scratch_shapes=[pltpu.VMEM((w, w), BF16)],
        compiler_params=_cparams("parallel"),
        name="s5_core",
    )(u_g, *mats)


def _s5_ungather_part(y_ref, nat_a, a, a2):
    q = S5_CHUNK
    rb = y_ref.shape[1]
    per = LANES // S5_GROUP_CH
    slot = lax.broadcasted_iota(jnp.int32, (rb, LANES), 1) // S5_GROUP_CH
    acc = [None] * per
    for gl in range(per):
        src = y_ref[a * per + gl, :, a2 * LANES:(a2 + 1) * LANES].astype(F32)
        for sl in range(per):
            piece = src if gl == sl else pltpu.roll(src, ((gl - sl) * S5_GROUP_CH) % LANES, 1)
            acc[sl] = piece if gl == 0 else jnp.where(slot == gl, piece, acc[sl])
    for sl in range(per):
        nat_a[pl.ds(a2 * per + sl, rb, stride=q), :] = acc[sl]


def _rotary(t, cos, sin_signed):
    w = t.shape[1]
    half = RET_HEAD_DIM // 2
    lane = lax.broadcasted_iota(jnp.int32, t.shape, 1) % RET_HEAD_DIM
    partner = jnp.where(lane < half, pltpu.roll(t, w - half, 1), pltpu.roll(t, half, 1))
    reps = w // cos.shape[1]
    return t * jnp.tile(cos, (1, reps)) + partner * jnp.tile(sin_signed, (1, reps))


def _ret_state_kernel(k_f, v_f, k_b, v_b, kdf_ref, kdb_ref, gl_ref, sf_out, sb_out, sf, sb):
    c = pl.program_id(1)
    l = RET_CHUNK
    nt = k_f.shape[0] // l
    pairs = RET_WIDTH // LANES

    @pl.when(c == 0)
    def _():
        sf[...] = jnp.zeros_like(sf)
        sb[...] = jnp.zeros_like(sb)

    same_head = (lax.broadcasted_iota(jnp.int32, (LANES, LANES), 0) // RET_HEAD_DIM
                 == lax.broadcasted_iota(jnp.int32, (LANES, LANES), 1) // RET_HEAD_DIM)
    for t in range(nt):
        rf = slice(t * l, (t + 1) * l)
        rb = slice((nt - 1 - t) * l, (nt - t) * l)
        hd = RET_HEAD_DIM
        for p in range(pairs):
            sl = slice(p * LANES, (p + 1) * LANES)
            sf_out[0, t, :, sl] = (sf[p, :hd, :] + sf[p, hd:, :]).astype(sf_out.dtype)
            sb_out[0, nt - 1 - t, :, sl] = (sb[p, :hd, :] + sb[p, hd:, :]).astype(sb_out.dtype)
        kf = (k_f[rf, :].astype(F32) * kdf_ref[...]).astype(BF16)
        kb = (k_b[rb, :].astype(F32) * kdb_ref[...]).astype(BF16)
        for p in range(pairs):
            sl = slice(p * LANES, (p + 1) * LANES)
            new_f = jnp.where(same_head, _dot_tn(kf[:, sl], v_f[rf, sl]), 0.0)
            new_b = jnp.where(same_head, _dot_tn(kb[:, sl], v_b[rb, sl]), 0.0)
            sf[p] = sf[p] * gl_ref[:, sl] + new_f
            sb[p] = sb[p] * gl_ref[:, sl] + new_b


def _ret_states(k, v, kdf, kdb, gl, batch, seq):
    nt = STEP_CHUNKS
    l = RET_CHUNK * nt
    nb = seq // l
    pairs = RET_WIDTH // LANES
    fwd = pl.BlockSpec((l, RET_WIDTH), lambda b, c: (b * nb + c, 0))
    bwd = pl.BlockSpec((l, RET_WIDTH), lambda b, c: (b * nb + nb - 1 - c, 0))
    st = (batch, nb * nt, RET_HEAD_DIM, RET_WIDTH)
    return pl.pallas_call(
        _ret_state_kernel,
        grid=(batch, nb),
        in_specs=[fwd, fwd, bwd, bwd, _resident(kdf.shape), _resident(kdb.shape), _resident(gl.shape)],
        out_specs=[pl.BlockSpec((1, nt, RET_HEAD_DIM, RET_WIDTH), lambda b, c: (b, c, 0, 0)),
                   pl.BlockSpec((1, nt, RET_HEAD_DIM, RET_WIDTH), lambda b, c: (b, nb - 1 - c, 0, 0))],
        out_shape=[jax.ShapeDtypeStruct(st, BF16)] * 2,
        scratch_shapes=[pltpu.VMEM((pairs, LANES, LANES), F32)] * 2,
        compiler_params=_cparams("parallel", "arbitrary"),
        name="ret_states",
    )(k, v, k, v, kdf, kdb, gl)


def _ret_out_kernel(q_ref, k_ref, v_ref, g_ref, sf_ref, sb_ref, dm_ref, qdf_ref, qdb_ref, ng_ref, o_ref):
    l = RET_CHUNK
    hd = RET_HEAD_DIM
    low = lax.broadcasted_iota(jnp.int32, (l, LANES), 1) < hd

    def block_diag(t):
        zero = jnp.zeros_like(t)
        m = lax.broadcasted_iota(jnp.int32, t.shape, 1) < hd
        return jnp.concatenate([jnp.where(m, t, zero), jnp.where(m, zero, t)], axis=0)

    for t in range(q_ref.shape[0] // l):
        rows = slice(t * l, (t + 1) * l)
        q = q_ref[rows, :]
        qf = q.astype(F32)
        q_lf = (qf * qdf_ref[...]).astype(BF16)
        q_lb = (qf * qdb_ref[...]).astype(BF16)
        ys = []
        for p in range(RET_WIDTH // LANES):
            sl = slice(p * LANES, (p + 1) * LANES)
            s = _dot_nt(q[:, sl], block_diag(k_ref[rows, sl]))
            y = _dot((s * dm_ref[p]).astype(BF16), block_diag(v_ref[rows, sl]))
            state = jnp.concatenate([block_diag(sf_ref[0, t, :, sl]), block_diag(sb_ref[0, t, :, sl])],
                                    axis=0)
            y = y + _dot(jnp.concatenate([q_lf[:, sl], q_lb[:, sl]], axis=1), state)
            y2 = y * y
            ms_a = jnp.sum(jnp.where(low, y2, 0.0), axis=-1, keepdims=True) * (1.0 / hd)
            ms_b = jnp.sum(jnp.where(low, 0.0, y2), axis=-1, keepdims=True) * (1.0 / hd)
            ys.append(y * jnp.where(low, lax.rsqrt(ms_a + EPS), lax.rsqrt(ms_b + EPS)))
        y = jnp.concatenate(ys, axis=1) * ng_ref[...]
        o_ref[rows, :] = (_silu(g_ref[rows, :].astype(F32)) * y).astype(o_ref.dtype)


def _ret_out(q, k, v, g, sf, sb, dm, qdf, qdb, ng, batch, seq):
    n = q.shape[0]
    nt = STEP_CHUNKS
    l = RET_CHUNK * nt
    nc = seq // l
    pairs = RET_WIDTH // LANES
    row = pl.BlockSpec((l, RET_WIDTH), lambda b, c: (b * nc + c, 0))
    st = pl.BlockSpec((1, nt, RET_HEAD_DIM, RET_WIDTH), lambda b, c: (b, c, 0, 0))
    return pl.pallas_call(
        _ret_out_kernel,
        grid=(batch, nc),
        in_specs=[row, row, row, row, st, st,
                  _resident(dm.shape), _resident(qdf.shape), _resident(qdb.shape), _resident((1, RET_WIDTH))],
        out_specs=row,
        out_shape=jax.ShapeDtypeStruct((n, RET_WIDTH), BF16),
        compiler_params=_cparams("parallel", "parallel"),
        name="ret_out",
    )(q, k, v, g, sf, sb, dm, qdf, qdb, ng)


def _ret_tables(seq):
    l = RET_CHUNK
    hd = RET_HEAD_DIM
    pos = np.arange(seq, dtype=np.float64)
    inv_freq = ROPE_BASE ** (-np.arange(0, hd, 2, dtype=np.float64) / hd)
    ang = pos[:, None] * inv_freq[None, :]
    cos = np.tile(np.cos(ang), (1, LANES // (hd // 2)))
    sin = np.sin(ang)
    sin_s = np.tile(np.concatenate([-sin, sin], axis=1), (1, LANES // hd))
    log_gamma = np.log1p(-np.exp2(-5.0 - np.arange(RET_HEADS, dtype=np.float64)))
    idx = np.arange(l, dtype=np.float64)
    dm = np.exp(log_gamma[:, None, None] * np.abs(idx[:, None] - idx[None, :]))
    dm = np.concatenate([dm[0::2], dm[1::2]], axis=2)
    per_head = lambda t: np.repeat(t, hd, axis=1)
    kdf = per_head(np.exp(log_gamma[None, :] * (l - 1.0 - idx)[:, None]))
    kdb = per_head(np.exp(log_gamma[None, :] * idx[:, None]))
    qdf = per_head(np.exp(log_gamma[None, :] * (idx + 1.0)[:, None]))
    qdb = per_head(np.exp(log_gamma[None, :] * (l - idx)[:, None]))
    gl = per_head(np.exp(log_gamma * l)[None, :])
    return tuple(jnp.asarray(t, F32) for t in (cos, sin_s, dm, kdf, kdb, qdf, qdb, gl))


def _ret_branch(q, k, v, g, norm_g, tables, batch, seq):
    _, _, dm, kdf, kdb, qdf, qdb, gl = tables
    sf, sb = _ret_states(k, v, kdf, kdb, gl, batch, seq)
    return _ret_out(q, k, v, g, sf, sb, dm, qdf, qdb, norm_g.reshape(1, RET_WIDTH), batch, seq)


def _merge_kernel(x_ref, ya_ref, yg_ref, yc_ref, g_ref, bgate_ref, wgate_ref, wa_ref, wv_ref, wgg_ref, wb_ref,
                  wc_ref, wo_ref, o_ref, *nat):
    x = x_ref[...]
    d = x.shape[1]
    h = _rms(x, g_ref[...]).astype(BF16)
    gates = []
    parts = S5_CHUNK * S5_GROUP_CH // LANES
    gw = d // parts
    for i in range(N_BRANCH):
        cols = []
        for a2 in range(parts):
            _s5_ungather_part(yg_ref, nat[i], i, a2)
            c0 = i * d + a2 * gw
            cols.append(jax.nn.sigmoid(_dot(h, wgate_ref[:, c0:c0 + gw]) + bgate_ref[:, c0:c0 + gw]))
        gates.append(jnp.concatenate(cols, axis=1))
    y5 = jnp.concatenate([t[...] for t in nat], axis=1).astype(BF16)
    yb = (_dot(y5, wv_ref[...]) * jax.nn.sigmoid(_dot(y5, wgg_ref[...]))).astype(BF16)
    branches = (_dot(ya_ref[...], wa_ref[...]), _dot(yb, wb_ref[...]), _dot(yc_ref[...], wc_ref[...]))
    mixed = gates[0] * branches[0] + gates[1] * branches[1] + gates[2] * branches[2]
    o_ref[...] = x + _dot(mixed.astype(BF16), wo_ref[...])


def _merge(x, ya, y_g, yc, g, bgate, stacked, layer):
    n, d = x.shape
    tm = WIDE_TOKEN_TILE
    row = lambda c: pl.BlockSpec((tm, c), lambda i: (i, 0))
    s5 = pl.BlockSpec((y_g.shape[0], tm // S5_CHUNK, y_g.shape[2]), lambda i: (0, i, 0))
    consts = (g.reshape(1, d), bgate.reshape(1, -1)) + tuple(stacked)
    return pl.pallas_call(
        _merge_kernel,
        grid=(n // tm,),
        in_specs=[row(d), row(ya.shape[1]), s5, row(yc.shape[1])] + [_resident(c.shape) for c in consts[:2]]
                 + [_resident_layer(w, layer) for w in stacked],
        out_specs=row(d),
        out_shape=jax.ShapeDtypeStruct((n, d), F32),
        scratch_shapes=[pltpu.VMEM((tm, LANES), F32)] * (S5_WIDTH // LANES),
        compiler_params=_cparams("parallel"),
        name="merge",
    )(x, ya, y_g, yc, *consts)


def _head_expander():
    return jnp.asarray(np.tile(np.repeat(np.eye(2 * SSD_HEADS), SSD_HEAD_DIM, axis=1), (3, 1)), BF16)


def kernel(x, ffn1_norm, ffn1_w_gate, ffn1_w_up, ffn1_w_down, mix_norm, w_in, b_gate, ssd_conv_w, ssd_conv_b, ssd_dt_bias, ssd_a_log, ssd_d, ssd_norm, w_br_ssd, s5_lam_re, s5_lam_im, s5_log_step, s5_b_re, s5_b_im, s5_c_re, s5_c_im, s5_d, s5_glu_wv, s5_glu_wg, w_br_s5, ret_norm, w_br_ret, w_out, ffn2_norm, ffn2_w_gate, ffn2_w_up, ffn2_w_down, final_norm):
    batch, seq, d = x.shape
    depth = w_in.shape[0]
    n = batch * seq
    assert seq % WIDE_TOKEN_TILE == 0 and seq % (STEP_CHUNKS * SSD_CHUNK) == 0 and seq % (STEP_CHUNKS * RET_CHUNK) == 0
    assert n % TOKEN_TILE == 0
    s5_steps = (seq // S5_CHUNK - 1).bit_length()
    assert 2 ** s5_steps == seq // S5_CHUNK
    bf = lambda w: w.astype(BF16)
    e = _head_expander()
    tables = _ret_tables(seq)
    offs = [0]
    for s in IN_PROJ_SIZES:
        offs.append(offs[-1] + s)
    xf = x.reshape(n, d).astype(F32)
    ffn1 = (bf(ffn1_w_gate), bf(ffn1_w_up), bf(ffn1_w_down))
    ffn2 = (bf(ffn2_w_gate), bf(ffn2_w_up), bf(ffn2_w_down))
    merge_w = (bf(w_br_ssd), bf(s5_glu_wv), bf(s5_glu_wg), bf(w_br_s5), bf(w_br_ret), bf(w_out))
    seg = [bf(w_in[:, :, offs[j]:offs[j + 1]]) for j in range(8)]
    seg[2] = jnp.pad(seg[2], ((0, 0), (0, 0), (0, LANES - seg[2].shape[2])))
    seg[3:4] = [seg[3][:, :, c:c + LANES] for c in range(0, S5_WIDTH, LANES)]
    merge_w = (bf(w_in[:, :, offs[8]:]),) + merge_w
    rope_scale = (None,) * 6 + (1.0, RET_HEAD_DIM ** -0.5, None, None)
    outputs = [(SSD_INNER, BF16), (SSD_INNER, BF16), (SSD_GROUPS * SSD_STATE, BF16), (SSD_GROUPS * SSD_STATE, BF16),
               ("dt_rows", F32), ("dt_rows", F32), ("dt_cols", F32), ("dt_cols", F32), ("s5", BF16),
               (RET_WIDTH, BF16), (RET_WIDTH, BF16), (RET_WIDTH, BF16), (RET_WIDTH, BF16)]
    dt_a = -jnp.exp(ssd_a_log.astype(F32))
    s5_mats = jax.vmap(functools.partial(_s5_matrices, nsteps=s5_steps))(
        s5_lam_re, s5_lam_im, s5_log_step, s5_b_re, s5_b_im, s5_c_re, s5_c_im, s5_d)
    for i in range(depth):
        xf = _ffn(xf, ffn1_norm[i], *ffn1, final_norm, False, i)
        z, xs, bm, cm, dt_r, cs_r, dt_c, cs_c, u_g, q, k, v, g = _inproj(
            xf, mix_norm[i], tables[0], tables[1], ssd_conv_w[i].astype(F32), ssd_conv_b[i].astype(F32),
            ssd_dt_bias[i].astype(F32), dt_a[i], seg, i, outputs, rope_scale, 1, 2, (3, 4, 5), seq)
        ya = _ssd_branch(z, xs, bm, cm, dt_r, cs_r, dt_c, cs_c, ssd_d[i], ssd_norm[i].astype(F32), e, batch, seq)
        yb = _s5_core(u_g, s5_mats, i, batch)
        yc = _ret_branch(q, k, v, g, ret_norm[i].astype(F32), tables, batch, seq)
        xf = _merge(xf, ya, yb, yc, mix_norm[i], b_gate[i].astype(F32), merge_w, i)
        xf = _ffn(xf, ffn2_norm[i], *ffn2, final_norm, i == depth - 1, i)
    return xf.reshape(batch, seq, d).astype(x.dtype)
```

```python
import functools
import math

import jax
import jax.numpy as jnp
import numpy as np
from jax import lax
from jax.experimental import pallas as pl
from jax.experimental.pallas import tpu as pltpu

F32 = jnp.float32
BF16 = jnp.bfloat16

EPS = 1e-6
LOG2E = 1.4426950408889634
SSD_HEADS = 8
SSD_HEAD_DIM = 64
SSD_INNER = 512
SSD_GROUPS = 2
SSD_STATE = 64
SSD_CONV = 5
SSD_CONV_CH = 768
S5_GROUP_CH = 16
S5_GROUPS = 24
S5_WIDTH = 384
S5_STATE = 64
S5_MAX_RE = -1e-4
RET_HEADS = 8
RET_HEAD_DIM = 64
RET_WIDTH = 512
ROPE_BASE = 10000.0
N_BRANCH = 3
IN_PROJ_SIZES = (512, 768, 16, 384, 512, 512, 512, 512, 3072)

V7X_VMEM_BYTES = 64 * 1024 * 1024
VMEM_LIMIT = V7X_VMEM_BYTES - 8 * 1024 * 1024
LANES = 128

TOKEN_TILE = 512
WIDE_TOKEN_TILE = 1024
SSD_CHUNK = 128
RET_CHUNK = 128
STEP_CHUNKS = 16
S5_CHUNK = 32


def _cparams(*sem):
    return pltpu.CompilerParams(dimension_semantics=sem, vmem_limit_bytes=VMEM_LIMIT)


def _resident(shape):
    nd = len(shape)
    return pl.BlockSpec(shape, lambda *_: (0,) * nd, pipeline_mode=pl.Buffered(1))


def _resident_layer(stacked, layer):
    nd = stacked.ndim - 1
    return pl.BlockSpec((None,) + stacked.shape[1:], lambda *_: (layer,) + (0,) * nd,
                        pipeline_mode=pl.Buffered(1))


def _rms(x, g):
    return (x * lax.rsqrt(jnp.mean(x * x, axis=-1, keepdims=True) + EPS)) * g


def _silu(x):
    return x * jax.nn.sigmoid(x)


def _dot(a, b):
    return jnp.dot(a, b, preferred_element_type=F32)


def _dot_nt(a, b):
    return lax.dot_general(a, b, (((1,), (1,)), ((), ())), preferred_element_type=F32)


def _dot_tn(a, b):
    return lax.dot_general(a, b, (((0,), (0,)), ((), ())), preferred_element_type=F32)


def _split3(v):
    hi = v.astype(BF16)
    r1 = v - hi.astype(F32)
    mid = r1.astype(BF16)
    lo = (r1 - mid.astype(F32)).astype(BF16)
    return hi, mid, lo


def _expand_heads(v, e3):
    return _dot(jnp.concatenate(_split3(v), axis=1), e3)


def _ffn_kernel(x_ref, g_ref, wg_ref, wu_ref, wd_ref, fg_ref, o_ref, *, final):
    x = x_ref[...]
    h = _rms(x, g_ref[...]).astype(BF16)
    a = _dot(h, wg_ref[...])
    b = _dot(h, wu_ref[...])
    t = (_silu(a) * b).astype(BF16)
    y = x + 0.5 * _dot(t, wd_ref[...])
    if final:
        y = _rms(y, fg_ref[...])
    o_ref[...] = y


def _ffn(x, g, wg, wu, wd, fg, final, layer):
    n, d = x.shape
    tm = TOKEN_TILE
    row = pl.BlockSpec((tm, d), lambda i: (i, 0))
    return pl.pallas_call(
        functools.partial(_ffn_kernel, final=final),
        grid=(n // tm,),
        in_specs=[row, _resident((1, d)), _resident_layer(wg, layer), _resident_layer(wu, layer),
                  _resident_layer(wd, layer), _resident((1, d))],
        out_specs=row,
        out_shape=jax.ShapeDtypeStruct((n, d), F32),
        compiler_params=_cparams("parallel"),
        name="ffn",
    )(x, g.reshape(1, d), wg, wu, wd, fg.reshape(1, d))


HALO = 16


def _inproj_kernel(xp_ref, x_ref, xn_ref, g_ref, cos_ref, sin_ref, cw_ref, cb_ref, dtb_ref, dta_ref, *refs,
                   rope_scale, conv_at, dt_at, gather_at, tiles_per_seq):
    nw = len(rope_scale)
    u_tiles = refs[len(refs) - len(gather_at):]
    outs = list(refs[nw:len(refs) - len(gather_at)])
    pending = []
    i = pl.program_id(0)
    t = x_ref.shape[0]
    he = _rms(jnp.concatenate([xp_ref[...], x_ref[...], xn_ref[...]], axis=0), g_ref[...]).astype(BF16)
    h = he[HALO:HALO + t]
    for j, (w_ref, scale) in enumerate(zip(refs[:nw], rope_scale)):
        if j == conv_at:
            ext = _dot(he, w_ref[...])
            rows = t + 2 * HALO
            r = lax.broadcasted_iota(jnp.int32, (rows, 1), 0)
            lo = jnp.where(i % tiles_per_seq == 0, HALO, 0)
            hi = jnp.where(i % tiles_per_seq == tiles_per_seq - 1, t + HALO, rows)
            ext = jnp.where(r >= lo, jnp.where(r < hi, ext, 0.0), 0.0)

            def conv_tile(o_ref, c_out, c_in, ext=ext, rows=rows):
                cols = slice(c_in, c_in + LANES)
                e = ext[:, cols]
                acc = jnp.broadcast_to(cb_ref[:, cols], (t, LANES))
                half = SSD_CONV // 2
                for k in range(SSD_CONV):
                    shifted = pltpu.roll(e, (half - k) % rows, 0)
                    acc = acc + cw_ref[k:k + 1, cols] * shifted[HALO:HALO + t]
                o_ref[:, c_out:c_out + LANES] = _silu(acc).astype(o_ref.dtype)

            c_in = 0
            for o_ref in (outs.pop(0), outs.pop(0), outs.pop(0)):
                for c_out in range(0, o_ref.shape[1], LANES):
                    pending.append(functools.partial(conv_tile, o_ref, c_out, c_in))
                    c_in += LANES
            continue
        if pending:
            pending.pop(0)()
        y = _dot(h, w_ref[...])
        if scale is not None:
            y = _rotary(y, cos_ref[...], sin_ref[...])
            if scale != 1.0:
                y = y * scale
        if j == dt_at:
            u_tiles[0][...] = y
            _ssd_dt_body(u_tiles[0][...], dtb_ref, dta_ref, outs.pop(0), outs.pop(0), outs.pop(0), outs.pop(0),
                         SSD_CHUNK)
            continue
        if j in gather_at:
            u_tiles[gather_at.index(j)][...] = y
            if j == gather_at[-1]:
                _s5_gather_kernel(*u_tiles, outs.pop(0))
            continue
        o_ref = outs.pop(0)
        o_ref[...] = y.astype(o_ref.dtype)
    for task in pending:
        task()


def _inproj(x, g, cos, sin_s, conv_w, conv_b, dt_bias, dt_a, weights, layer, outputs, rope_scale, conv_at, dt_at,
            gather_at, seq):
    n, d = x.shape
    tm = WIDE_TOKEN_TILE
    nh = 2 * SSD_HEADS
    s5w = S5_CHUNK * S5_GROUP_CH
    hb = tm // HALO
    nblk = n // HALO
    row = lambda c: pl.BlockSpec((tm, c), lambda i: (i, 0))
    pos = pl.BlockSpec((tm, LANES), lambda i: (i % (seq // tm), 0))
    prev = pl.BlockSpec((HALO, d), lambda i: (jnp.maximum(i * hb - 1, 0), 0))
    nxt = pl.BlockSpec((HALO, d), lambda i: (jnp.minimum((i + 1) * hb, nblk - 1), 0))
    special = {"s5": (pl.BlockSpec((S5_GROUPS, tm // S5_CHUNK, s5w), lambda i: (0, i, 0)),
                      (S5_GROUPS, n // S5_CHUNK, s5w)),
               "dt_rows": (pl.BlockSpec((nh, tm), lambda i: (0, i)), (nh, n)),
               "dt_cols": (pl.BlockSpec((tm, nh), lambda i: (i, 0)), (n, nh))}
    specs = [special[k][0] if k in special else row(k) for k, _ in outputs]
    shapes = [jax.ShapeDtypeStruct(special[k][1] if k in special else (n, k), dt) for k, dt in outputs]
    return pl.pallas_call(
        functools.partial(_inproj_kernel, rope_scale=rope_scale, conv_at=conv_at, dt_at=dt_at,
                          gather_at=gather_at, tiles_per_seq=seq // tm),
        grid=(n // tm,),
        in_specs=[prev, row(d), nxt, _resident((1, d)), pos, pos, _resident(conv_w.shape),
                  _resident((1, conv_b.shape[0])), _resident((nh, 1)), _resident((nh, 1))]
                 + [_resident_layer(w, layer) for w in weights],
        out_specs=specs,
        out_shape=shapes,
        scratch_shapes=[pltpu.VMEM((tm, LANES), F32)] * len(gather_at),
        compiler_params=_cparams("parallel"),
        name="inproj",
    )(x, x, x, g.reshape(1, d), cos, sin_s, conv_w, conv_b.reshape(1, -1), dt_bias.reshape(nh, 1),
      dt_a.reshape(nh, 1), *weights)


def _ssd_dt_body(raw, bias_ref, a_ref, dt_ref, cs_ref, dtc_ref, csc_ref, chunk):
    nh = 2 * SSD_HEADS
    x = raw.T[:nh] + bias_ref[...]
    dt = jnp.maximum(x, 0.0) + jnp.log1p(jnp.exp(-jnp.abs(x)))
    cs = dt * a_ref[...]
    t = x.shape[1]
    pos = lax.broadcasted_iota(jnp.int32, x.shape, 1) % chunk
    fwd = lax.broadcasted_iota(jnp.int32, x.shape, 0) < SSD_HEADS
    k = 1
    while k < chunk:
        before = jnp.where(pos >= k, pltpu.roll(cs, k, 1), 0.0)
        after = jnp.where(pos < chunk - k, pltpu.roll(cs, t - k, 1), 0.0)
        cs = cs + jnp.where(fwd, before, after)
        k *= 2
    dt_ref[...] = dt
    cs_ref[...] = cs
    both = jnp.concatenate([dt, cs, jnp.zeros((LANES - 2 * nh, t), F32)], axis=0).T
    dtc_ref[...] = both[:, :nh]
    csc_ref[...] = both[:, nh:2 * nh]


def _ssd_chunk_state(xs, bm, wexp):
    xw = (xs.astype(F32) * wexp).astype(BF16)
    gw = SSD_INNER // SSD_GROUPS
    parts = [_dot_tn(bm[:, g * SSD_STATE:(g + 1) * SSD_STATE], xw[:, g * gw:(g + 1) * gw])
             for g in range(SSD_GROUPS)]
    return jnp.concatenate(parts, axis=1)


def _ssd_state_kernel(xs_f, bm_f, dtc_f, csc_f, xs_b, bm_b, dtc_b, csc_b, e_ref,
                      sf_out, sb_out, sf, sb):
    c = pl.program_id(1)
    h = SSD_HEADS
    l = SSD_CHUNK
    nt = xs_f.shape[0] // l

    @pl.when(c == 0)
    def _():
        sf[...] = jnp.zeros_like(sf)
        sb[...] = jnp.zeros_like(sb)

    lane = lax.broadcasted_iota(jnp.int32, (1, 2 * h), 1)
    e_f = e_ref[:, :SSD_INNER]
    e_b = e_ref[:, SSD_INNER:]
    for t in range(nt):
        rf = slice(t * l, (t + 1) * l)
        rb = slice((nt - 1 - t) * l, (nt - t) * l)
        sf_out[0, t] = sf[...].astype(sf_out.dtype)
        sb_out[0, nt - 1 - t] = sb[...].astype(sb_out.dtype)
        cs_f = csc_f[rf, :]
        tot_f = cs_f[l - 1:l, :]
        w_f = jnp.exp(tot_f - cs_f) * dtc_f[rf, :]
        cs_b = csc_b[rb, :]
        tot_b = cs_b[0:1, :]
        w_b = jnp.exp(tot_b - cs_b) * dtc_b[rb, :]
        wexp_f = _expand_heads(jnp.where(lane < h, w_f, 0.0), e_f)
        wexp_b = _expand_heads(jnp.where(lane >= h, w_b, 0.0), e_b)
        dec_f = _expand_heads(jnp.broadcast_to(jnp.exp(tot_f), (8, 2 * h)), e_f)[0:1]
        dec_b = _expand_heads(jnp.broadcast_to(jnp.exp(tot_b), (8, 2 * h)), e_b)[0:1]
        sf[...] = sf[...] * dec_f + _ssd_chunk_state(xs_f[rf, :], bm_f[rf, :], wexp_f)
        sb[...] = sb[...] * dec_b + _ssd_chunk_state(xs_b[rb, :], bm_b[rb, :], wexp_b)


def _ssd_states(xs, bm, dt_c, cs_c, e, batch, seq):
    l = SSD_CHUNK
    nt = STEP_CHUNKS
    nb = seq // (l * nt)
    fwd = lambda w: pl.BlockSpec((nt * l, w), lambda b, c: (b * nb + c, 0))
    bwd = lambda w: pl.BlockSpec((nt * l, w), lambda b, c: (b * nb + nb - 1 - c, 0))
    st = (batch, nb * nt, SSD_STATE, SSD_INNER)
    return pl.pallas_call(
        _ssd_state_kernel,
        grid=(batch, nb),
        in_specs=[fwd(SSD_INNER), fwd(LANES), fwd(16), fwd(16),
                  bwd(SSD_INNER), bwd(LANES), bwd(16), bwd(16), _resident(e.shape)],
        out_specs=[pl.BlockSpec((1, nt, SSD_STATE, SSD_INNER), lambda b, c: (b, c, 0, 0)),
                   pl.BlockSpec((1, nt, SSD_STATE, SSD_INNER), lambda b, c: (b, nb - 1 - c, 0, 0))],
        out_shape=[jax.ShapeDtypeStruct(st, BF16)] * 2,
        scratch_shapes=[pltpu.VMEM((SSD_STATE, SSD_INNER), F32)] * 2,
        compiler_params=_cparams("parallel", "arbitrary"),
        name="ssd_states",
    )(xs, bm, dt_c, cs_c, xs, bm, dt_c, cs_c, e)


def _ssd_out_kernel(xs_ref, bm_ref, cm_ref, z_ref, dtc_ref, csc_ref, dtr_ref, csr_ref,
                    sf_ref, sb_ref, e_ref, dskip_ref, ng_ref, o_ref):
    h = SSD_HEADS
    l = SSD_CHUNK
    gw = SSD_INNER // SSD_GROUPS
    ri = lax.broadcasted_iota(jnp.int32, (l, l), 0)
    ci = lax.broadcasted_iota(jnp.int32, (l, l), 1)
    causal = ri >= ci
    diag = ri == ci
    low = lax.broadcasted_iota(jnp.int32, (l, LANES), 1) < LANES // 2

    def block_diag(t):
        zero = jnp.zeros_like(t)
        return jnp.concatenate([jnp.where(low, t, zero), jnp.where(low, zero, t)], axis=0)

    for t in range(xs_ref.shape[0] // l):
        rows = slice(t * l, (t + 1) * l)
        xs = xs_ref[rows, :]
        cm = cm_ref[rows, :]
        csc = csc_ref[rows, :]
        dtr = dtr_ref[:, rows]
        col = csc * LOG2E
        row = jnp.log2(dtr) - csr_ref[:, rows] * LOG2E
        scores = _dot_nt(cm, block_diag(bm_ref[rows, :]))
        ys = []
        for pair in range(h // 2):
            g = pair // (h // 2 // SSD_GROUPS)
            sc = scores[:, g * l:(g + 1) * l]
            ms = []
            for hh in (2 * pair, 2 * pair + 1):
                w = jnp.exp2(jnp.where(causal, col[:, hh:hh + 1] + row[hh:hh + 1, :],
                                       col[:, h + hh:h + hh + 1] + row[h + hh:h + hh + 1, :]))
                w = w + jnp.where(diag, dtr[h + hh:h + hh + 1, :], 0.0)
                ms.append((sc * w).astype(BF16))
            ys.append(_dot(jnp.concatenate(ms, axis=1), block_diag(xs[:, pair * LANES:(pair + 1) * LANES])))
        y = jnp.concatenate(ys, axis=1)
        eo = _expand_heads(jnp.exp(csc), e_ref[...])
        off_f = []
        off_b = []
        for g in range(SSD_GROUPS):
            cg = cm[:, g * SSD_STATE:(g + 1) * SSD_STATE]
            off_f.append(_dot(cg, sf_ref[0, t, :, g * gw:(g + 1) * gw]))
            off_b.append(_dot(cg, sb_ref[0, t, :, g * gw:(g + 1) * gw]))
        y = (y + jnp.concatenate(off_f, axis=1) * eo[:, :SSD_INNER]
             + jnp.concatenate(off_b, axis=1) * eo[:, SSD_INNER:])
        y = y + dskip_ref[...] * xs.astype(F32)
        y = y * _silu(z_ref[rows, :].astype(F32))
        o_ref[rows, :] = _rms(y, ng_ref[...]).astype(o_ref.dtype)


def _ssd_out(xs, bm, cm, z, dt_c, cs_c, dt_r, cs_r, sf, sb, e, dskip, ng, batch, seq):
    n = xs.shape[0]
    nt = STEP_CHUNKS
    l = SSD_CHUNK * nt
    nc = seq // l
    row = lambda w: pl.BlockSpec((l, w), lambda b, c: (b * nc + c, 0))
    col = pl.BlockSpec((16, l), lambda b, c: (0, b * nc + c))
    st = pl.BlockSpec((1, nt, SSD_STATE, SSD_INNER), lambda b, c: (b, c, 0, 0))
    return pl.pallas_call(
        _ssd_out_kernel,
        grid=(batch, nc),
        in_specs=[row(SSD_INNER), row(LANES), row(LANES), row(SSD_INNER), row(16), row(16), col, col,
                  st, st, _resident(e.shape), _resident((1, SSD_INNER)), _resident((1, SSD_INNER))],
        out_specs=row(SSD_INNER),
        out_shape=jax.ShapeDtypeStruct((n, SSD_INNER), BF16),
        compiler_params=_cparams("parallel", "parallel"),
        name="ssd_out",
    )(xs, bm, cm, z, dt_c, cs_c, dt_r, cs_r, sf, sb, e, dskip, ng)


def _ssd_branch(z, xs, bm, cm, dt_r, cs_r, dt_c, cs_c, d_skip, norm_g, e, batch, seq):
    sf, sb = _ssd_states(xs, bm, dt_c, cs_c, e, batch, seq)
    dskip = jnp.repeat(d_skip.astype(F32), SSD_HEAD_DIM).reshape(1, SSD_INNER)
    return _ssd_out(xs, bm, cm, z, dt_c, cs_c, dt_r, cs_r, sf, sb, e, dskip,
                    norm_g.reshape(1, SSD_INNER), batch, seq)


def _s5_matrices(lam_re, lam_im, log_step, b_re, b_im, c_re, c_im, d_s5, nsteps):
    q = S5_CHUNK
    lr = jnp.minimum(lam_re.astype(F32), S5_MAX_RE)
    li = lam_im.astype(F32)
    step = jnp.exp(log_step.astype(F32))[..., None]
    tau = jnp.arange(q + 1, dtype=F32)[:, None, None, None]
    mag = jnp.exp(lr * step * tau)
    ang = li * step * tau
    pr = mag * jnp.cos(ang)
    pi = mag * jnp.sin(ang)
    den = lr * lr + li * li
    nr = pr[1] - 1.0
    coef_re = ((nr * lr + pi[1] * li) / den)[..., None]
    coef_im = ((pi[1] * lr - nr * li) / den)[..., None]
    br = b_re.astype(F32)
    bi = b_im.astype(F32)
    bb_re = coef_re * br - coef_im * bi
    bb_im = coef_re * bi + coef_im * br
    cr = c_re.astype(F32)
    ci = c_im.astype(F32)

    def to_state(p_r, p_i, d):
        p_r = p_r.transpose(1, 0, 2)[:, :, None, :]
        p_i = p_i.transpose(1, 0, 2)[:, :, None, :]
        b_r = bb_re[d].transpose(0, 2, 1)[:, None]
        b_i = bb_im[d].transpose(0, 2, 1)[:, None]
        return p_r * b_r - p_i * b_i, p_r * b_i + p_i * b_r

    sf_r, sf_i = to_state(pr[:q, 0][::-1], pi[:q, 0][::-1], 0)
    sb_r, sb_i = to_state(pr[:q, 1], pi[:q, 1], 1)
    bmat = jnp.concatenate([sf_r, sb_r, sf_i, sb_i], axis=-1)
    bmat = bmat.reshape(S5_GROUPS, q * S5_GROUP_CH, 4 * S5_STATE)

    def from_state(p_r, p_i, d, pad=(0, 0)):
        lanes_of_t = lambda p: jnp.repeat(jnp.pad(p.transpose(1, 2, 0), ((0, 0), (0, 0), pad)), S5_GROUP_CH, axis=2)
        p_r, p_i = lanes_of_t(p_r), lanes_of_t(p_i)
        reps = p_r.shape[2] // S5_GROUP_CH
        c_r = jnp.tile(cr[d].transpose(0, 2, 1), (1, 1, reps))
        c_i = jnp.tile(ci[d].transpose(0, 2, 1), (1, 1, reps))
        return c_r * p_r - c_i * p_i, c_r * p_i + c_i * p_r

    gf_r, gf_i = from_state(pr[:q, 0], pi[:q, 0], 0, (q - 1, 1))
    gb_r, gb_i = from_state(pr[:q, 1][::-1], pi[:q, 1][::-1], 1, (0, q))
    gen_c = jnp.concatenate([gf_r, gf_i, gb_r, gb_i], axis=1)
    tr = lambda t: t.transpose(0, 2, 1)
    gen_b = jnp.concatenate([tr(bb_re[0]), -tr(bb_im[0]), tr(bb_re[1]), -tr(bb_im[1])], axis=-1)
    lane = lax.broadcasted_iota(jnp.int32, (S5_GROUP_CH, 2 * q * S5_GROUP_CH), 1)
    row = lax.broadcasted_iota(jnp.int32, (S5_GROUP_CH, 2 * q * S5_GROUP_CH), 0)
    d_gen = jnp.where(lane == (q - 1) * S5_GROUP_CH + row, d_s5.astype(F32)[:, :, None], 0.0)

    of_r, of_i = from_state(pr[1:q + 1, 0], pi[1:q + 1, 0], 0)
    ob_r, ob_i = from_state(pr[1:q + 1, 1][::-1], pi[1:q + 1, 1][::-1], 1)
    cmat = jnp.concatenate([of_r, ob_r, -of_i, -ob_i], axis=1)
    m = (q * 2.0 ** jnp.arange(nsteps, dtype=F32))[:, None, None, None]
    amag = jnp.exp(lr * step * m)
    aang = li * step * m
    lanes = lambda t: jnp.concatenate([t[:, 0], t[:, 1]], axis=-1).transpose(1, 0, 2)
    return (gen_b, gen_c, d_gen, bmat.astype(BF16), cmat.astype(BF16),
            lanes(amag * jnp.cos(aang)), lanes(amag * jnp.sin(aang)))


def _s5_gather_kernel(*refs):
    u_refs, o_ref = refs[:-1], refs[-1]
    q = S5_CHUNK
    rb = o_ref.shape[1]
    per = LANES // S5_GROUP_CH
    slot = lax.broadcasted_iota(jnp.int32, (rb, LANES), 1) // S5_GROUP_CH
    for a in range(S5_GROUPS // per):
        for a2 in range(q // per):
            acc = [None] * per
            for sl in range(per):
                src = u_refs[a][pl.ds(a2 * per + sl, rb, stride=q), :]
                for gl in range(per):
                    piece = src if sl == gl else pltpu.roll(src, ((sl - gl) * S5_GROUP_CH) % LANES, 1)
                    acc[gl] = piece if sl == 0 else jnp.where(slot == sl, piece, acc[gl])
            for gl in range(per):
                o_ref[a * per + gl, :, a2 * LANES:(a2 + 1) * LANES] = acc[gl].astype(o_ref.dtype)


def _dot_f32(a, b):
    a0, a1, a2 = _split3(a)
    b0, b1, b2 = _split3(b)
    return ((_dot(a0, b0) + (_dot(a0, b1) + _dot(a1, b0)))
            + ((_dot(a0, b2) + _dot(a2, b0)) + _dot(a1, b1)))


def _s5_core_kernel(u_ref, gb_ref, gc_ref, gd_ref, b_ref, c_ref, apr_ref, api_ref, o_ref, toe, *, nc):
    q = S5_CHUNK
    w = q * S5_GROUP_CH
    r = u_ref.shape[1]
    p2 = 2 * S5_STATE
    kc = _dot_f32(gb_ref[0], gc_ref[0]) + gd_ref[0]
    for s in range(q):
        off = (q - 1 - s) * S5_GROUP_CH
        win = kc if off == 0 else pltpu.roll(kc, 2 * w - off, 1)
        toe[s * S5_GROUP_CH:(s + 1) * S5_GROUP_CH, :] = win[:, :w].astype(BF16)
    u = u_ref[0]
    loc = _dot(u, b_ref[0])
    xr, xi = loc[:, :p2], loc[:, p2:]
    row = lax.broadcasted_iota(jnp.int32, (r, p2), 0) % nc
    fwd = lax.broadcasted_iota(jnp.int32, (r, p2), 1) < S5_STATE

    def shift(v, k):
        dn = jnp.where(row >= k, pltpu.roll(v, k, 0), 0.0)
        up = jnp.where(row < nc - k, pltpu.roll(v, r - k, 0), 0.0)
        return jnp.where(fwd, dn, up)

    k, i = 1, 0
    while k < nc:
        sr, si = shift(xr, k), shift(xi, k)
        ar, ai = apr_ref[0, i:i + 1, :], api_ref[0, i:i + 1, :]
        xr, xi = xr + (ar * sr - ai * si), xi + (ar * si + ai * sr)
        k, i = 2 * k, i + 1
    h = jnp.concatenate([shift(xr, 1), shift(xi, 1)], axis=1).astype(BF16)
    y = _dot(u, toe[...]) + _dot(h, c_ref[0])
    o_ref[0] = jax.nn.gelu(y, approximate=True).astype(o_ref.dtype)


def _s5_core(u_g, mats, layer, batch):
    g, r, w = u_g.shape
    blk = pl.BlockSpec((1, r, w), lambda i: (i, 0, 0))
    per_group = lambda a: pl.BlockSpec((None, 1) + a.shape[2:], lambda i: (layer, i, 0, 0))
    return pl.pallas_call(
        functools.partial(_s5_core_kernel, nc=r // batch),
        grid=(g,),
        in_specs=[blk] + [per_group(a) for a in mats],
        out_specs=blk,
        out_shape=jax.ShapeDtypeStruct((g, r, w), BF16),
        scratch_shapes=[pltpu.VMEM((w, w), BF16)],
        compiler_params=_cparams("parallel"),
        name="s5_core",
    )(u_g, *mats)


def _s5_ungather_part(y_ref, nat_a, a, a2):
    q = S5_CHUNK
    rb = y_ref.shape[1]
    per = LANES // S5_GROUP_CH
    slot = lax.broadcasted_iota(jnp.int32, (rb, LANES), 1) // S5_GROUP_CH
    acc = [None] * per
    for gl in range(per):
        src = y_ref[a * per + gl, :, a2 * LANES:(a2 + 1) * LANES].astype(F32)
        for sl in range(per):
            piece = src if gl == sl else pltpu.roll(src, ((gl - sl) * S5_GROUP_CH) % LANES, 1)
            acc[sl] = piece if gl == 0 else jnp.where(slot == gl, piece, acc[sl])
    for sl in range(per):
        nat_a[pl.ds(a2 * per + sl, rb, stride=q), :] = acc[sl]


def _rotary(t, cos, sin_signed):
    w = t.shape[1]
    half = RET_HEAD_DIM // 2
    lane = lax.broadcasted_iota(jnp.int32, t.shape, 1) % RET_HEAD_DIM
    partner = jnp.where(lane < half, pltpu.roll(t, w - half, 1), pltpu.roll(t, half, 1))
    reps = w // cos.shape[1]
    return t * jnp.tile(cos, (1, reps)) + partner * jnp.tile(sin_signed, (1, reps))


def _ret_state_kernel(k_f, v_f, k_b, v_b, kdf_ref, kdb_ref, gl_ref, sf_out, sb_out, sf, sb):
    c = pl.program_id(1)
    l = RET_CHUNK
    nt = k_f.shape[0] // l
    pairs = RET_WIDTH // LANES

    @pl.when(c == 0)
    def _():
        sf[...] = jnp.zeros_like(sf)
        sb[...] = jnp.zeros_like(sb)

    same_head = (lax.broadcasted_iota(jnp.int32, (LANES, LANES), 0) // RET_HEAD_DIM
                 == lax.broadcasted_iota(jnp.int32, (LANES, LANES), 1) // RET_HEAD_DIM)
    for t in range(nt):
        rf = slice(t * l, (t + 1) * l)
        rb = slice((nt - 1 - t) * l, (nt - t) * l)
        hd = RET_HEAD_DIM
        for p in range(pairs):
            sl = slice(p * LANES, (p + 1) * LANES)
            sf_out[0, t, :, sl] = (sf[p, :hd, :] + sf[p, hd:, :]).astype(sf_out.dtype)
            sb_out[0, nt - 1 - t, :, sl] = (sb[p, :hd, :] + sb[p, hd:, :]).astype(sb_out.dtype)
        kf = (k_f[rf, :].astype(F32) * kdf_ref[...]).astype(BF16)
        kb = (k_b[rb, :].astype(F32) * kdb_ref[...]).astype(BF16)
        for p in range(pairs):
            sl = slice(p * LANES, (p + 1) * LANES)
            new_f = jnp.where(same_head, _dot_tn(kf[:, sl], v_f[rf, sl]), 0.0)
            new_b = jnp.where(same_head, _dot_tn(kb[:, sl], v_b[rb, sl]), 0.0)
            sf[p] = sf[p] * gl_ref[:, sl] + new_f
            sb[p] = sb[p] * gl_ref[:, sl] + new_b


def _ret_states(k, v, kdf, kdb, gl, batch, seq):
    nt = STEP_CHUNKS
    l = RET_CHUNK * nt
    nb = seq // l
    pairs = RET_WIDTH // LANES
    fwd = pl.BlockSpec((l, RET_WIDTH), lambda b, c: (b * nb + c, 0))
    bwd = pl.BlockSpec((l, RET_WIDTH), lambda b, c: (b * nb + nb - 1 - c, 0))
    st = (batch, nb * nt, RET_HEAD_DIM, RET_WIDTH)
    return pl.pallas_call(
        _ret_state_kernel,
        grid=(batch, nb),
        in_specs=[fwd, fwd, bwd, bwd, _resident(kdf.shape), _resident(kdb.shape), _resident(gl.shape)],
        out_specs=[pl.BlockSpec((1, nt, RET_HEAD_DIM, RET_WIDTH), lambda b, c: (b, c, 0, 0)),
                   pl.BlockSpec((1, nt, RET_HEAD_DIM, RET_WIDTH), lambda b, c: (b, nb - 1 - c, 0, 0))],
        out_shape=[jax.ShapeDtypeStruct(st, BF16)] * 2,
        scratch_shapes=[pltpu.VMEM((pairs, LANES, LANES), F32)] * 2,
        compiler_params=_cparams("parallel", "arbitrary"),
        name="ret_states",
    )(k, v, k, v, kdf, kdb, gl)


def _ret_out_kernel(q_ref, k_ref, v_ref, g_ref, sf_ref, sb_ref, dm_ref, qdf_ref, qdb_ref, ng_ref, o_ref):
    l = RET_CHUNK
    hd = RET_HEAD_DIM
    low = lax.broadcasted_iota(jnp.int32, (l, LANES), 1) < hd

    def block_diag(t):
        zero = jnp.zeros_like(t)
        m = lax.broadcasted_iota(jnp.int32, t.shape, 1) < hd
        return jnp.concatenate([jnp.where(m, t, zero), jnp.where(m, zero, t)], axis=0)

    for t in range(q_ref.shape[0] // l):
        rows = slice(t * l, (t + 1) * l)
        q = q_ref[rows, :]
        qf = q.astype(F32)
        q_lf = (qf * qdf_ref[...]).astype(BF16)
        q_lb = (qf * qdb_ref[...]).astype(BF16)
        ys = []
        for p in range(RET_WIDTH // LANES):
            sl = slice(p * LANES, (p + 1) * LANES)
            s = _dot_nt(q[:, sl], block_diag(k_ref[rows, sl]))
            y = _dot((s * dm_ref[p]).astype(BF16), block_diag(v_ref[rows, sl]))
            state = jnp.concatenate([block_diag(sf_ref[0, t, :, sl]), block_diag(sb_ref[0, t, :, sl])],
                                    axis=0)
            y = y + _dot(jnp.concatenate([q_lf[:, sl], q_lb[:, sl]], axis=1), state)
            y2 = y * y
            ms_a = jnp.sum(jnp.where(low, y2, 0.0), axis=-1, keepdims=True) * (1.0 / hd)
            ms_b = jnp.sum(jnp.where(low, 0.0, y2), axis=-1, keepdims=True) * (1.0 / hd)
            ys.append(y * jnp.where(low, lax.rsqrt(ms_a + EPS), lax.rsqrt(ms_b + EPS)))
        y = jnp.concatenate(ys, axis=1) * ng_ref[...]
        o_ref[rows, :] = (_silu(g_ref[rows, :].astype(F32)) * y).astype(o_ref.dtype)


def _ret_out(q, k, v, g, sf, sb, dm, qdf, qdb, ng, batch, seq):
    n = q.shape[0]
    nt = STEP_CHUNKS
    l = RET_CHUNK * nt
    nc = seq // l
    pairs = RET_WIDTH // LANES
    row = pl.BlockSpec((l, RET_WIDTH), lambda b, c: (b * nc + c, 0))
    st = pl.BlockSpec((1, nt, RET_HEAD_DIM, RET_WIDTH), lambda b, c: (b, c, 0, 0))
    return pl.pallas_call(
        _ret_out_kernel,
        grid=(batch, nc),
        in_specs=[row, row, row, row, st, st,
                  _resident(dm.shape), _resident(qdf.shape), _resident(qdb.shape), _resident((1, RET_WIDTH))],
        out_specs=row,
        out_shape=jax.ShapeDtypeStruct((n, RET_WIDTH), BF16),
        compiler_params=_cparams("parallel", "parallel"),
        name="ret_out",
    )(q, k, v, g, sf, sb, dm, qdf, qdb, ng)


def _ret_tables(seq):
    l = RET_CHUNK
    hd = RET_HEAD_DIM
    pos = np.arange(seq, dtype=np.float64)
    inv_freq = ROPE_BASE ** (-np.arange(0, hd, 2, dtype=np.float64) / hd)
    ang = pos[:, None] * inv_freq[None, :]
    cos = np.tile(np.cos(ang), (1, LANES // (hd // 2)))
    sin = np.sin(ang)
    sin_s = np.tile(np.concatenate([-sin, sin], axis=1), (1, LANES // hd))
    log_gamma = np.log1p(-np.exp2(-5.0 - np.arange(RET_HEADS, dtype=np.float64)))
    idx = np.arange(l, dtype=np.float64)
    dm = np.exp(log_gamma[:, None, None] * np.abs(idx[:, None] - idx[None, :]))
    dm = np.concatenate([dm[0::2], dm[1::2]], axis=2)
    per_head = lambda t: np.repeat(t, hd, axis=1)
    kdf = per_head(np.exp(log_gamma[None, :] * (l - 1.0 - idx)[:, None]))
    kdb = per_head(np.exp(log_gamma[None, :] * idx[:, None]))
    qdf = per_head(np.exp(log_gamma[None, :] * (idx + 1.0)[:, None]))
    qdb = per_head(np.exp(log_gamma[None, :] * (l - idx)[:, None]))
    gl = per_head(np.exp(log_gamma * l)[None, :])
    return tuple(jnp.asarray(t, F32) for t in (cos, sin_s, dm, kdf, kdb, qdf, qdb, gl))


def _ret_branch(q, k, v, g, norm_g, tables, batch, seq):
    _, _, dm, kdf, kdb, qdf, qdb, gl = tables
    sf, sb = _ret_states(k, v, kdf, kdb, gl, batch, seq)
    return _ret_out(q, k, v, g, sf, sb, dm, qdf, qdb, norm_g.reshape(1, RET_WIDTH), batch, seq)


def _merge_kernel(x_ref, ya_ref, yg_ref, yc_ref, g_ref, bgate_ref, wgate_ref, wa_ref, wv_ref, wgg_ref, wb_ref,
                  wc_ref, wo_ref, o_ref, *nat):
    x = x_ref[...]
    d = x.shape[1]
    h = _rms(x, g_ref[...]).astype(BF16)
    gates = []
    parts = S5_CHUNK * S5_GROUP_CH // LANES
    gw = d // parts
    for i in range(N_BRANCH):
        cols = []
        for a2 in range(parts):
            _s5_ungather_part(yg_ref, nat[i], i, a2)
            c0 = i * d + a2 * gw
            cols.append(jax.nn.sigmoid(_dot(h, wgate_ref[:, c0:c0 + gw]) + bgate_ref[:, c0:c0 + gw]))
        gates.append(jnp.concatenate(cols, axis=1))
    y5 = jnp.concatenate([t[...] for t in nat], axis=1).astype(BF16)
    yb = (_dot(y5, wv_ref[...]) * jax.nn.sigmoid(_dot(y5, wgg_ref[...]))).astype(BF16)
    branches = (_dot(ya_ref[...], wa_ref[...]), _dot(yb, wb_ref[...]), _dot(yc_ref[...], wc_ref[...]))
    mixed = gates[0] * branches[0] + gates[1] * branches[1] + gates[2] * branches[2]
    o_ref[...] = x + _dot(mixed.astype(BF16), wo_ref[...])


def _merge(x, ya, y_g, yc, g, bgate, stacked, layer):
    n, d = x.shape
    tm = WIDE_TOKEN_TILE
    row = lambda c: pl.BlockSpec((tm, c), lambda i: (i, 0))
    s5 = pl.BlockSpec((y_g.shape[0], tm // S5_CHUNK, y_g.shape[2]), lambda i: (0, i, 0))
    consts = (g.reshape(1, d), bgate.reshape(1, -1)) + tuple(stacked)
    return pl.pallas_call(
        _merge_kernel,
        grid=(n // tm,),
        in_specs=[row(d), row(ya.shape[1]), s5, row(yc.shape[1])] + [_resident(c.shape) for c in consts[:2]]
                 + [_resident_layer(w, layer) for w in stacked],
        out_specs=row(d),
        out_shape=jax.ShapeDtypeStruct((n, d), F32),
        scratch_shapes=[pltpu.VMEM((tm, LANES), F32)] * (S5_WIDTH // LANES),
        compiler_params=_cparams("parallel"),
        name="merge",
    )(x, ya, y_g, yc, *consts)


def _head_expander():
    return jnp.asarray(np.tile(np.repeat(np.eye(2 * SSD_HEADS), SSD_HEAD_DIM, axis=1), (3, 1)), BF16)


def kernel(x, ffn1_norm, ffn1_w_gate, ffn1_w_up, ffn1_w_down, mix_norm, w_in, b_gate, ssd_conv_w, ssd_conv_b, ssd_dt_bias, ssd_a_log, ssd_d, ssd_norm, w_br_ssd, s5_lam_re, s5_lam_im, s5_log_step, s5_b_re, s5_b_im, s5_c_re, s5_c_im, s5_d, s5_glu_wv, s5_glu_wg, w_br_s5, ret_norm, w_br_ret, w_out, ffn2_norm, ffn2_w_gate, ffn2_w_up, ffn2_w_down, final_norm):
    batch, seq, d = x.shape
    depth = w_in.shape[0]
    n = batch * seq
    assert seq % WIDE_TOKEN_TILE == 0 and seq % (STEP_CHUNKS * SSD_CHUNK) == 0 and seq % (STEP_CHUNKS * RET_CHUNK) == 0
    assert n % TOKEN_TILE == 0
    s5_steps = (seq // S5_CHUNK - 1).bit_length()
    assert 2 ** s5_steps == seq // S5_CHUNK
    bf = lambda w: w.astype(BF16)
    e = _head_expander()
    tables = _ret_tables(seq)
    offs = [0]
    for s in IN_PROJ_SIZES:
        offs.append(offs[-1] + s)
    xf = x.reshape(n, d).astype(F32)
    ffn1 = (bf(ffn1_w_gate), bf(ffn1_w_up), bf(ffn1_w_down))
    ffn2 = (bf(ffn2_w_gate), bf(ffn2_w_up), bf(ffn2_w_down))
    merge_w = (bf(w_br_ssd), bf(s5_glu_wv), bf(s5_glu_wg), bf(w_br_s5), bf(w_br_ret), bf(w_out))
    seg = [bf(w_in[:, :, offs[j]:offs[j + 1]]) for j in range(8)]
    seg[2] = jnp.pad(seg[2], ((0, 0), (0, 0), (0, LANES - seg[2].shape[2])))
    seg[3:4] = [seg[3][:, :, c:c + LANES] for c in range(0, S5_WIDTH, LANES)]
    merge_w = (bf(w_in[:, :, offs[8]:]),) + merge_w
    rope_scale = (None,) * 6 + (1.0, RET_HEAD_DIM ** -0.5, None, None)
    outputs = [(SSD_INNER, BF16), (SSD_INNER, BF16), (SSD_GROUPS * SSD_STATE, BF16), (SSD_GROUPS * SSD_STATE, BF16),
               ("dt_rows", F32), ("dt_rows", F32), ("dt_cols", F32), ("dt_cols", F32), ("s5", BF16),
               (RET_WIDTH, BF16), (RET_WIDTH, BF16), (RET_WIDTH, BF16), (RET_WIDTH, BF16)]
    dt_a = -jnp.exp(ssd_a_log.astype(F32))
    s5_mats = jax.vmap(functools.partial(_s5_matrices, nsteps=s5_steps))(
        s5_lam_re, s5_lam_im, s5_log_step, s5_b_re, s5_b_im, s5_c_re, s5_c_im, s5_d)
    for i in range(depth):
        xf = _ffn(xf, ffn1_norm[i], *ffn1, final_norm, False, i)
        z, xs, bm, cm, dt_r, cs_r, dt_c, cs_c, u_g, q, k, v, g = _inproj(
            xf, mix_norm[i], tables[0], tables[1], ssd_conv_w[i].astype(F32), ssd_conv_b[i].astype(F32),
            ssd_dt_bias[i].astype(F32), dt_a[i], seg, i, outputs, rope_scale, 1, 2, (3, 4, 5), seq)
        ya = _ssd_branch(z, xs, bm, cm, dt_r, cs_r, dt_c, cs_c, ssd_d[i], ssd_norm[i].astype(F32), e, batch, seq)
        yb = _s5_core(u_g, s5_mats, i, batch)
        yc = _ret_branch(q, k, v, g, ret_norm[i].astype(F32), tables, batch, seq)
        xf = _merge(xf, ya, yb, yc, mix_norm[i], b_gate[i].astype(F32), merge_w, i)
        xf = _ffn(xf, ffn2_norm[i], *ffn2, final_norm, i == depth - 1, i)
    return xf.reshape(batch, seq, d).astype(x.dtype)
```
